```python
import math
import jax, jax.numpy as jnp
from jax import lax
import numpy as np

D_MODEL = 1024
BATCH = 8
SEQ = 2048
DEPTH = 4
DEC_BATCH = 128
DEC_SEQ = 1
PAST_LEN = 16384
PAGE_SIZE = 128

D_POOL = D_MODEL // 4
POOL_WINDOWS = (2, 4, 8, 16)
N_POOL_GROUPS = len(POOL_WINDOWS)
POOL_GROUP = D_POOL // N_POOL_GROUPS
POOL_BUF = max(POOL_WINDOWS) - 1
D_MLSTM = 3 * D_MODEL // 8
MLSTM_HEADS = 4
MLSTM_DH = D_MLSTM // MLSTM_HEADS
D_GLA = D_MODEL - D_POOL - D_MLSTM
GLA_HEADS = 4
GLA_DV = D_GLA // GLA_HEADS
GLA_DK = GLA_DV // 2
GLA_RANK = 16
GLA_TAU = 16.0
D_MIX = D_POOL + D_MLSTM + D_GLA
D_FF = -(-8 * D_MODEL // (3 * 256)) * 256
D_PLE = 256
CHUNK = 64
ALPHA = (2 * DEPTH) ** 0.25
BETA = (8 * DEPTH) ** -0.25
LN_EPS = 1e-5

kernel_name = 'hymba_pool_mlstm_gla_deepnorm_step'


def _split_sizes():
    return (D_POOL,
            D_MLSTM, D_MLSTM, D_MLSTM,
            MLSTM_HEADS, MLSTM_HEADS,
            D_MLSTM,
            GLA_HEADS * GLA_DK, GLA_HEADS * GLA_DK,
            D_GLA,
            GLA_RANK,
            D_GLA)


def _split_indices():
    return tuple(int(s) for s in np.cumsum(_split_sizes())[:-1])


def layer_norm(x, g, b):
    x32 = x.astype(jnp.float32)
    mu = jnp.mean(x32, axis=-1, keepdims=True)
    var = jnp.mean(jnp.square(x32 - mu), axis=-1, keepdims=True)
    return ((x32 - mu) * lax.rsqrt(var + LN_EPS) * g + b).astype(x.dtype)


def head_layer_norm(h):
    mu = jnp.mean(h, axis=-1, keepdims=True)
    var = jnp.mean(jnp.square(h - mu), axis=-1, keepdims=True)
    return (h - mu) * lax.rsqrt(var + LN_EPS)


def head_rms_norm(h):
    return h * lax.rsqrt(jnp.mean(jnp.square(h), axis=-1, keepdims=True) + LN_EPS)


def _to_chunks(a, L):
    B, T = a.shape[:2]
    a = a.reshape((B, T // L, L) + a.shape[2:])
    perm = (1, 0, 3, 2) + tuple(range(4, a.ndim))
    return a.transpose(perm)


def _from_chunks(a):
    NC, B, H, L, D = a.shape
    return a.transpose(1, 0, 3, 2, 4).reshape(B, NC * L, H, D)


def pool_mixer(u, buf, pos, w_mix, scale):
    B, T, _ = u.shape
    ext = jnp.concatenate([buf.astype(jnp.float32), u], axis=1)
    cs = jnp.concatenate([jnp.zeros((B, 1, D_POOL), jnp.float32), jnp.cumsum(ext, axis=1)], axis=1)
    outs = []
    for g, w in enumerate(POOL_WINDOWS):
        sl = slice(g * POOL_GROUP, (g + 1) * POOL_GROUP)
        wsum = cs[:, 1 + POOL_BUF:1 + POOL_BUF + T, sl] - cs[:, 1 + POOL_BUF - w:1 + POOL_BUF - w + T, sl]
        cnt = jnp.minimum(w, pos + 1).astype(jnp.float32)
        outs.append(wsum / cnt[None, :, None] - u[..., sl])
    d = jnp.stack(outs, axis=2)
    y = jnp.einsum('btgc,gcd->btgd', d, w_mix.astype(jnp.float32)).reshape(B, T, D_POOL)
    return y * scale, ext[:, -POOL_BUF:]


def mlstm_mixer(q, k, v, log_i, log_f, C0, n0, m0):
    T = q.shape[1]
    L = math.gcd(T, CHUNK)
    causal = jnp.tril(jnp.ones((L, L), dtype=bool))

    def step(carry, inp):
        C, n, m = carry
        qc, kc, vc, ic, fc = inp
        b = jnp.cumsum(fc, axis=-1)
        dmat = jnp.where(causal, b[..., :, None] - b[..., None, :] + ic[..., None, :], -jnp.inf)
        inter = b + m[..., None]
        m_t = jnp.maximum(inter, jnp.max(dmat, axis=-1))
        s = jnp.einsum('bhtd,bhsd->bhts', qc, kc) * jnp.exp(dmat - m_t[..., None])
        w_inter = jnp.exp(inter - m_t)
        num = w_inter[..., None] * jnp.einsum('bhtk,bhkv->bhtv', qc, C) + jnp.einsum('bhts,bhsv->bhtv', s, vc)
        nq = w_inter * jnp.einsum('bhtk,bhk->bht', qc, n) + jnp.sum(s, axis=-1)
        h = num / jnp.maximum(jnp.abs(nq), jnp.exp(-m_t))[..., None]
        m_new = m_t[..., -1]
        decay = jnp.exp(b[..., -1] + m - m_new)
        w_end = jnp.exp(b[..., -1:] - b + ic - m_new[..., None])
        C_new = decay[..., None, None] * C + jnp.einsum('bhs,bhsk,bhsv->bhkv', w_end, kc, vc)
        n_new = decay[..., None] * n + jnp.einsum('bhs,bhsk->bhk', w_end, kc)
        return (C_new, n_new, m_new), h

    xs = (_to_chunks(q, L), _to_chunks(k, L), _to_chunks(v, L), _to_chunks(log_i, L), _to_chunks(log_f, L))
    (C, n, m), h = lax.scan(step, (C0, n0, m0), xs)
    return _from_chunks(h), C, n, m


def gla_mixer(q, k, v, log_a, S0):
    T = q.shape[1]
    L = math.gcd(T, CHUNK)
    causal = jnp.tril(jnp.ones((L, L), dtype=bool))

    def step(S, inp):
        qc, kc, vc, ac = inp
        bc = jnp.cumsum(ac, axis=2)
        diff = jnp.where(causal[:, :, None], bc[:, :, :, None, :] - bc[:, :, None, :, :], -jnp.inf)
        A = jnp.einsum('bhtk,bhsk,bhtsk->bhts', qc, kc, jnp.exp(diff))
        o = jnp.einsum('bhtk,bhkv->bhtv', qc * jnp.exp(bc), S) + jnp.einsum('bhts,bhsv->bhtv', A, vc)
        last = bc[:, :, -1:, :]
        S_new = jnp.exp(last[:, :, 0, :])[..., None] * S + jnp.einsum('bhsk,bhsv->bhkv', kc * jnp.exp(last - bc), vc)
        return S_new, o

    xs = (_to_chunks(q, L), _to_chunks(k, L), _to_chunks(v, L), _to_chunks(log_a, L))
    S, o = lax.scan(step, S0, xs)
    return _from_chunks(o), S


def _layer(x, p, pool_buf, C, n, m, S, pos,
           w_in, mlstm_i_bias, mlstm_f_bias, mlstm_norm_w, gla_a2, gla_a_bias, gla_norm_w,
           pool_mix, pool_scale, w_out, ln1_g, ln1_b, w_gate, w_up, w_down, w_ple, w_ple_gate, ln2_g, ln2_b):
    B, T, _ = x.shape
    f32 = jnp.float32
    z = (x @ w_in).astype(f32)
    u, mq, mk, mv, mi, mf, mo, gq, gk, gv, ga, gg = jnp.split(z, _split_indices(), axis=-1)
    y_pool, pool_new = pool_mixer(u, pool_buf, pos, pool_mix, pool_scale.astype(f32))
    hs = (B, T, MLSTM_HEADS, MLSTM_DH)
    h, C_new, n_new, m_new = mlstm_mixer(
        mq.reshape(hs) * MLSTM_DH ** -0.5, mk.reshape(hs), mv.reshape(hs),
        mi + mlstm_i_bias.astype(f32), jax.nn.log_sigmoid(mf + mlstm_f_bias.astype(f32)),
        C.astype(f32), n.astype(f32), m.astype(f32))
    h = head_layer_norm(h) * mlstm_norm_w.astype(f32).reshape(MLSTM_HEADS, MLSTM_DH)
    y_m = jax.nn.sigmoid(mo) * h.reshape(B, T, D_MLSTM)
    ks = (B, T, GLA_HEADS, GLA_DK)
    log_a = jax.nn.log_sigmoid(ga @ gla_a2.astype(f32) + gla_a_bias.astype(f32)) / GLA_TAU
    o, S_new = gla_mixer(gq.reshape(ks) * GLA_DK ** -0.5, gk.reshape(ks),
                         gv.reshape(B, T, GLA_HEADS, GLA_DV), log_a.reshape(ks), S.astype(f32))
    o = head_rms_norm(o) * gla_norm_w.astype(f32).reshape(GLA_HEADS, GLA_DV)
    y_g = jax.nn.silu(gg) * o.reshape(B, T, D_GLA)
    mix = jnp.concatenate([y_pool, y_m, y_g], axis=-1).astype(x.dtype) @ w_out
    x = layer_norm(ALPHA * x + mix, ln1_g, ln1_b)
    ffn = (jax.nn.silu(x @ w_gate) * (x @ w_up)) @ w_down
    pe = (p @ w_ple) * jax.nn.sigmoid(x @ w_ple_gate)
    x = layer_norm(ALPHA * x + ffn + pe, ln2_g, ln2_b)
    return x, (pool_new.astype(pool_buf.dtype), C_new.astype(C.dtype), n_new.astype(n.dtype),
               m_new.astype(m.dtype), S_new.astype(S.dtype))


def _run_group(x, p, pool_buf, C, n, m, S, pos, weights):
    news = []
    for i in range(DEPTH):
        x, st = _layer(x, p[i], pool_buf[i], C[i], n[i], m[i], S[i], pos, *[w[i] for w in weights])
        news.append(st)
    stacked = tuple(jnp.stack([s[j] for s in news], axis=0) for j in range(5))
    return x, stacked


def setup_inputs(seed: int = 0) -> dict:
    key = jax.random.key(seed)
    ks = jax.random.split(key, 40)
    D_IN = sum(_split_sizes())

    def nrm(k, shape, scale):
        return jax.random.normal(k, shape, jnp.float32) * scale

    f_bias = jnp.tile(jnp.linspace(3.0, 6.0, MLSTM_HEADS, dtype=jnp.float32)[None], (DEPTH, 1))
    return {
        'x_prompt': nrm(ks[0], (BATCH, SEQ, D_MODEL), 1.0),
        'x_sample': nrm(ks[1], (DEC_BATCH, DEC_SEQ, D_MODEL), 1.0),
        'p_prompt': nrm(ks[2], (DEPTH, BATCH, SEQ, D_PLE), 1.0),
        'p_sample': nrm(ks[3], (DEPTH, DEC_BATCH, DEC_SEQ, D_PLE), 1.0),
        'state_pool': nrm(ks[4], (DEPTH, DEC_BATCH, POOL_BUF, D_POOL), 1.0),
        'state_mlstm_C': nrm(ks[5], (DEPTH, DEC_BATCH, MLSTM_HEADS, MLSTM_DH, MLSTM_DH), 0.1),
        'state_mlstm_n': nrm(ks[6], (DEPTH, DEC_BATCH, MLSTM_HEADS, MLSTM_DH), 0.1),
        'state_mlstm_m': nrm(ks[7], (DEPTH, DEC_BATCH, MLSTM_HEADS), 1.0),
        'state_gla_S': nrm(ks[8], (DEPTH, DEC_BATCH, GLA_HEADS, GLA_DK, GLA_DV), 0.3),
        'w_in': nrm(ks[9], (DEPTH, D_MODEL, D_IN), D_MODEL ** -0.5),
        'mlstm_i_bias': nrm(ks[10], (DEPTH, MLSTM_HEADS), 0.1),
        'mlstm_f_bias': f_bias + nrm(ks[11], (DEPTH, MLSTM_HEADS), 0.01),
        'mlstm_norm_w': 1.0 + nrm(ks[12], (DEPTH, D_MLSTM), 0.02),
        'gla_a2': nrm(ks[13], (DEPTH, GLA_RANK, GLA_HEADS * GLA_DK), GLA_RANK ** -0.5),
        'gla_a_bias': nrm(ks[14], (DEPTH, GLA_HEADS * GLA_DK), 0.01),
        'gla_norm_w': 1.0 + nrm(ks[15], (DEPTH, D_GLA), 0.02),
        'pool_mix': nrm(ks[16], (DEPTH, N_POOL_GROUPS, POOL_GROUP, POOL_GROUP), POOL_GROUP ** -0.5),
        'pool_scale': 1.0 + nrm(ks[17], (DEPTH, D_POOL), 0.02),
        'w_out': nrm(ks[18], (DEPTH, D_MIX, D_MODEL), D_MIX ** -0.5 * BETA),
        'ln1_g': 1.0 + nrm(ks[19], (DEPTH, D_MODEL), 0.02),
        'ln1_b': nrm(ks[20], (DEPTH, D_MODEL), 0.02),
        'w_gate': nrm(ks[21], (DEPTH, D_MODEL, D_FF), D_MODEL ** -0.5),
        'w_up': nrm(ks[22], (DEPTH, D_MODEL, D_FF), D_MODEL ** -0.5),
        'w_down': nrm(ks[23], (DEPTH, D_FF, D_MODEL), D_FF ** -0.5 * BETA),
        'w_ple': nrm(ks[24], (DEPTH, D_PLE, D_MODEL), D_PLE ** -0.5 * BETA),
        'w_ple_gate': nrm(ks[25], (DEPTH, D_MODEL, D_MODEL), D_MODEL ** -0.5),
        'ln2_g': 1.0 + nrm(ks[26], (DEPTH, D_MODEL), 0.02),
        'ln2_b': nrm(ks[27], (DEPTH, D_MODEL), 0.02),
    }


def reference(x_prompt, x_sample, p_prompt, p_sample, state_pool, state_mlstm_C, state_mlstm_n,
              state_mlstm_m, state_gla_S, w_in, mlstm_i_bias, mlstm_f_bias, mlstm_norm_w, gla_a2,
              gla_a_bias, gla_norm_w, pool_mix, pool_scale, w_out, ln1_g, ln1_b, w_gate, w_up, w_down,
              w_ple, w_ple_gate, ln2_g, ln2_b):
    weights = (w_in, mlstm_i_bias, mlstm_f_bias, mlstm_norm_w, gla_a2, gla_a_bias, gla_norm_w,
               pool_mix, pool_scale, w_out, ln1_g, ln1_b, w_gate, w_up, w_down, w_ple, w_ple_gate,
               ln2_g, ln2_b)
    Bp, Tp, _ = x_prompt.shape
    Bs, Ts, _ = x_sample.shape
    dt = state_pool.dtype
    pool0 = jnp.zeros((DEPTH, Bp, POOL_BUF, D_POOL), dt)
    C0 = jnp.zeros((DEPTH, Bp, MLSTM_HEADS, MLSTM_DH, MLSTM_DH), dt)
    n0 = jnp.zeros((DEPTH, Bp, MLSTM_HEADS, MLSTM_DH), dt)
    m0 = jnp.zeros((DEPTH, Bp, MLSTM_HEADS), dt)
    S0 = jnp.zeros((DEPTH, Bp, GLA_HEADS, GLA_DK, GLA_DV), dt)
    pos_p = jnp.arange(Tp, dtype=jnp.int32)
    pos_s = PAST_LEN + jnp.arange(Ts, dtype=jnp.int32)
    y_prompt, (pool_p, C_p, n_p, m_p, S_p) = _run_group(x_prompt, p_prompt, pool0, C0, n0, m0, S0, pos_p, weights)
    y_sample, (pool_s, C_s, n_s, m_s, S_s) = _run_group(
        x_sample, p_sample, state_pool, state_mlstm_C, state_mlstm_n, state_mlstm_m, state_gla_S, pos_s, weights)
    return (y_prompt, y_sample, pool_p, C_p, n_p, m_p, S_p, pool_s, C_s, n_s, m_s, S_s)
```

```python
import functools

import numpy as np
import jax
import jax.numpy as jnp
from jax import lax
from jax.experimental import pallas as pl
from jax.experimental.pallas import tpu as pltpu

F32 = jnp.float32
BF16 = jnp.bfloat16

D_MODEL = 1024
DEPTH = 4
PAST_LEN = 16384
D_POOL = 256
POOL_WINDOWS = (2, 4, 8, 16)
POOL_GROUP = 64
POOL_BUF = 15
D_MLSTM = 384
HEADS = 4
MLSTM_DH = 96
D_GLA = 384
GLA_DV = 96
GLA_DK = 48
GLA_RANK = 16
GLA_TAU = 16.0
D_MIX = 1024
D_FF = 2816
D_PLE = 256
CHUNK = 64
ALPHA = (2 * DEPTH) ** 0.25
LN_EPS = 1e-5

LANES = 128
SUBLANES = 8
VMEM_LIMIT = 56 * 1024 * 1024

HP = LANES
KP = 64
U_OFF = 0
MQ_OFF = U_OFF + D_POOL
MK_OFF = MQ_OFF + HEADS * HP
MV_OFF = MK_OFF + HEADS * HP
MO_OFF = MV_OFF + HEADS * HP
GQ_OFF = MO_OFF + HEADS * HP
GK_OFF = GQ_OFF + HEADS * KP
GV_OFF = GK_OFF + HEADS * KP
GG_OFF = GV_OFF + HEADS * HP
GATE_OFF = GG_OFF + HEADS * HP
N_Z = GATE_OFF + LANES
GATE_I = 0
GATE_F = 4
GATE_A = 8
YM_OFF = D_POOL
YG_OFF = YM_OFF + HEADS * HP
N_MIX = YG_OFF + HEADS * HP

TT = 512
SG = SUBLANES
FF_CHUNK = 512


def _z_column_map():
    src = -np.ones((N_Z,), np.int64)
    sizes = (D_POOL, D_MLSTM, D_MLSTM, D_MLSTM, HEADS, HEADS, D_MLSTM,
             HEADS * GLA_DK, HEADS * GLA_DK, D_GLA, GLA_RANK, D_GLA)
    starts = np.concatenate([[0], np.cumsum(sizes)[:-1]])
    (s_u, s_mq, s_mk, s_mv, s_mi, s_mf, s_mo, s_gq, s_gk, s_gv, s_ga, s_gg) = [int(s) for s in starts]
    src[U_OFF:U_OFF + D_POOL] = s_u + np.arange(D_POOL)
    for h in range(HEADS):
        for off, s in ((MQ_OFF, s_mq), (MK_OFF, s_mk), (MV_OFF, s_mv), (MO_OFF, s_mo),
                       (GV_OFF, s_gv), (GG_OFF, s_gg)):
            src[off + h * HP: off + h * HP + MLSTM_DH] = s + h * MLSTM_DH + np.arange(MLSTM_DH)
        for off, s in ((GQ_OFF, s_gq), (GK_OFF, s_gk)):
            src[off + h * KP: off + h * KP + GLA_DK] = s + h * GLA_DK + np.arange(GLA_DK)
    src[GATE_OFF + GATE_I: GATE_OFF + GATE_I + HEADS] = s_mi + np.arange(HEADS)
    src[GATE_OFF + GATE_F: GATE_OFF + GATE_F + HEADS] = s_mf + np.arange(HEADS)
    src[GATE_OFF + GATE_A: GATE_OFF + GATE_A + GLA_RANK] = s_ga + np.arange(GLA_RANK)
    return src


def _mix_row_map():
    src = -np.ones((N_MIX,), np.int64)
    src[:D_POOL] = np.arange(D_POOL)
    for h in range(HEADS):
        src[YM_OFF + h * HP: YM_OFF + h * HP + MLSTM_DH] = D_POOL + h * MLSTM_DH + np.arange(MLSTM_DH)
        src[YG_OFF + h * HP: YG_OFF + h * HP + GLA_DV] = D_POOL + D_MLSTM + h * GLA_DV + np.arange(GLA_DV)
    return src


def _gather_pad(a, src, axis):
    idx = jnp.asarray(np.maximum(src, 0), jnp.int32)
    keep = jnp.asarray(src >= 0)
    out = jnp.take(a, idx, axis=axis)
    shape = [1] * a.ndim
    shape[axis] = src.shape[0]
    return jnp.where(keep.reshape(shape), out, 0)


def _pad_heads(a, width, padded):
    lead = a.shape[:-1]
    a = a.reshape(lead + (HEADS, width))
    a = jnp.pad(a, [(0, 0)] * len(lead) + [(0, 0), (0, padded - width)])
    return a.reshape(lead + (HEADS * padded,))


def _dot(a, b):
    return jnp.dot(a, b, preferred_element_type=F32)


def _dot_nt(a, b):
    return lax.dot_general(a, b, (((1,), (1,)), ((), ())), preferred_element_type=F32)


def _log_sigmoid(x):
    return jnp.minimum(x, 0.0) - jnp.log1p(jnp.exp(-jnp.abs(x)))


def _cumsum_rows(x):
    n = x.shape[0]
    rid = lax.broadcasted_iota(jnp.int32, x.shape, 0)
    sh = 1
    while sh < n:
        x = x + jnp.where(rid >= sh, pltpu.roll(x, sh, axis=0), 0.0)
        sh *= 2
    return x


def _layer_norm_rows(y, g, b):
    mu = jnp.mean(y, axis=-1, keepdims=True)
    d = y - mu
    var = jnp.mean(d * d, axis=-1, keepdims=True)
    return d * lax.rsqrt(var + LN_EPS) * g + b


def _pool_select(lane, a, b):
    return jnp.where((lane & (LANES - 1)) < POOL_GROUP, a, b)


def _prompt_mixer_kernel(x_ref, w_in_ref, w_out_ref, gbias_ref, a2_ref, abias_ref, mnorm_ref,
                         gnorm_ref, pmix_ref, pscale_ref, lng_ref, lnb_ref,
                         xo_ref, pool_o_ref, c_o_ref, n_o_ref, m_o_ref, s_o_ref,
                         z_scr, mix_scr, ext_scr, c_scr, m_scr, s_scr):
    ti = pl.program_id(1)
    n_t = pl.num_programs(1)

    @pl.when(ti == 0)
    def _():
        ext_scr[0:2 * SUBLANES, :] = jnp.zeros((2 * SUBLANES, D_POOL), F32)
        c_scr[...] = jnp.zeros(c_scr.shape, F32)
        m_scr[...] = jnp.zeros(m_scr.shape, F32)
        s_scr[...] = jnp.zeros(s_scr.shape, F32)

    xb = x_ref[0].astype(BF16)
    for c0 in range(0, N_Z, 512):
        c1 = min(c0 + 512, N_Z)
        z_scr[:, c0:c1] = _dot(xb, w_in_ref[:, c0:c1])

    hist = 2 * SUBLANES
    u = z_scr[:, U_OFF:U_OFF + D_POOL]
    ext_scr[hist:hist + TT, :] = u
    ext = ext_scr[...]
    e0 = ext[:, 0:LANES]
    e1 = ext[:, LANES:2 * LANES]
    a2s = e0 + pltpu.roll(e0, 1, axis=0)
    a4s = a2s + pltpu.roll(a2s, 2, axis=0)
    b2s = e1 + pltpu.roll(e1, 1, axis=0)
    b4s = b2s + pltpu.roll(b2s, 2, axis=0)
    b8s = b4s + pltpu.roll(b4s, 4, axis=0)
    b16s = b8s + pltpu.roll(b8s, 8, axis=0)
    lane = lax.broadcasted_iota(jnp.int32, (TT, LANES), 1)
    pos1 = (lax.broadcasted_iota(jnp.int32, (TT, LANES), 0) + ti * TT + 1).astype(F32)
    ws0 = _pool_select(lane, a2s[hist:], a4s[hist:])
    ws1 = _pool_select(lane, b8s[hist:], b16s[hist:])
    cnt0 = jnp.minimum(_pool_select(lane, 2.0, 4.0), pos1)
    cnt1 = jnp.minimum(_pool_select(lane, 8.0, 16.0), pos1)
    d0 = ws0 / cnt0 - u[:, 0:LANES]
    d1 = ws1 / cnt1 - u[:, LANES:]
    dpool = jnp.concatenate([d0, d1], axis=1).astype(BF16)
    y_pool = _dot(dpool, pmix_ref[...]) * pscale_ref[...]
    mix_scr[:, 0:D_POOL] = y_pool.astype(BF16)
    tail = ext_scr[TT:TT + hist, :]
    ext_scr[0:hist, :] = tail

    @pl.when(ti == n_t - 1)
    def _():
        pool_o_ref[0] = ext_scr[1:hist, :]

    gbias = gbias_ref[...]
    abias = abias_ref[...]
    lane_g = lax.broadcasted_iota(jnp.int32, (CHUNK, LANES), 1)
    tcol = lax.broadcasted_iota(jnp.int32, (CHUNK, CHUNK), 0)
    srow = lax.broadcasted_iota(jnp.int32, (CHUNK, CHUNK), 1)
    causal = srow <= tcol

    def chunk_body(c, carry):
        r0 = pl.multiple_of(c * CHUNK, CHUNK)
        rows = pl.ds(r0, CHUNK)
        graw = z_scr[rows, GATE_OFF:GATE_OFF + LANES]

        g = graw + gbias
        fl = _log_sigmoid(g)
        b_all = _cumsum_rows(fl)
        i_sh = pltpu.roll(g, GATE_F - GATE_I, axis=1)
        m_row = m_scr[0:1, :]
        inter_all = b_all + m_row
        row_all = jnp.transpose(i_sh - b_all)
        m_new_row = m_row
        for h in range(HEADS):
            gl = GATE_F + h
            b_col = b_all[:, gl:gl + 1]
            dmat = jnp.where(causal, b_col + row_all[gl:gl + 1, :], -jnp.inf)
            inter = inter_all[:, gl:gl + 1]
            m_t = jnp.maximum(inter, jnp.max(dmat, axis=1, keepdims=True))
            e = jnp.exp(dmat - m_t)
            w_inter = jnp.exp(inter - m_t)
            q = (z_scr[rows, MQ_OFF + h * HP:MQ_OFF + (h + 1) * HP] * (MLSTM_DH ** -0.5)).astype(BF16)
            kf = z_scr[rows, MK_OFF + h * HP:MK_OFF + (h + 1) * HP]
            v = z_scr[rows, MV_OFF + h * HP:MV_OFF + (h + 1) * HP]
            v_aug = jnp.where(lane_g == MLSTM_DH, 1.0, v).astype(BF16)
            s = _dot_nt(q, kf.astype(BF16)) * e
            c_aug = c_scr[h]
            num = w_inter * _dot(q, c_aug.astype(BF16)) + _dot(s.astype(BF16), v_aug)
            nq = jnp.sum(jnp.where(lane_g == MLSTM_DH, num, 0.0), axis=1, keepdims=True)
            hval = num / jnp.maximum(jnp.abs(nq), jnp.exp(-m_t))
            valid = lane_g < MLSTM_DH
            mu = jnp.sum(jnp.where(valid, hval, 0.0), axis=1, keepdims=True) * (1.0 / MLSTM_DH)
            dlt = jnp.where(valid, hval - mu, 0.0)
            var = jnp.sum(dlt * dlt, axis=1, keepdims=True) * (1.0 / MLSTM_DH)
            hn = dlt * lax.rsqrt(var + LN_EPS) * mnorm_ref[:, h * HP:(h + 1) * HP]
            mo = z_scr[rows, MO_OFF + h * HP:MO_OFF + (h + 1) * HP]
            mix_scr[rows, YM_OFF + h * HP:YM_OFF + (h + 1) * HP] = (jax.nn.sigmoid(mo) * hn).astype(BF16)
            b_last = b_col[CHUNK - 1:CHUNK, :]
            m_prev = m_row[:, gl:gl + 1]
            m_new = m_t[CHUNK - 1:CHUNK, :]
            decay = jnp.exp(b_last + m_prev - m_new)
            w_end = jnp.exp(b_last - b_col + i_sh[:, gl:gl + 1] - m_new)
            kw_t = jnp.transpose(kf * w_end).astype(BF16)
            c_scr[h] = decay * c_aug + _dot(kw_t, v_aug)
            m_new_row = jnp.where(lax.broadcasted_iota(jnp.int32, (1, LANES), 1) == gl, m_new, m_new_row)
        m_scr[0:1, :] = m_new_row

        ya = _dot(graw.astype(BF16), a2_ref[...]) + abias
        la = _log_sigmoid(ya) * (1.0 / GLA_TAU)
        bc = _cumsum_rows(la)
        gq = z_scr[rows, GQ_OFF:GQ_OFF + HEADS * KP] * (GLA_DK ** -0.5)
        gk = z_scr[rows, GK_OFF:GK_OFF + HEADS * KP]
        width = HEADS * KP
        t_id = lax.broadcasted_iota(jnp.int32, (CHUNK, width), 0)
        lane_head = lax.broadcasted_iota(jnp.int32, (CHUNK, width), 1) // KP

        def stack_heads(a):
            return jnp.concatenate([jnp.where(lane_head == h, a, 0.0) for h in range(HEADS)],
                                   axis=0).astype(BF16)

        trow = lax.broadcasted_iota(jnp.int32, (HEADS * CHUNK, CHUNK), 0) & (CHUNK - 1)
        scol = lax.broadcasted_iota(jnp.int32, (HEADS * CHUNK, CHUNK), 1)
        amat = jnp.where(trow == scol, _dot_nt(stack_heads(gq), gk.astype(BF16)), 0.0)
        half = CHUNK // 2
        while half >= 1:
            blk = 2 * half
            upper = (t_id & half) != 0
            if blk >= SUBLANES:
                ref_rows = jnp.concatenate(
                    [jnp.broadcast_to(bc[b0 + half - 1:b0 + half, :], (blk, width))
                     for b0 in range(0, CHUNK, blk)], axis=0)
            else:
                off = t_id & (blk - 1)
                ref_rows = bc
                for o in range(blk):
                    sh = (o - (half - 1)) % CHUNK
                    if sh != 0:
                        ref_rows = jnp.where(off == o, pltpu.roll(bc, sh, axis=0), ref_rows)
            dlt = bc - ref_rows
            qe = jnp.where(upper, gq * jnp.exp(jnp.minimum(dlt, 0.0)), 0.0)
            ke = jnp.where(upper, 0.0, gk * jnp.exp(jnp.minimum(-dlt, 0.0)))
            p = _dot_nt(stack_heads(qe), ke.astype(BF16))
            shift = blk.bit_length() - 1
            amat = amat + jnp.where((trow >> shift) == (scol >> shift), p, 0.0)
            half //= 2

        o_inter = _dot(stack_heads(gq * jnp.exp(bc)), s_scr[...].astype(BF16))
        bc_t = jnp.transpose(bc)
        gk_t = jnp.transpose(gk)
        last_t = bc_t[:, CHUNK - 1:CHUNK]
        ke_t = (gk_t * jnp.exp(last_t - bc_t)).astype(BF16)
        s_dec = jnp.exp(last_t)
        for h in range(HEADS):
            hr = slice(h * CHUNK, (h + 1) * CHUNK)
            gv = z_scr[rows, GV_OFF + h * HP:GV_OFF + (h + 1) * HP].astype(BF16)
            o = o_inter[hr] + _dot(amat[hr].astype(BF16), gv)
            ms = jnp.sum(o * o, axis=1, keepdims=True) * (1.0 / GLA_DV)
            on = o * lax.rsqrt(ms + LN_EPS) * gnorm_ref[:, h * HP:(h + 1) * HP]
            gg = z_scr[rows, GG_OFF + h * HP:GG_OFF + (h + 1) * HP]
            mix_scr[rows, YG_OFF + h * HP:YG_OFF + (h + 1) * HP] = (gg * jax.nn.sigmoid(gg) * on).astype(BF16)
            kr = slice(h * KP, (h + 1) * KP)
            s_scr[kr, :] = s_dec[kr] * s_scr[kr, :] + _dot(ke_t[kr], gv)
        return carry

    lax.fori_loop(0, TT // CHUNK, chunk_body, 0)

    y = ALPHA * x_ref[0] + _dot(mix_scr[...], w_out_ref[...])
    xo_ref[0] = _layer_norm_rows(y, lng_ref[...], lnb_ref[...])

    @pl.when(ti == n_t - 1)
    def _():
        for h in range(HEADS):
            c_aug = c_scr[h]
            c_o_ref[0, h] = c_aug[0:MLSTM_DH, 0:MLSTM_DH]
            n_o_ref[0, h:h + 1, :] = jnp.transpose(c_aug)[MLSTM_DH:MLSTM_DH + 1, :]
            s_o_ref[0, h] = s_scr[h * KP:h * KP + GLA_DK, 0:GLA_DV]
        m_o_ref[0] = m_scr[...]


def _const_spec(shape):
    nd = len(shape)
    return pl.BlockSpec(shape, lambda *_: (0,) * nd, pipeline_mode=pl.Buffered(1))


def _prompt_mixer(x, w_in_p, w_out_p, gbias, a2_p, abias_p, mnorm_p, gnorm_p, pmix_bd, pscale, ln_g, ln_b):
    B, T, _ = x.shape
    consts = (w_in_p, w_out_p, gbias, a2_p, abias_p, mnorm_p, gnorm_p, pmix_bd, pscale, ln_g, ln_b)
    out_shape = (
        jax.ShapeDtypeStruct((B, T, D_MODEL), F32),
        jax.ShapeDtypeStruct((B, POOL_BUF, D_POOL), F32),
        jax.ShapeDtypeStruct((B, HEADS, MLSTM_DH, MLSTM_DH), F32),
        jax.ShapeDtypeStruct((B, HEADS, HP), F32),
        jax.ShapeDtypeStruct((B, SUBLANES, LANES), F32),
        jax.ShapeDtypeStruct((B, HEADS, GLA_DK, GLA_DV), F32),
    )
    out_specs = (
        pl.BlockSpec((1, TT, D_MODEL), lambda b, t: (b, t, 0)),
        pl.BlockSpec((1, POOL_BUF, D_POOL), lambda b, t: (b, 0, 0)),
        pl.BlockSpec((1, HEADS, MLSTM_DH, MLSTM_DH), lambda b, t: (b, 0, 0, 0)),
        pl.BlockSpec((1, HEADS, HP), lambda b, t: (b, 0, 0)),
        pl.BlockSpec((1, SUBLANES, LANES), lambda b, t: (b, 0, 0)),
        pl.BlockSpec((1, HEADS, GLA_DK, GLA_DV), lambda b, t: (b, 0, 0, 0)),
    )
    return pl.pallas_call(
        _prompt_mixer_kernel,
        grid=(B, T // TT),
        in_specs=[pl.BlockSpec((1, TT, D_MODEL), lambda b, t: (b, t, 0))] + [_const_spec(c.shape) for c in consts],
        out_specs=out_specs,
        out_shape=out_shape,
        scratch_shapes=[
            pltpu.VMEM((TT, N_Z), F32),
            pltpu.VMEM((TT, N_MIX), BF16),
            pltpu.VMEM((TT + 2 * SUBLANES, D_POOL), F32),
            pltpu.VMEM((HEADS, HP, HP), F32),
            pltpu.VMEM((SUBLANES, LANES), F32),
            pltpu.VMEM((HEADS * KP, HP), F32),
        ],
        compiler_params=pltpu.CompilerParams(
            dimension_semantics=("arbitrary", "arbitrary"), vmem_limit_bytes=VMEM_LIMIT),
        name="prompt_mixer",
    )(x, *consts)


def _column_bcast(row):
    return jnp.transpose(jnp.broadcast_to(row, (LANES, LANES)))


def _sample_mixer_kernel(x_ref, pool_ref, c_ref, n_ref, m_ref, s_ref,
                         w_in_ref, w_out_ref, gbias_ref, a2_ref, abias_ref, mnorm_ref,
                         gnorm_ref, pmix_ref, pscale_ref, lng_ref, lnb_ref,
                         xo_ref, pool_o_ref, c_o_ref, n_o_ref, m_o_ref, s_o_ref,
                         z_scr, mix_scr, qc_scr, og_scr):
    gi = pl.program_id(0)
    n_g = pl.num_programs(0)

    @pl.when(gi == 0)
    def _():
        mix_scr[...] = jnp.zeros(mix_scr.shape, F32)
        xb = x_ref[...].astype(BF16)
        for c0 in range(0, N_Z, 512):
            c1 = min(c0 + 512, N_Z)
            z_scr[:, c0:c1] = _dot(xb, w_in_ref[:, c0:c1])

    r0 = pl.multiple_of(gi * SG, SG)
    rows = pl.ds(r0, SG)

    u = z_scr[rows, U_OFF:U_OFF + D_POOL]
    run = u
    sums = {}
    for j in range(1, max(POOL_WINDOWS)):
        run = run + pool_ref[POOL_BUF - j]
        if j + 1 in POOL_WINDOWS:
            sums[j + 1] = run
    lane = lax.broadcasted_iota(jnp.int32, (SG, D_POOL), 1)
    grp = lane // POOL_GROUP
    wsum = jnp.zeros((SG, D_POOL), F32)
    for gidx, w in enumerate(POOL_WINDOWS):
        cnt = float(min(w, PAST_LEN + 1))
        wsum = jnp.where(grp == gidx, sums[w] / cnt, wsum)
    dpool = (wsum - u).astype(BF16)
    mix_scr[rows, 0:D_POOL] = _dot(dpool, pmix_ref[...]) * pscale_ref[...]
    for j in range(POOL_BUF - 1):
        pool_o_ref[j] = pool_ref[j + 1]
    pool_o_ref[POOL_BUF - 1] = u

    graw = z_scr[rows, GATE_OFF:GATE_OFF + LANES]
    g = graw + gbias_ref[...]

    log_i = g[:, GATE_I:GATE_I + HEADS]
    log_f = _log_sigmoid(g[:, GATE_F:GATE_F + HEADS])
    inter = log_f + m_ref[...]
    m_t = jnp.maximum(inter, log_i)
    w_inter = jnp.exp(inter - m_t)
    w_in_gate = jnp.exp(log_i - m_t)
    m_o_ref[...] = m_t
    for h in range(HEADS):
        q = z_scr[rows, MQ_OFF + h * HP:MQ_OFF + (h + 1) * HP] * (MLSTM_DH ** -0.5)
        k = z_scr[rows, MK_OFF + h * HP:MK_OFF + (h + 1) * HP]
        v = z_scr[rows, MV_OFF + h * HP:MV_OFF + (h + 1) * HP]
        n_old = n_ref[:, h * MLSTM_DH:(h + 1) * MLSTM_DH]
        dec = w_inter[:, h:h + 1]
        wig = w_in_gate[:, h:h + 1]
        for j in range(SG):
            c_old = c_ref[j, h]
            qcol = _column_bcast(q[j:j + 1, :])[0:MLSTM_DH, 0:MLSTM_DH]
            kcol = _column_bcast(k[j:j + 1, :])[0:MLSTM_DH, 0:MLSTM_DH]
            qc_scr[j:j + 1, h * HP:h * HP + MLSTM_DH] = jnp.sum(qcol * c_old, axis=0, keepdims=True)
            c_o_ref[j, h] = dec[j:j + 1, :] * c_old + kcol * (wig[j:j + 1, :] * v[j:j + 1, 0:MLSTM_DH])
        q96 = q[:, 0:MLSTM_DH]
        k96 = k[:, 0:MLSTM_DH]
        v96 = v[:, 0:MLSTM_DH]
        sc = jnp.sum(q96 * k96, axis=1, keepdims=True) * wig
        num = dec * qc_scr[:, h * HP:h * HP + MLSTM_DH] + sc * v96
        nq = dec * jnp.sum(q96 * n_old, axis=1, keepdims=True) + sc
        hval = num / jnp.maximum(jnp.abs(nq), jnp.exp(-m_t[:, h:h + 1]))
        n_o_ref[:, h * MLSTM_DH:(h + 1) * MLSTM_DH] = dec * n_old + wig * k96
        mu = jnp.mean(hval, axis=1, keepdims=True)
        dlt = hval - mu
        var = jnp.mean(dlt * dlt, axis=1, keepdims=True)
        hn = dlt * lax.rsqrt(var + LN_EPS) * mnorm_ref[:, h * HP:h * HP + MLSTM_DH]
        mo = z_scr[rows, MO_OFF + h * HP:MO_OFF + h * HP + MLSTM_DH]
        mix_scr[rows, YM_OFF + h * HP:YM_OFF + h * HP + MLSTM_DH] = jax.nn.sigmoid(mo) * hn

    ya = _dot(graw.astype(BF16), a2_ref[...]) + abias_ref[...]
    ea = jnp.exp(_log_sigmoid(ya) * (1.0 / GLA_TAU))
    gq = z_scr[rows, GQ_OFF:GQ_OFF + HEADS * KP] * (GLA_DK ** -0.5)
    gk = z_scr[rows, GK_OFF:GK_OFF + HEADS * KP]
    gqe = gq * ea
    lane_head = lax.broadcasted_iota(jnp.int32, (SG, HEADS * KP), 1) // KP
    qk_prod = gq * gk
    gvs = [z_scr[rows, GV_OFF + h * HP:GV_OFF + h * HP + GLA_DV] for h in range(HEADS)]
    for pair in range(HEADS // 2):
        pl_ = slice(pair * LANES, (pair + 1) * LANES)
        for j in range(SG):
            qecol = _column_bcast(gqe[j:j + 1, pl_])
            kcol = _column_bcast(gk[j:j + 1, pl_])
            eacol = _column_bcast(ea[j:j + 1, pl_])
            for hh in range(2):
                h = 2 * pair + hh
                kr = slice(hh * KP, hh * KP + GLA_DK)
                s_old = s_ref[j, h]
                vrow = gvs[h][j:j + 1, :]
                og_scr[j:j + 1, h * HP:h * HP + GLA_DV] = jnp.sum(
                    qecol[kr, 0:GLA_DV] * s_old, axis=0, keepdims=True)
                s_o_ref[j, h] = eacol[kr, 0:GLA_DV] * s_old + kcol[kr, 0:GLA_DV] * vrow
    for h in range(HEADS):
        gv = gvs[h]
        qk = jnp.sum(jnp.where(lane_head == h, qk_prod, 0.0), axis=1, keepdims=True)
        o = og_scr[:, h * HP:h * HP + GLA_DV] + qk * gv
        ms = jnp.mean(o * o, axis=1, keepdims=True)
        on = o * lax.rsqrt(ms + LN_EPS) * gnorm_ref[:, h * HP:h * HP + GLA_DV]
        gg = z_scr[rows, GG_OFF + h * HP:GG_OFF + h * HP + GLA_DV]
        mix_scr[rows, YG_OFF + h * HP:YG_OFF + h * HP + GLA_DV] = gg * jax.nn.sigmoid(gg) * on

    @pl.when(gi == n_g - 1)
    def _():
        y = ALPHA * x_ref[...] + _dot(mix_scr[...].astype(BF16), w_out_ref[...])
        xo_ref[...] = _layer_norm_rows(y, lng_ref[...], lnb_ref[...])


def _sample_mixer(x, pool_t, c, n2, m, s, consts):
    B = x.shape[0]
    out_shape = (
        jax.ShapeDtypeStruct((B, D_MODEL), F32),
        jax.ShapeDtypeStruct(pool_t.shape, F32),
        jax.ShapeDtypeStruct(c.shape, F32),
        jax.ShapeDtypeStruct(n2.shape, F32),
        jax.ShapeDtypeStruct(m.shape, F32),
        jax.ShapeDtypeStruct(s.shape, F32),
    )
    state_specs = [
        pl.BlockSpec((POOL_BUF, SG, D_POOL), lambda g: (0, g, 0)),
        pl.BlockSpec((SG, HEADS, MLSTM_DH, MLSTM_DH), lambda g: (g, 0, 0, 0)),
        pl.BlockSpec((SG, HEADS * MLSTM_DH), lambda g: (g, 0)),
        pl.BlockSpec((SG, HEADS), lambda g: (g, 0)),
        pl.BlockSpec((SG, HEADS, GLA_DK, GLA_DV), lambda g: (g, 0, 0, 0)),
    ]
    x_spec = pl.BlockSpec((B, D_MODEL), lambda g: (0, 0))
    return pl.pallas_call(
        _sample_mixer_kernel,
        grid=(B // SG,),
        in_specs=[x_spec] + state_specs + [_const_spec(cst.shape) for cst in consts],
        out_specs=[x_spec] + state_specs,
        out_shape=out_shape,
        scratch_shapes=[
            pltpu.VMEM((B, N_Z), F32),
            pltpu.VMEM((B, N_MIX), F32),
            pltpu.VMEM((SG, HEADS * HP), F32),
            pltpu.VMEM((SG, HEADS * HP), F32),
        ],
        compiler_params=pltpu.CompilerParams(
            dimension_semantics=("arbitrary",), vmem_limit_bytes=VMEM_LIMIT),
        name="sample_mixer",
    )(x, pool_t, c, n2, m, s, *consts)


def _ffn_kernel(x_ref, p_ref, wg_ref, wu_ref, wd_ref, wp_ref, wpg_ref, lng_ref, lnb_ref, o_ref):
    x = x_ref[...]
    xb = x.astype(BF16)
    acc = ALPHA * x + _dot(p_ref[...].astype(BF16), wp_ref[...]) * jax.nn.sigmoid(_dot(xb, wpg_ref[...]))
    for c0 in range(0, D_FF, FF_CHUNK):
        c1 = min(c0 + FF_CHUNK, D_FF)
        gate = _dot(xb, wg_ref[:, c0:c1])
        up = _dot(xb, wu_ref[:, c0:c1])
        hid = (gate * jax.nn.sigmoid(gate) * up).astype(BF16)
        acc = acc + _dot(hid, wd_ref[c0:c1, :])
    o_ref[...] = _layer_norm_rows(acc, lng_ref[...], lnb_ref[...])


def _ffn(x, p, wg, wu, wd, wp, wpg, ln_g, ln_b, tm):
    M = x.shape[0]
    consts = (wg, wu, wd, wp, wpg, ln_g, ln_b)
    return pl.pallas_call(
        _ffn_kernel,
        grid=(M // tm,),
        in_specs=[pl.BlockSpec((tm, D_MODEL), lambda i: (i, 0)),
                  pl.BlockSpec((tm, D_PLE), lambda i: (i, 0))] + [_const_spec(c.shape) for c in consts],
        out_specs=pl.BlockSpec((tm, D_MODEL), lambda i: (i, 0)),
        out_shape=jax.ShapeDtypeStruct((M, D_MODEL), F32),
        compiler_params=pltpu.CompilerParams(
            dimension_semantics=("arbitrary",), vmem_limit_bytes=VMEM_LIMIT),
        name="ffn",
    )(x, p, *consts)


def kernel(x_prompt, x_sample, p_prompt, p_sample, state_pool, state_mlstm_C, state_mlstm_n, state_mlstm_m, state_gla_S, w_in, mlstm_i_bias, mlstm_f_bias, mlstm_norm_w, gla_a2, gla_a_bias, gla_norm_w, pool_mix, pool_scale, w_out, ln1_g, ln1_b, w_gate, w_up, w_down, w_ple, w_ple_gate, ln2_g, ln2_b):
    Bp, Tp, _ = x_prompt.shape
    Bs = x_sample.shape[0]

    w_in_p = _gather_pad(w_in, _z_column_map(), axis=2).astype(BF16)
    w_out_p = _gather_pad(w_out, _mix_row_map(), axis=1).astype(BF16)
    gbias = jnp.zeros((DEPTH, 1, LANES), F32)
    gbias = gbias.at[:, 0, GATE_I:GATE_I + HEADS].set(mlstm_i_bias.astype(F32))
    gbias = gbias.at[:, 0, GATE_F:GATE_F + HEADS].set(mlstm_f_bias.astype(F32))
    a2_p = jnp.zeros((DEPTH, LANES, HEADS * KP), F32)
    a2_p = a2_p.at[:, GATE_A:GATE_A + GLA_RANK, :].set(_pad_heads(gla_a2.astype(F32), GLA_DK, KP)).astype(BF16)
    abias_p = _pad_heads(gla_a_bias.astype(F32), GLA_DK, KP)[:, None, :]
    mnorm_p = _pad_heads(mlstm_norm_w.astype(F32), MLSTM_DH, HP)[:, None, :]
    gnorm_p = _pad_heads(gla_norm_w.astype(F32), GLA_DV, HP)[:, None, :]
    n_grp = len(POOL_WINDOWS)
    eye = jnp.eye(n_grp, dtype=F32)
    pmix_bd = (pool_mix.astype(F32)[:, :, :, None, :] * eye[None, :, None, :, None]).reshape(
        DEPTH, D_POOL, D_POOL).astype(BF16)
    pscale = pool_scale.astype(F32)[:, None, :]
    row = lambda a: a.astype(F32)[:, None, :]
    ln1_g_r, ln1_b_r, ln2_g_r, ln2_b_r = row(ln1_g), row(ln1_b), row(ln2_g), row(ln2_b)
    wg_b, wu_b, wd_b = w_gate.astype(BF16), w_up.astype(BF16), w_down.astype(BF16)
    wp_b, wpg_b = w_ple.astype(BF16), w_ple_gate.astype(BF16)

    xp = x_prompt
    xs = x_sample.reshape(Bs, D_MODEL)
    pool_t = jnp.transpose(state_pool, (0, 2, 1, 3))
    n2 = state_mlstm_n.reshape(DEPTH, Bs, HEADS * MLSTM_DH)
    outs_p, outs_s = [], []
    for l in range(DEPTH):
        mixer_consts = (w_in_p[l], w_out_p[l], gbias[l], a2_p[l], abias_p[l], mnorm_p[l], gnorm_p[l],
                        pmix_bd[l], pscale[l], ln1_g_r[l], ln1_b_r[l])
        ffn_consts = (wg_b[l], wu_b[l], wd_b[l], wp_b[l], wpg_b[l], ln2_g_r[l], ln2_b_r[l])
        xp, pool_p, c_p, n_p, m_p, s_p = _prompt_mixer(xp, *mixer_consts)
        xp = _ffn(xp.reshape(Bp * Tp, D_MODEL), p_prompt[l].reshape(Bp * Tp, D_PLE), *ffn_consts,
                  tm=512).reshape(Bp, Tp, D_MODEL)
        outs_p.append((pool_p, c_p, n_p[:, :, :MLSTM_DH], m_p[:, 0, GATE_F:GATE_F + HEADS], s_p))
        xs, pool_s, c_s, n_s, m_s, s_s = _sample_mixer(
            xs, pool_t[l], state_mlstm_C[l], n2[l], state_mlstm_m[l], state_gla_S[l], mixer_consts)
        xs = _ffn(xs, p_sample[l].reshape(Bs, D_PLE), *ffn_consts, tm=Bs)
        outs_s.append((jnp.transpose(pool_s, (1, 0, 2)), c_s, n_s.reshape(Bs, HEADS, MLSTM_DH), m_s, s_s))

    stack = lambda outs, j: jnp.stack([o[j] for o in outs], axis=0)
    return ((xp, xs.reshape(Bs, 1, D_MODEL))
            + tuple(stack(outs_p, j) for j in range(5))
            + tuple(stack(outs_s, j) for j in range(5)))
```

```python
import functools

import numpy as np
import jax
import jax.numpy as jnp
from jax import lax
from jax.experimental import pallas as pl
from jax.experimental.pallas import tpu as pltpu

F32 = jnp.float32
BF16 = jnp.bfloat16

D_MODEL = 1024
DEPTH = 4
PAST_LEN = 16384
D_POOL = 256
POOL_WINDOWS = (2, 4, 8, 16)
POOL_GROUP = 64
POOL_BUF = 15
D_MLSTM = 384
HEADS = 4
MLSTM_DH = 96
D_GLA = 384
GLA_DV = 96
GLA_DK = 48
GLA_RANK = 16
GLA_TAU = 16.0
D_MIX = 1024
D_FF = 2816
D_PLE = 256
CHUNK = 64
ALPHA = (2 * DEPTH) ** 0.25
LN_EPS = 1e-5

LANES = 128
SUBLANES = 8
VMEM_LIMIT = 56 * 1024 * 1024

HP = LANES
KP = 64
U_OFF = 0
MQ_OFF = U_OFF + D_POOL
MK_OFF = MQ_OFF + HEADS * HP
MV_OFF = MK_OFF + HEADS * HP
MO_OFF = MV_OFF + HEADS * HP
GQ_OFF = MO_OFF + HEADS * HP
GK_OFF = GQ_OFF + HEADS * KP
GV_OFF = GK_OFF + HEADS * KP
GG_OFF = GV_OFF + HEADS * HP
GATE_OFF = GG_OFF + HEADS * HP
N_Z = GATE_OFF + LANES
GATE_I = 0
GATE_F = 4
GATE_A = 8
YM_OFF = D_POOL
YG_OFF = YM_OFF + HEADS * HP
N_MIX = YG_OFF + HEADS * HP

TT = 512
CHUNK_UNROLL = 2
SG = SUBLANES
FF_CHUNK = 512


def _pad_heads(a, width, padded):
    lead = a.shape[:-1]
    a = a.reshape(lead + (HEADS, width))
    a = jnp.pad(a, [(0, 0)] * len(lead) + [(0, 0), (0, padded - width)])
    return a.reshape(lead + (HEADS * padded,))


def _pad_in_proj(w_in):
    sizes = (D_POOL, D_MLSTM, D_MLSTM, D_MLSTM, HEADS, HEADS, D_MLSTM,
             HEADS * GLA_DK, HEADS * GLA_DK, D_GLA, GLA_RANK, D_GLA)
    u, mq, mk, mv, mi, mf, mo, gq, gk, gv, ga, gg = jnp.split(w_in, np.cumsum(sizes)[:-1].tolist(), axis=-1)
    gate_pad = jnp.zeros(w_in.shape[:-1] + (LANES - GATE_A - GLA_RANK,), w_in.dtype)
    wide = lambda a: _pad_heads(a, MLSTM_DH, HP)
    narrow = lambda a: _pad_heads(a, GLA_DK, KP)
    return jnp.concatenate([u, wide(mq), wide(mk), wide(mv), wide(mo), narrow(gq), narrow(gk), wide(gv),
                            wide(gg), mi, mf, ga, gate_pad], axis=-1)


def _pad_out_proj(w_out):
    pool = w_out[:, :D_POOL]
    heads = w_out[:, D_POOL:].reshape(w_out.shape[0], 2 * HEADS, MLSTM_DH, D_MODEL)
    heads = jnp.pad(heads, ((0, 0), (0, 0), (0, HP - MLSTM_DH), (0, 0)))
    return jnp.concatenate([pool, heads.reshape(w_out.shape[0], 2 * HEADS * HP, D_MODEL)], axis=1)


def _dot(a, b):
    return jnp.dot(a, b, preferred_element_type=F32)


def _dot_nt(a, b):
    return lax.dot_general(a, b, (((1,), (1,)), ((), ())), preferred_element_type=F32)


def _log_sigmoid(x):
    return jnp.minimum(x, 0.0) - jnp.log1p(jnp.exp(-jnp.abs(x)))


def _cumsum_rows(x):
    n = x.shape[0]
    rid = lax.broadcasted_iota(jnp.int32, x.shape, 0)
    sh = 1
    while sh < n:
        x = x + jnp.where(rid >= sh, pltpu.roll(x, sh, axis=0), 0.0)
        sh *= 2
    return x


def _layer_norm_rows(y, g, b):
    mu = jnp.mean(y, axis=-1, keepdims=True)
    d = y - mu
    var = jnp.mean(d * d, axis=-1, keepdims=True)
    return d * lax.rsqrt(var + LN_EPS) * g + b


def _pool_select(lane, a, b):
    return jnp.where((lane & (LANES - 1)) < POOL_GROUP, a, b)


def _prompt_mixer_kernel(x_ref, w_in_ref, w_out_ref, gbias_ref, a2_ref, abias_ref, mnorm_ref,
                         gnorm_ref, pmix_ref, pscale_ref, lng_ref, lnb_ref,
                         xo_ref, pool_o_ref, c_o_ref, n_o_ref, m_o_ref, s_o_ref,
                         z_scr, mix_scr, ext_scr, c_scr, m_scr, s_scr):
    ti = pl.program_id(1)
    n_t = pl.num_programs(1)

    @pl.when(ti == 0)
    def _():
        ext_scr[0:2 * SUBLANES, :] = jnp.zeros((2 * SUBLANES, D_POOL), F32)
        c_scr[...] = jnp.zeros(c_scr.shape, F32)
        m_scr[...] = jnp.zeros(m_scr.shape, F32)
        s_scr[...] = jnp.zeros(s_scr.shape, F32)

    xb = x_ref[0].astype(BF16)
    for c0 in range(0, N_Z, 512):
        c1 = min(c0 + 512, N_Z)
        z_scr[:, c0:c1] = _dot(xb, w_in_ref[:, c0:c1])

    hist = 2 * SUBLANES
    u = z_scr[:, U_OFF:U_OFF + D_POOL]
    ext_scr[hist:hist + TT, :] = u
    ext = ext_scr[...]
    e0 = ext[:, 0:LANES]
    e1 = ext[:, LANES:2 * LANES]
    a2s = e0 + pltpu.roll(e0, 1, axis=0)
    a4s = a2s + pltpu.roll(a2s, 2, axis=0)
    b2s = e1 + pltpu.roll(e1, 1, axis=0)
    b4s = b2s + pltpu.roll(b2s, 2, axis=0)
    b8s = b4s + pltpu.roll(b4s, 4, axis=0)
    b16s = b8s + pltpu.roll(b8s, 8, axis=0)
    lane = lax.broadcasted_iota(jnp.int32, (TT, LANES), 1)
    pos1 = (lax.broadcasted_iota(jnp.int32, (TT, LANES), 0) + ti * TT + 1).astype(F32)
    ws0 = _pool_select(lane, a2s[hist:], a4s[hist:])
    ws1 = _pool_select(lane, b8s[hist:], b16s[hist:])
    cnt0 = jnp.minimum(_pool_select(lane, 2.0, 4.0), pos1)
    cnt1 = jnp.minimum(_pool_select(lane, 8.0, 16.0), pos1)
    d0 = ws0 / cnt0 - u[:, 0:LANES]
    d1 = ws1 / cnt1 - u[:, LANES:]
    dpool = jnp.concatenate([d0, d1], axis=1).astype(BF16)
    y_pool = _dot(dpool, pmix_ref[...]) * pscale_ref[...]
    mix_scr[:, 0:D_POOL] = y_pool.astype(BF16)
    tail = ext_scr[TT:TT + hist, :]
    ext_scr[0:hist, :] = tail

    @pl.when(ti == n_t - 1)
    def _():
        pool_o_ref[0] = ext_scr[1:hist, :]

    gbias = gbias_ref[...]
    abias = abias_ref[...]
    lane_g = lax.broadcasted_iota(jnp.int32, (CHUNK, LANES), 1)
    tcol = lax.broadcasted_iota(jnp.int32, (CHUNK, CHUNK), 0)
    srow = lax.broadcasted_iota(jnp.int32, (CHUNK, CHUNK), 1)
    causal = srow <= tcol

    def chunk_body(c, carry):
        r0 = pl.multiple_of(c * CHUNK, CHUNK)
        rows = pl.ds(r0, CHUNK)
        graw = z_scr[rows, GATE_OFF:GATE_OFF + LANES]

        g = graw + gbias
        fl = _log_sigmoid(g)
        b_all = _cumsum_rows(fl)
        i_sh = pltpu.roll(g, GATE_F - GATE_I, axis=1)
        m_row = m_scr[0:1, :]
        inter_all = b_all + m_row
        row_all = jnp.transpose(i_sh - b_all)
        m_new_row = m_row
        for h in range(HEADS):
            gl = GATE_F + h
            b_col = b_all[:, gl:gl + 1]
            dmat = jnp.where(causal, b_col + row_all[gl:gl + 1, :], -jnp.inf)
            inter = inter_all[:, gl:gl + 1]
            m_t = jnp.maximum(inter, jnp.max(dmat, axis=1, keepdims=True))
            e = jnp.exp(dmat - m_t)
            w_inter = jnp.exp(inter - m_t)
            q = (z_scr[rows, MQ_OFF + h * HP:MQ_OFF + (h + 1) * HP] * (MLSTM_DH ** -0.5)).astype(BF16)
            kf = z_scr[rows, MK_OFF + h * HP:MK_OFF + (h + 1) * HP]
            v = z_scr[rows, MV_OFF + h * HP:MV_OFF + (h + 1) * HP]
            v_aug = jnp.where(lane_g == MLSTM_DH, 1.0, v).astype(BF16)
            s = _dot_nt(q, kf.astype(BF16)) * e
            c_aug = c_scr[h]
            num = w_inter * _dot(q, c_aug.astype(BF16)) + _dot(s.astype(BF16), v_aug)
            nq = jnp.sum(jnp.where(lane_g == MLSTM_DH, num, 0.0), axis=1, keepdims=True)
            hval = num / jnp.maximum(jnp.abs(nq), jnp.exp(-m_t))
            valid = lane_g < MLSTM_DH
            mu = jnp.sum(jnp.where(valid, hval, 0.0), axis=1, keepdims=True) * (1.0 / MLSTM_DH)
            dlt = jnp.where(valid, hval - mu, 0.0)
            var = jnp.sum(dlt * dlt, axis=1, keepdims=True) * (1.0 / MLSTM_DH)
            hn = dlt * lax.rsqrt(var + LN_EPS) * mnorm_ref[:, h * HP:(h + 1) * HP]
            mo = z_scr[rows, MO_OFF + h * HP:MO_OFF + (h + 1) * HP]
            mix_scr[rows, YM_OFF + h * HP:YM_OFF + (h + 1) * HP] = (jax.nn.sigmoid(mo) * hn).astype(BF16)
            b_last = b_col[CHUNK - 1:CHUNK, :]
            m_prev = m_row[:, gl:gl + 1]
            m_new = m_t[CHUNK - 1:CHUNK, :]
            decay = jnp.exp(b_last + m_prev - m_new)
            w_end = jnp.exp(b_last - b_col + i_sh[:, gl:gl + 1] - m_new)
            kw_t = jnp.transpose(kf * w_end).astype(BF16)
            c_scr[h] = decay * c_aug + _dot(kw_t, v_aug)
            m_new_row = jnp.where(lax.broadcasted_iota(jnp.int32, (1, LANES), 1) == gl, m_new, m_new_row)
        m_scr[0:1, :] = m_new_row

        ya = _dot(graw.astype(BF16), a2_ref[...]) + abias
        la = _log_sigmoid(ya) * (1.0 / GLA_TAU)
        bc = _cumsum_rows(la)
        gq = z_scr[rows, GQ_OFF:GQ_OFF + HEADS * KP] * (GLA_DK ** -0.5)
        gk = z_scr[rows, GK_OFF:GK_OFF + HEADS * KP]
        width = HEADS * KP
        t_id = lax.broadcasted_iota(jnp.int32, (CHUNK, width), 0)
        lane_head = lax.broadcasted_iota(jnp.int32, (CHUNK, width), 1) // KP

        def stack_heads(a):
            return jnp.concatenate([jnp.where(lane_head == h, a, 0.0) for h in range(HEADS)],
                                   axis=0).astype(BF16)

        trow = lax.broadcasted_iota(jnp.int32, (HEADS * CHUNK, CHUNK), 0) & (CHUNK - 1)
        scol = lax.broadcasted_iota(jnp.int32, (HEADS * CHUNK, CHUNK), 1)
        amat = jnp.where(trow == scol, _dot_nt(stack_heads(gq), gk.astype(BF16)), 0.0)
        half = CHUNK // 2
        while half >= 1:
            blk = 2 * half
            upper = (t_id & half) != 0
            if blk >= SUBLANES:
                ref_rows = jnp.concatenate(
                    [jnp.broadcast_to(bc[b0 + half - 1:b0 + half, :], (blk, width))
                     for b0 in range(0, CHUNK, blk)], axis=0)
            else:
                off = t_id & (blk - 1)
                ref_rows = bc
                for o in range(blk):
                    sh = (o - (half - 1)) % CHUNK
                    if sh != 0:
                        ref_rows = jnp.where(off == o, pltpu.roll(bc, sh, axis=0), ref_rows)
            dlt = bc - ref_rows
            qe = jnp.where(upper, gq * jnp.exp(jnp.minimum(dlt, 0.0)), 0.0)
            ke = jnp.where(upper, 0.0, gk * jnp.exp(jnp.minimum(-dlt, 0.0)))
            p = _dot_nt(stack_heads(qe), ke.astype(BF16))
            shift = blk.bit_length() - 1
            amat = amat + jnp.where((trow >> shift) == (scol >> shift), p, 0.0)
            half //= 2

        o_inter = _dot(stack_heads(gq * jnp.exp(bc)), s_scr[...].astype(BF16))
        bc_t = jnp.transpose(bc)
        gk_t = jnp.transpose(gk)
        last_t = bc_t[:, CHUNK - 1:CHUNK]
        ke_t = (gk_t * jnp.exp(last_t - bc_t)).astype(BF16)
        s_dec = jnp.exp(last_t)
        for h in range(HEADS):
            hr = slice(h * CHUNK, (h + 1) * CHUNK)
            gv = z_scr[rows, GV_OFF + h * HP:GV_OFF + (h + 1) * HP].astype(BF16)
            o = o_inter[hr] + _dot(amat[hr].astype(BF16), gv)
            ms = jnp.sum(o * o, axis=1, keepdims=True) * (1.0 / GLA_DV)
            on = o * lax.rsqrt(ms + LN_EPS) * gnorm_ref[:, h * HP:(h + 1) * HP]
            gg = z_scr[rows, GG_OFF + h * HP:GG_OFF + (h + 1) * HP]
            mix_scr[rows, YG_OFF + h * HP:YG_OFF + (h + 1) * HP] = (gg * jax.nn.sigmoid(gg) * on).astype(BF16)
            kr = slice(h * KP, (h + 1) * KP)
            s_scr[kr, :] = s_dec[kr] * s_scr[kr, :] + _dot(ke_t[kr], gv)
        return carry

    lax.fori_loop(0, TT // CHUNK, chunk_body, 0, unroll=CHUNK_UNROLL)

    y = ALPHA * x_ref[0] + _dot(mix_scr[...], w_out_ref[...])
    xo_ref[0] = _layer_norm_rows(y, lng_ref[...], lnb_ref[...])

    @pl.when(ti == n_t - 1)
    def _():
        for h in range(HEADS):
            c_aug = c_scr[h]
            c_o_ref[0, h] = c_aug[0:MLSTM_DH, 0:MLSTM_DH]
            n_o_ref[0, h:h + 1, :] = jnp.transpose(c_aug)[MLSTM_DH:MLSTM_DH + 1, :]
            s_o_ref[0, h] = s_scr[h * KP:h * KP + GLA_DK, 0:GLA_DV]
        m_o_ref[0] = m_scr[...]


def _layer_spec(shape, layer):
    nd = len(shape)
    return pl.BlockSpec((None,) + tuple(shape[1:]), lambda *_: (layer,) + (0,) * (nd - 1),
                        pipeline_mode=pl.Buffered(1))


def _prompt_mixer(x, consts, layer):
    B, T, _ = x.shape
    out_shape = (
        jax.ShapeDtypeStruct((B, T, D_MODEL), F32),
        jax.ShapeDtypeStruct((B, POOL_BUF, D_POOL), F32),
        jax.ShapeDtypeStruct((B, HEADS, MLSTM_DH, MLSTM_DH), F32),
        jax.ShapeDtypeStruct((B, HEADS, HP), F32),
        jax.ShapeDtypeStruct((B, SUBLANES, LANES), F32),
        jax.ShapeDtypeStruct((B, HEADS, GLA_DK, GLA_DV), F32),
    )
    out_specs = (
        pl.BlockSpec((1, TT, D_MODEL), lambda b, t: (b, t, 0)),
        pl.BlockSpec((1, POOL_BUF, D_POOL), lambda b, t: (b, 0, 0)),
        pl.BlockSpec((1, HEADS, MLSTM_DH, MLSTM_DH), lambda b, t: (b, 0, 0, 0)),
        pl.BlockSpec((1, HEADS, HP), lambda b, t: (b, 0, 0)),
        pl.BlockSpec((1, SUBLANES, LANES), lambda b, t: (b, 0, 0)),
        pl.BlockSpec((1, HEADS, GLA_DK, GLA_DV), lambda b, t: (b, 0, 0, 0)),
    )
    return pl.pallas_call(
        _prompt_mixer_kernel,
        grid=(B, T // TT),
        in_specs=[pl.BlockSpec((1, TT, D_MODEL), lambda b, t: (b, t, 0))]
        + [_layer_spec(c.shape, layer) for c in consts],
        out_specs=out_specs,
        out_shape=out_shape,
        scratch_shapes=[
            pltpu.VMEM((TT, N_Z), F32),
            pltpu.VMEM((TT, N_MIX), BF16),
            pltpu.VMEM((TT + 2 * SUBLANES, D_POOL), F32),
            pltpu.VMEM((HEADS, HP, HP), F32),
            pltpu.VMEM((SUBLANES, LANES), F32),
            pltpu.VMEM((HEADS * KP, HP), F32),
        ],
        compiler_params=pltpu.CompilerParams(
            dimension_semantics=("arbitrary", "arbitrary"), vmem_limit_bytes=VMEM_LIMIT),
        name="prompt_mixer",
    )(x, *consts)


def _column_bcast(row):
    return jnp.transpose(jnp.broadcast_to(row, (LANES, LANES)))


def _sample_mixer_kernel(n_carried, *refs):
    (x_ref, pool_ref, c_ref, n_ref, m_ref, s_ref,
     w_in_ref, w_out_ref, gbias_ref, a2_ref, abias_ref, mnorm_ref,
     gnorm_ref, pmix_ref, pscale_ref, lng_ref, lnb_ref) = refs[:17]
    (xo_ref, pool_o_ref, c_o_ref, n_o_ref, m_o_ref, s_o_ref,
     z_scr, mix_scr, qc_scr, og_scr, psum_scr) = refs[17 + n_carried:]
    gi = pl.program_id(0)
    n_g = pl.num_programs(0)

    @pl.when(gi == 0)
    def _():
        mix_scr[...] = jnp.zeros(mix_scr.shape, F32)
        xb = x_ref[...].astype(BF16)
        for c0 in range(0, N_Z, 512):
            c1 = min(c0 + 512, N_Z)
            z_scr[:, c0:c1] = _dot(xb, w_in_ref[:, c0:c1])

    r0 = pl.multiple_of(gi * SG, SG)
    rows = pl.ds(r0, SG)

    u = z_scr[rows, U_OFF:U_OFF + D_POOL]

    def window_of(lane):
        grp = lane // POOL_GROUP
        w = jnp.full(lane.shape, POOL_WINDOWS[0], jnp.int32)
        for gidx in range(1, len(POOL_WINDOWS)):
            w = jnp.where(grp == gidx, POOL_WINDOWS[gidx], w)
        return w

    hist_row = lax.broadcasted_iota(jnp.int32, (POOL_BUF, D_POOL), 0)
    in_window = hist_row > POOL_BUF - window_of(lax.broadcasted_iota(jnp.int32, (POOL_BUF, D_POOL), 1))
    for j in range(SG):
        psum_scr[j:j + 1, :] = jnp.sum(jnp.where(in_window, pool_ref[j], 0.0), axis=0, keepdims=True)
        pool_o_ref[j, 0:POOL_BUF - 1, :] = pool_ref[j, 1:POOL_BUF, :]
        pool_o_ref[j, POOL_BUF - 1:POOL_BUF, :] = u[j:j + 1, :]
    cnt = jnp.minimum(window_of(lax.broadcasted_iota(jnp.int32, (SG, D_POOL), 1)), PAST_LEN + 1).astype(F32)
    dpool = ((psum_scr[...] + u) / cnt - u).astype(BF16)
    mix_scr[rows, 0:D_POOL] = _dot(dpool, pmix_ref[...]) * pscale_ref[...]

    graw = z_scr[rows, GATE_OFF:GATE_OFF + LANES]
    g = graw + gbias_ref[...]

    log_i = g[:, GATE_I:GATE_I + HEADS]
    log_f = _log_sigmoid(g[:, GATE_F:GATE_F + HEADS])
    inter = log_f + m_ref[...]
    m_t = jnp.maximum(inter, log_i)
    w_inter = jnp.exp(inter - m_t)
    w_in_gate = jnp.exp(log_i - m_t)
    m_o_ref[...] = m_t
    for h in range(HEADS):
        q = z_scr[rows, MQ_OFF + h * HP:MQ_OFF + (h + 1) * HP] * (MLSTM_DH ** -0.5)
        k = z_scr[rows, MK_OFF + h * HP:MK_OFF + (h + 1) * HP]
        v = z_scr[rows, MV_OFF + h * HP:MV_OFF + (h + 1) * HP]
        n_old = n_ref[:, h * MLSTM_DH:(h + 1) * MLSTM_DH]
        dec = w_inter[:, h:h + 1]
        wig = w_in_gate[:, h:h + 1]
        for j in range(SG):
            c_old = c_ref[j, h]
            qcol = _column_bcast(q[j:j + 1, :])[0:MLSTM_DH, 0:MLSTM_DH]
            kcol = _column_bcast(k[j:j + 1, :])[0:MLSTM_DH, 0:MLSTM_DH]
            qc_scr[j:j + 1, h * HP:h * HP + MLSTM_DH] = jnp.sum(qcol * c_old, axis=0, keepdims=True)
            c_o_ref[j, h] = dec[j:j + 1, :] * c_old + kcol * (wig[j:j + 1, :] * v[j:j + 1, 0:MLSTM_DH])
        q96 = q[:, 0:MLSTM_DH]
        k96 = k[:, 0:MLSTM_DH]
        v96 = v[:, 0:MLSTM_DH]
        sc = jnp.sum(q96 * k96, axis=1, keepdims=True) * wig
        num = dec * qc_scr[:, h * HP:h * HP + MLSTM_DH] + sc * v96
        nq = dec * jnp.sum(q96 * n_old, axis=1, keepdims=True) + sc
        hval = num / jnp.maximum(jnp.abs(nq), jnp.exp(-m_t[:, h:h + 1]))
        n_o_ref[:, h * MLSTM_DH:(h + 1) * MLSTM_DH] = dec * n_old + wig * k96
        mu = jnp.mean(hval, axis=1, keepdims=True)
        dlt = hval - mu
        var = jnp.mean(dlt * dlt, axis=1, keepdims=True)
        hn = dlt * lax.rsqrt(var + LN_EPS) * mnorm_ref[:, h * HP:h * HP + MLSTM_DH]
        mo = z_scr[rows, MO_OFF + h * HP:MO_OFF + h * HP + MLSTM_DH]
        mix_scr[rows, YM_OFF + h * HP:YM_OFF + h * HP + MLSTM_DH] = jax.nn.sigmoid(mo) * hn

    ya = _dot(graw.astype(BF16), a2_ref[...]) + abias_ref[...]
    ea = jnp.exp(_log_sigmoid(ya) * (1.0 / GLA_TAU))
    gq = z_scr[rows, GQ_OFF:GQ_OFF + HEADS * KP] * (GLA_DK ** -0.5)
    gk = z_scr[rows, GK_OFF:GK_OFF + HEADS * KP]
    gqe = gq * ea
    lane_head = lax.broadcasted_iota(jnp.int32, (SG, HEADS * KP), 1) // KP
    qk_prod = gq * gk
    gvs = [z_scr[rows, GV_OFF + h * HP:GV_OFF + h * HP + GLA_DV] for h in range(HEADS)]
    for pair in range(HEADS // 2):
        pl_ = slice(pair * LANES, (pair + 1) * LANES)
        for j in range(SG):
            qecol = _column_bcast(gqe[j:j + 1, pl_])
            kcol = _column_bcast(gk[j:j + 1, pl_])
            eacol = _column_bcast(ea[j:j + 1, pl_])
            for hh in range(2):
                h = 2 * pair + hh
                kr = slice(hh * KP, hh * KP + GLA_DK)
                s_old = s_ref[j, h]
                vrow = gvs[h][j:j + 1, :]
                og_scr[j:j + 1, h * HP:h * HP + GLA_DV] = jnp.sum(
                    qecol[kr, 0:GLA_DV] * s_old, axis=0, keepdims=True)
                s_o_ref[j, h] = eacol[kr, 0:GLA_DV] * s_old + kcol[kr, 0:GLA_DV] * vrow
    for h in range(HEADS):
        gv = gvs[h]
        qk = jnp.sum(jnp.where(lane_head == h, qk_prod, 0.0), axis=1, keepdims=True)
        o = og_scr[:, h * HP:h * HP + GLA_DV] + qk * gv
        ms = jnp.mean(o * o, axis=1, keepdims=True)
        on = o * lax.rsqrt(ms + LN_EPS) * gnorm_ref[:, h * HP:h * HP + GLA_DV]
        gg = z_scr[rows, GG_OFF + h * HP:GG_OFF + h * HP + GLA_DV]
        mix_scr[rows, YG_OFF + h * HP:YG_OFF + h * HP + GLA_DV] = gg * jax.nn.sigmoid(gg) * on

    @pl.when(gi == n_g - 1)
    def _():
        y = ALPHA * x_ref[...] + _dot(mix_scr[...].astype(BF16), w_out_ref[...])
        xo_ref[...] = _layer_norm_rows(y, lng_ref[...], lnb_ref[...])


def _sample_mixer(x, states, consts, carried, layer):
    B = x.shape[0]
    out_shape = (jax.ShapeDtypeStruct((B, D_MODEL), F32),) + tuple(
        jax.ShapeDtypeStruct(s.shape, F32) for s in states)
    state_specs = [
        pl.BlockSpec((None, SG, POOL_BUF, D_POOL), lambda g: (layer, g, 0, 0)),
        pl.BlockSpec((None, SG, HEADS, MLSTM_DH, MLSTM_DH), lambda g: (layer, g, 0, 0, 0)),
        pl.BlockSpec((None, SG, HEADS * MLSTM_DH), lambda g: (layer, g, 0)),
        pl.BlockSpec((None, SG, HEADS), lambda g: (layer, g, 0)),
        pl.BlockSpec((None, SG, HEADS, GLA_DK, GLA_DV), lambda g: (layer, g, 0, 0, 0)),
    ]
    x_spec = pl.BlockSpec((B, D_MODEL), lambda g: (0, 0))
    n_in = 1 + len(states) + len(consts)
    return pl.pallas_call(
        functools.partial(_sample_mixer_kernel, len(carried)),
        grid=(B // SG,),
        in_specs=[x_spec] + state_specs + [_layer_spec(cst.shape, layer) for cst in consts]
        + [pl.BlockSpec(memory_space=pl.ANY)] * len(carried),
        out_specs=[x_spec] + state_specs,
        out_shape=out_shape,
        input_output_aliases={n_in + i: 1 + i for i in range(len(carried))},
        scratch_shapes=[
            pltpu.VMEM((B, N_Z), F32),
            pltpu.VMEM((B, N_MIX), F32),
            pltpu.VMEM((SG, HEADS * HP), F32),
            pltpu.VMEM((SG, HEADS * HP), F32),
            pltpu.VMEM((SG, D_POOL), F32),
        ],
        compiler_params=pltpu.CompilerParams(
            dimension_semantics=("arbitrary",), vmem_limit_bytes=VMEM_LIMIT),
        name="sample_mixer",
    )(x, *states, *consts, *carried)


def _ffn_kernel(x_ref, p_ref, wg_ref, wu_ref, wd_ref, wp_ref, wpg_ref, lng_ref, lnb_ref, o_ref):
    x = x_ref[...]
    xb = x.astype(BF16)
    acc = ALPHA * x + _dot(p_ref[...].astype(BF16), wp_ref[...]) * jax.nn.sigmoid(_dot(xb, wpg_ref[...]))
    for c0 in range(0, D_FF, FF_CHUNK):
        c1 = min(c0 + FF_CHUNK, D_FF)
        gate = _dot(xb, wg_ref[:, c0:c1])
        up = _dot(xb, wu_ref[:, c0:c1])
        hid = (gate * jax.nn.sigmoid(gate) * up).astype(BF16)
        acc = acc + _dot(hid, wd_ref[c0:c1, :])
    o_ref[...] = _layer_norm_rows(acc, lng_ref[...], lnb_ref[...])


def _ffn(x, p, consts, layer, tm):
    M = x.shape[0]
    return pl.pallas_call(
        _ffn_kernel,
        grid=(M // tm,),
        in_specs=[pl.BlockSpec((tm, D_MODEL), lambda i: (i, 0)),
                  pl.BlockSpec((None, tm, D_PLE), lambda i: (layer, i, 0))]
        + [_layer_spec(c.shape, layer) for c in consts],
        out_specs=pl.BlockSpec((tm, D_MODEL), lambda i: (i, 0)),
        out_shape=jax.ShapeDtypeStruct((M, D_MODEL), F32),
        compiler_params=pltpu.CompilerParams(
            dimension_semantics=("arbitrary",), vmem_limit_bytes=VMEM_LIMIT),
        name="ffn",
    )(x, p, *consts)


def kernel(x_prompt, x_sample, p_prompt, p_sample, state_pool, state_mlstm_C, state_mlstm_n, state_mlstm_m, state_gla_S, w_in, mlstm_i_bias, mlstm_f_bias, mlstm_norm_w, gla_a2, gla_a_bias, gla_norm_w, pool_mix, pool_scale, w_out, ln1_g, ln1_b, w_gate, w_up, w_down, w_ple, w_ple_gate, ln2_g, ln2_b):
    Bp, Tp, _ = x_prompt.shape
    Bs = x_sample.shape[0]

    w_in_p = _pad_in_proj(w_in).astype(BF16)
    w_out_p = _pad_out_proj(w_out).astype(BF16)
    gbias = jnp.zeros((DEPTH, 1, LANES), F32)
    gbias = gbias.at[:, 0, GATE_I:GATE_I + HEADS].set(mlstm_i_bias.astype(F32))
    gbias = gbias.at[:, 0, GATE_F:GATE_F + HEADS].set(mlstm_f_bias.astype(F32))
    a2_p = jnp.zeros((DEPTH, LANES, HEADS * KP), F32)
    a2_p = a2_p.at[:, GATE_A:GATE_A + GLA_RANK, :].set(_pad_heads(gla_a2.astype(F32), GLA_DK, KP)).astype(BF16)
    abias_p = _pad_heads(gla_a_bias.astype(F32), GLA_DK, KP)[:, None, :]
    mnorm_p = _pad_heads(mlstm_norm_w.astype(F32), MLSTM_DH, HP)[:, None, :]
    gnorm_p = _pad_heads(gla_norm_w.astype(F32), GLA_DV, HP)[:, None, :]
    n_grp = len(POOL_WINDOWS)
    eye = jnp.eye(n_grp, dtype=F32)
    pmix_bd = (pool_mix.astype(F32)[:, :, :, None, :] * eye[None, :, None, :, None]).reshape(
        DEPTH, D_POOL, D_POOL).astype(BF16)
    pscale = pool_scale.astype(F32)[:, None, :]
    row = lambda a: a.astype(F32)[:, None, :]
    ln1_g_r, ln1_b_r, ln2_g_r, ln2_b_r = row(ln1_g), row(ln1_b), row(ln2_g), row(ln2_b)
    wg_b, wu_b, wd_b = w_gate.astype(BF16), w_up.astype(BF16), w_down.astype(BF16)
    wp_b, wpg_b = w_ple.astype(BF16), w_ple_gate.astype(BF16)

    mixer_consts = (w_in_p, w_out_p, gbias, a2_p, abias_p, mnorm_p, gnorm_p, pmix_bd, pscale, ln1_g_r, ln1_b_r)
    ffn_consts = (wg_b, wu_b, wd_b, wp_b, wpg_b, ln2_g_r, ln2_b_r)
    xp = x_prompt
    xs = x_sample.reshape(Bs, D_MODEL)
    pp = p_prompt.reshape(DEPTH, Bp * Tp, D_PLE)
    ps = p_sample.reshape(DEPTH, Bs, D_PLE)
    sample_states = (state_pool, state_mlstm_C, state_mlstm_n.reshape(DEPTH, Bs, HEADS * MLSTM_DH),
                     state_mlstm_m, state_gla_S)
    outs_p = []
    carried = ()
    for l in range(DEPTH):
        xp, pool_p, c_p, n_p, m_p, s_p = _prompt_mixer(xp, mixer_consts, l)
        xp = _ffn(xp.reshape(Bp * Tp, D_MODEL), pp, ffn_consts, l, tm=512).reshape(Bp, Tp, D_MODEL)
        outs_p.append((pool_p, c_p, n_p[:, :, :MLSTM_DH], m_p[:, 0, GATE_F:GATE_F + HEADS], s_p))
        xs, *carried = _sample_mixer(xs, sample_states, mixer_consts, tuple(carried), l)
        xs = _ffn(xs, ps, ffn_consts, l, tm=Bs)

    pool_s, c_s, n_s, m_s, s_s = carried
    stack = lambda j: jnp.stack([o[j] for o in outs_p], axis=0)
    return ((xp, xs.reshape(Bs, 1, D_MODEL)) + tuple(stack(j) for j in range(5))
            + (pool_s, c_s, n_s.reshape(DEPTH, Bs, HEADS, MLSTM_DH), m_s, s_s))
```

```python
import functools

import numpy as np
import jax
import jax.numpy as jnp
from jax import lax
from jax.experimental import pallas as pl
from jax.experimental.pallas import tpu as pltpu

F32 = jnp.float32
BF16 = jnp.bfloat16

D_MODEL = 1024
DEPTH = 4
PAST_LEN = 16384
D_POOL = 256
POOL_WINDOWS = (2, 4, 8, 16)
POOL_GROUP = 64
POOL_BUF = 15
D_MLSTM = 384
HEADS = 4
MLSTM_DH = 96
D_GLA = 384
GLA_DV = 96
GLA_DK = 48
GLA_RANK = 16
GLA_TAU = 16.0
D_MIX = 1024
D_FF = 2816
D_PLE = 256
CHUNK = 64
ALPHA = (2 * DEPTH) ** 0.25
LN_EPS = 1e-5

LANES = 128
SUBLANES = 8
VMEM_LIMIT = 56 * 1024 * 1024

HP = LANES
KP = 64
U_OFF = 0
MQ_OFF = U_OFF + D_POOL
MK_OFF = MQ_OFF + HEADS * HP
MV_OFF = MK_OFF + HEADS * HP
MO_OFF = MV_OFF + HEADS * HP
GQ_OFF = MO_OFF + HEADS * HP
GK_OFF = GQ_OFF + HEADS * KP
GV_OFF = GK_OFF + HEADS * KP
GG_OFF = GV_OFF + HEADS * HP
GATE_OFF = GG_OFF + HEADS * HP
N_Z = GATE_OFF + LANES
GATE_I = 0
GATE_F = 4
GATE_A = 8
YM_OFF = D_POOL
YG_OFF = YM_OFF + HEADS * HP
N_MIX = YG_OFF + HEADS * HP

TT = 512
CHUNK_UNROLL = 2
SG = SUBLANES
FF_CHUNK = 512


def _pad_heads(a, width, padded):
    lead = a.shape[:-1]
    a = a.reshape(lead + (HEADS, width))
    a = jnp.pad(a, [(0, 0)] * len(lead) + [(0, 0), (0, padded - width)])
    return a.reshape(lead + (HEADS * padded,))


def _pad_head_rows(a, width, padded):
    L, _, K = a.shape
    a = jnp.pad(a.reshape(L, HEADS, width, K), ((0, 0), (0, 0), (0, padded - width), (0, 0)))
    return a.reshape(L, HEADS * padded, K)


def _pad_in_proj_t(w_in):
    sizes = (D_POOL, D_MLSTM, D_MLSTM, D_MLSTM, HEADS, HEADS, D_MLSTM,
             HEADS * GLA_DK, HEADS * GLA_DK, D_GLA, GLA_RANK, D_GLA)
    w_t = jnp.transpose(w_in, (0, 2, 1))
    u, mq, mk, mv, mi, mf, mo, gq, gk, gv, ga, gg = jnp.split(w_t, np.cumsum(sizes)[:-1].tolist(), axis=1)
    gate_pad = jnp.zeros((w_t.shape[0], LANES - GATE_A - GLA_RANK, w_t.shape[2]), w_t.dtype)
    wide = lambda a: _pad_head_rows(a, MLSTM_DH, HP)
    narrow = lambda a: _pad_head_rows(a, GLA_DK, KP)
    return jnp.concatenate([u, wide(mq), wide(mk), wide(mv), wide(mo), narrow(gq), narrow(gk), wide(gv),
                            wide(gg), mi, mf, ga, gate_pad], axis=1)


def _pad_out_proj(w_out):
    pool = w_out[:, :D_POOL]
    heads = w_out[:, D_POOL:].reshape(w_out.shape[0], 2 * HEADS, MLSTM_DH, D_MODEL)
    heads = jnp.pad(heads, ((0, 0), (0, 0), (0, HP - MLSTM_DH), (0, 0)))
    return jnp.concatenate([pool, heads.reshape(w_out.shape[0], 2 * HEADS * HP, D_MODEL)], axis=1)


def _dot(a, b):
    return jnp.dot(a, b, preferred_element_type=F32)


def _dot_nt(a, b):
    return lax.dot_general(a, b, (((1,), (1,)), ((), ())), preferred_element_type=F32)


def _log_sigmoid(x):
    return jnp.minimum(x, 0.0) - jnp.log1p(jnp.exp(-jnp.abs(x)))


def _cumsum_rows(x):
    n = x.shape[0]
    rid = lax.broadcasted_iota(jnp.int32, x.shape, 0)
    sh = 1
    while sh < n:
        x = x + jnp.where(rid >= sh, pltpu.roll(x, sh, axis=0), 0.0)
        sh *= 2
    return x


def _layer_norm_rows(y, g, b):
    mu = jnp.mean(y, axis=-1, keepdims=True)
    d = y - mu
    var = jnp.mean(d * d, axis=-1, keepdims=True)
    return d * lax.rsqrt(var + LN_EPS) * g + b


def _pool_select(lane, a, b):
    return jnp.where((lane & (LANES - 1)) < POOL_GROUP, a, b)


def _prompt_mixer_kernel(x_ref, w_in_ref, w_out_ref, gbias_ref, a2_ref, abias_ref, mnorm_ref,
                         gnorm_ref, pmix_ref, pscale_ref, lng_ref, lnb_ref,
                         xo_ref, pool_o_ref, c_o_ref, n_o_ref, m_o_ref, s_o_ref,
                         z_scr, mix_scr, ext_scr, c_scr, m_scr, s_scr):
    ti = pl.program_id(1)
    n_t = pl.num_programs(1)

    @pl.when(ti == 0)
    def _():
        ext_scr[0:2 * SUBLANES, :] = jnp.zeros((2 * SUBLANES, D_POOL), F32)
        c_scr[...] = jnp.zeros(c_scr.shape, F32)
        m_scr[...] = jnp.zeros(m_scr.shape, F32)
        s_scr[...] = jnp.zeros(s_scr.shape, F32)

    xb = x_ref[0].astype(BF16)
    for c0 in range(0, N_Z, 512):
        c1 = min(c0 + 512, N_Z)
        z_scr[:, c0:c1] = _dot_nt(xb, w_in_ref[c0:c1, :])

    hist = 2 * SUBLANES
    u = z_scr[:, U_OFF:U_OFF + D_POOL]
    ext_scr[hist:hist + TT, :] = u
    ext = ext_scr[...]
    e0 = ext[:, 0:LANES]
    e1 = ext[:, LANES:2 * LANES]
    a2s = e0 + pltpu.roll(e0, 1, axis=0)
    a4s = a2s + pltpu.roll(a2s, 2, axis=0)
    b2s = e1 + pltpu.roll(e1, 1, axis=0)
    b4s = b2s + pltpu.roll(b2s, 2, axis=0)
    b8s = b4s + pltpu.roll(b4s, 4, axis=0)
    b16s = b8s + pltpu.roll(b8s, 8, axis=0)
    lane = lax.broadcasted_iota(jnp.int32, (TT, LANES), 1)
    pos1 = (lax.broadcasted_iota(jnp.int32, (TT, LANES), 0) + ti * TT + 1).astype(F32)
    ws0 = _pool_select(lane, a2s[hist:], a4s[hist:])
    ws1 = _pool_select(lane, b8s[hist:], b16s[hist:])
    cnt0 = jnp.minimum(_pool_select(lane, 2.0, 4.0), pos1)
    cnt1 = jnp.minimum(_pool_select(lane, 8.0, 16.0), pos1)
    d0 = ws0 / cnt0 - u[:, 0:LANES]
    d1 = ws1 / cnt1 - u[:, LANES:]
    dpool = jnp.concatenate([d0, d1], axis=1).astype(BF16)
    y_pool = _dot(dpool, pmix_ref[...]) * pscale_ref[...]
    mix_scr[:, 0:D_POOL] = y_pool.astype(BF16)
    tail = ext_scr[TT:TT + hist, :]
    ext_scr[0:hist, :] = tail

    @pl.when(ti == n_t - 1)
    def _():
        pool_o_ref[0] = ext_scr[1:hist, :]

    gbias = gbias_ref[...]
    abias = abias_ref[...]
    lane_g = lax.broadcasted_iota(jnp.int32, (CHUNK, LANES), 1)
    tcol = lax.broadcasted_iota(jnp.int32, (CHUNK, CHUNK), 0)
    srow = lax.broadcasted_iota(jnp.int32, (CHUNK, CHUNK), 1)
    causal = srow <= tcol

    def chunk_body(c, carry):
        r0 = pl.multiple_of(c * CHUNK, CHUNK)
        rows = pl.ds(r0, CHUNK)
        graw = z_scr[rows, GATE_OFF:GATE_OFF + LANES]

        g = graw + gbias
        fl = _log_sigmoid(g)
        b_all = _cumsum_rows(fl)
        i_sh = pltpu.roll(g, GATE_F - GATE_I, axis=1)
        m_row = m_scr[0:1, :]
        inter_all = b_all + m_row
        row_all = jnp.transpose(i_sh - b_all)
        m_new_row = m_row
        for h in range(HEADS):
            gl = GATE_F + h
            b_col = b_all[:, gl:gl + 1]
            dmat = jnp.where(causal, b_col + row_all[gl:gl + 1, :], -jnp.inf)
            inter = inter_all[:, gl:gl + 1]
            m_t = jnp.maximum(inter, jnp.max(dmat, axis=1, keepdims=True))
            e = jnp.exp(dmat - m_t)
            w_inter = jnp.exp(inter - m_t)
            q = (z_scr[rows, MQ_OFF + h * HP:MQ_OFF + (h + 1) * HP] * (MLSTM_DH ** -0.5)).astype(BF16)
            kf = z_scr[rows, MK_OFF + h * HP:MK_OFF + (h + 1) * HP]
            v = z_scr[rows, MV_OFF + h * HP:MV_OFF + (h + 1) * HP]
            v_aug = jnp.where(lane_g == MLSTM_DH, 1.0, v).astype(BF16)
            s = _dot_nt(q, kf.astype(BF16)) * e
            c_aug = c_scr[h]
            num = w_inter * _dot(q, c_aug.astype(BF16)) + _dot(s.astype(BF16), v_aug)
            nq = jnp.sum(jnp.where(lane_g == MLSTM_DH, num, 0.0), axis=1, keepdims=True)
            hval = num / jnp.maximum(jnp.abs(nq), jnp.exp(-m_t))
            valid = lane_g < MLSTM_DH
            mu = jnp.sum(jnp.where(valid, hval, 0.0), axis=1, keepdims=True) * (1.0 / MLSTM_DH)
            dlt = jnp.where(valid, hval - mu, 0.0)
            var = jnp.sum(dlt * dlt, axis=1, keepdims=True) * (1.0 / MLSTM_DH)
            hn = dlt * lax.rsqrt(var + LN_EPS) * mnorm_ref[:, h * HP:(h + 1) * HP]
            mo = z_scr[rows, MO_OFF + h * HP:MO_OFF + (h + 1) * HP]
            mix_scr[rows, YM_OFF + h * HP:YM_OFF + (h + 1) * HP] = (jax.nn.sigmoid(mo) * hn).astype(BF16)
            b_last = b_col[CHUNK - 1:CHUNK, :]
            m_prev = m_row[:, gl:gl + 1]
            m_new = m_t[CHUNK - 1:CHUNK, :]
            decay = jnp.exp(b_last + m_prev - m_new)
            w_end = jnp.exp(b_last - b_col + i_sh[:, gl:gl + 1] - m_new)
            kw_t = jnp.transpose(kf * w_end).astype(BF16)
            c_scr[h] = decay * c_aug + _dot(kw_t, v_aug)
            m_new_row = jnp.where(lax.broadcasted_iota(jnp.int32, (1, LANES), 1) == gl, m_new, m_new_row)
        m_scr[0:1, :] = m_new_row

        ya = _dot(graw.astype(BF16), a2_ref[...]) + abias
        la = _log_sigmoid(ya) * (1.0 / GLA_TAU)
        bc = _cumsum_rows(la)
        gq = z_scr[rows, GQ_OFF:GQ_OFF + HEADS * KP] * (GLA_DK ** -0.5)
        gk = z_scr[rows, GK_OFF:GK_OFF + HEADS * KP]
        width = HEADS * KP
        t_id = lax.broadcasted_iota(jnp.int32, (CHUNK, width), 0)
        lane_head = lax.broadcasted_iota(jnp.int32, (CHUNK, width), 1) // KP

        def stack_heads(a):
            return jnp.concatenate([jnp.where(lane_head == h, a, 0.0) for h in range(HEADS)],
                                   axis=0).astype(BF16)

        trow = lax.broadcasted_iota(jnp.int32, (HEADS * CHUNK, CHUNK), 0) & (CHUNK - 1)
        scol = lax.broadcasted_iota(jnp.int32, (HEADS * CHUNK, CHUNK), 1)
        amat = jnp.where(trow == scol, _dot_nt(stack_heads(gq), gk.astype(BF16)), 0.0)
        half = CHUNK // 2
        while half >= 1:
            blk = 2 * half
            upper = (t_id & half) != 0
            if blk >= SUBLANES:
                ref_rows = jnp.concatenate(
                    [jnp.broadcast_to(bc[b0 + half - 1:b0 + half, :], (blk, width))
                     for b0 in range(0, CHUNK, blk)], axis=0)
            else:
                off = t_id & (blk - 1)
                ref_rows = bc
                for o in range(blk):
                    sh = (o - (half - 1)) % CHUNK
                    if sh != 0:
                        ref_rows = jnp.where(off == o, pltpu.roll(bc, sh, axis=0), ref_rows)
            dlt = bc - ref_rows
            qe = jnp.where(upper, gq * jnp.exp(jnp.minimum(dlt, 0.0)), 0.0)
            ke = jnp.where(upper, 0.0, gk * jnp.exp(jnp.minimum(-dlt, 0.0)))
            p = _dot_nt(stack_heads(qe), ke.astype(BF16))
            shift = blk.bit_length() - 1
            amat = amat + jnp.where((trow >> shift) == (scol >> shift), p, 0.0)
            half //= 2

        o_inter = _dot(stack_heads(gq * jnp.exp(bc)), s_scr[...].astype(BF16))
        bc_t = jnp.transpose(bc)
        gk_t = jnp.transpose(gk)
        last_t = bc_t[:, CHUNK - 1:CHUNK]
        ke_t = (gk_t * jnp.exp(last_t - bc_t)).astype(BF16)
        s_dec = jnp.exp(last_t)
        for h in range(HEADS):
            hr = slice(h * CHUNK, (h + 1) * CHUNK)
            gv = z_scr[rows, GV_OFF + h * HP:GV_OFF + (h + 1) * HP].astype(BF16)
            o = o_inter[hr] + _dot(amat[hr].astype(BF16), gv)
            ms = jnp.sum(o * o, axis=1, keepdims=True) * (1.0 / GLA_DV)
            on = o * lax.rsqrt(ms + LN_EPS) * gnorm_ref[:, h * HP:(h + 1) * HP]
            gg = z_scr[rows, GG_OFF + h * HP:GG_OFF + (h + 1) * HP]
            mix_scr[rows, YG_OFF + h * HP:YG_OFF + (h + 1) * HP] = (gg * jax.nn.sigmoid(gg) * on).astype(BF16)
            kr = slice(h * KP, (h + 1) * KP)
            s_scr[kr, :] = s_dec[kr] * s_scr[kr, :] + _dot(ke_t[kr], gv)
        return carry

    lax.fori_loop(0, TT // CHUNK, chunk_body, 0, unroll=CHUNK_UNROLL)

    y = ALPHA * x_ref[0] + _dot(mix_scr[...], w_out_ref[...])
    xo_ref[0] = _layer_norm_rows(y, lng_ref[...], lnb_ref[...])

    @pl.when(ti == n_t - 1)
    def _():
        for h in range(HEADS):
            c_aug = c_scr[h]
            c_o_ref[0, h] = c_aug[0:MLSTM_DH, 0:MLSTM_DH]
            n_o_ref[0, h:h + 1, :] = jnp.transpose(c_aug)[MLSTM_DH:MLSTM_DH + 1, :]
            s_o_ref[0, h] = s_scr[h * KP:h * KP + GLA_DK, 0:GLA_DV]
        m_o_ref[0] = m_scr[...]


def _layer_spec(shape, layer):
    nd = len(shape)
    return pl.BlockSpec((None,) + tuple(shape[1:]), lambda *_: (layer,) + (0,) * (nd - 1),
                        pipeline_mode=pl.Buffered(1))


def _prompt_mixer(x, consts, layer):
    B, T, _ = x.shape
    out_shape = (
        jax.ShapeDtypeStruct((B, T, D_MODEL), F32),
        jax.ShapeDtypeStruct((B, POOL_BUF, D_POOL), F32),
        jax.ShapeDtypeStruct((B, HEADS, MLSTM_DH, MLSTM_DH), F32),
        jax.ShapeDtypeStruct((B, HEADS, HP), F32),
        jax.ShapeDtypeStruct((B, SUBLANES, LANES), F32),
        jax.ShapeDtypeStruct((B, HEADS, GLA_DK, GLA_DV), F32),
    )
    out_specs = (
        pl.BlockSpec((1, TT, D_MODEL), lambda b, t: (b, t, 0)),
        pl.BlockSpec((1, POOL_BUF, D_POOL), lambda b, t: (b, 0, 0)),
        pl.BlockSpec((1, HEADS, MLSTM_DH, MLSTM_DH), lambda b, t: (b, 0, 0, 0)),
        pl.BlockSpec((1, HEADS, HP), lambda b, t: (b, 0, 0)),
        pl.BlockSpec((1, SUBLANES, LANES), lambda b, t: (b, 0, 0)),
        pl.BlockSpec((1, HEADS, GLA_DK, GLA_DV), lambda b, t: (b, 0, 0, 0)),
    )
    return pl.pallas_call(
        _prompt_mixer_kernel,
        grid=(B, T // TT),
        in_specs=[pl.BlockSpec((1, TT, D_MODEL), lambda b, t: (b, t, 0))]
        + [_layer_spec(c.shape, layer) for c in consts],
        out_specs=out_specs,
        out_shape=out_shape,
        scratch_shapes=[
            pltpu.VMEM((TT, N_Z), F32),
            pltpu.VMEM((TT, N_MIX), BF16),
            pltpu.VMEM((TT + 2 * SUBLANES, D_POOL), F32),
            pltpu.VMEM((HEADS, HP, HP), F32),
            pltpu.VMEM((SUBLANES, LANES), F32),
            pltpu.VMEM((HEADS * KP, HP), F32),
        ],
        compiler_params=pltpu.CompilerParams(
            dimension_semantics=("arbitrary", "arbitrary"), vmem_limit_bytes=VMEM_LIMIT),
        name="prompt_mixer",
    )(x, *consts)


def _column_bcast(row):
    return jnp.transpose(jnp.broadcast_to(row, (LANES, LANES)))


def _sample_mixer_kernel(n_carried, *refs):
    (x_ref, pool_ref, c_ref, n_ref, m_ref, s_ref,
     w_in_ref, w_out_ref, gbias_ref, a2_ref, abias_ref, mnorm_ref,
     gnorm_ref, pmix_ref, pscale_ref, lng_ref, lnb_ref) = refs[:17]
    (xo_ref, pool_o_ref, c_o_ref, n_o_ref, m_o_ref, s_o_ref,
     z_scr, mix_scr, qc_scr, og_scr, psum_scr) = refs[17 + n_carried:]
    gi = pl.program_id(0)
    n_g = pl.num_programs(0)

    @pl.when(gi == 0)
    def _():
        mix_scr[...] = jnp.zeros(mix_scr.shape, F32)
        xb = x_ref[...].astype(BF16)
        for c0 in range(0, N_Z, 512):
            c1 = min(c0 + 512, N_Z)
            z_scr[:, c0:c1] = _dot(xb, w_in_ref[:, c0:c1])

    r0 = pl.multiple_of(gi * SG, SG)
    rows = pl.ds(r0, SG)

    u = z_scr[rows, U_OFF:U_OFF + D_POOL]

    def window_of(lane):
        grp = lane // POOL_GROUP
        w = jnp.full(lane.shape, POOL_WINDOWS[0], jnp.int32)
        for gidx in range(1, len(POOL_WINDOWS)):
            w = jnp.where(grp == gidx, POOL_WINDOWS[gidx], w)
        return w

    hist_row = lax.broadcasted_iota(jnp.int32, (POOL_BUF, D_POOL), 0)
    in_window = hist_row > POOL_BUF - window_of(lax.broadcasted_iota(jnp.int32, (POOL_BUF, D_POOL), 1))
    for j in range(SG):
        psum_scr[j:j + 1, :] = jnp.sum(jnp.where(in_window, pool_ref[j], 0.0), axis=0, keepdims=True)
        pool_o_ref[j, 0:POOL_BUF - 1, :] = pool_ref[j, 1:POOL_BUF, :]
        pool_o_ref[j, POOL_BUF - 1:POOL_BUF, :] = u[j:j + 1, :]
    cnt = jnp.minimum(window_of(lax.broadcasted_iota(jnp.int32, (SG, D_POOL), 1)), PAST_LEN + 1).astype(F32)
    dpool = ((psum_scr[...] + u) / cnt - u).astype(BF16)
    mix_scr[rows, 0:D_POOL] = _dot(dpool, pmix_ref[...]) * pscale_ref[...]

    graw = z_scr[rows, GATE_OFF:GATE_OFF + LANES]
    g = graw + gbias_ref[...]

    log_i = g[:, GATE_I:GATE_I + HEADS]
    log_f = _log_sigmoid(g[:, GATE_F:GATE_F + HEADS])
    inter = log_f + m_ref[...]
    m_t = jnp.maximum(inter, log_i)
    w_inter = jnp.exp(inter - m_t)
    w_in_gate = jnp.exp(log_i - m_t)
    m_o_ref[...] = m_t
    for h in range(HEADS):
        q = z_scr[rows, MQ_OFF + h * HP:MQ_OFF + (h + 1) * HP] * (MLSTM_DH ** -0.5)
        k = z_scr[rows, MK_OFF + h * HP:MK_OFF + (h + 1) * HP]
        v = z_scr[rows, MV_OFF + h * HP:MV_OFF + (h + 1) * HP]
        n_old = n_ref[:, h * MLSTM_DH:(h + 1) * MLSTM_DH]
        dec = w_inter[:, h:h + 1]
        wig = w_in_gate[:, h:h + 1]
        for j in range(SG):
            c_old = c_ref[j, h]
            qcol = _column_bcast(q[j:j + 1, :])[0:MLSTM_DH, 0:MLSTM_DH]
            kcol = _column_bcast(k[j:j + 1, :])[0:MLSTM_DH, 0:MLSTM_DH]
            qc_scr[j:j + 1, h * HP:h * HP + MLSTM_DH] = jnp.sum(qcol * c_old, axis=0, keepdims=True)
            c_o_ref[j, h] = dec[j:j + 1, :] * c_old + kcol * (wig[j:j + 1, :] * v[j:j + 1, 0:MLSTM_DH])
        q96 = q[:, 0:MLSTM_DH]
        k96 = k[:, 0:MLSTM_DH]
        v96 = v[:, 0:MLSTM_DH]
        sc = jnp.sum(q96 * k96, axis=1, keepdims=True) * wig
        num = dec * qc_scr[:, h * HP:h * HP + MLSTM_DH] + sc * v96
        nq = dec * jnp.sum(q96 * n_old, axis=1, keepdims=True) + sc
        hval = num / jnp.maximum(jnp.abs(nq), jnp.exp(-m_t[:, h:h + 1]))
        n_o_ref[:, h * MLSTM_DH:(h + 1) * MLSTM_DH] = dec * n_old + wig * k96
        mu = jnp.mean(hval, axis=1, keepdims=True)
        dlt = hval - mu
        var = jnp.mean(dlt * dlt, axis=1, keepdims=True)
        hn = dlt * lax.rsqrt(var + LN_EPS) * mnorm_ref[:, h * HP:h * HP + MLSTM_DH]
        mo = z_scr[rows, MO_OFF + h * HP:MO_OFF + h * HP + MLSTM_DH]
        mix_scr[rows, YM_OFF + h * HP:YM_OFF + h * HP + MLSTM_DH] = jax.nn.sigmoid(mo) * hn

    ya = _dot(graw.astype(BF16), a2_ref[...]) + abias_ref[...]
    ea = jnp.exp(_log_sigmoid(ya) * (1.0 / GLA_TAU))
    gq = z_scr[rows, GQ_OFF:GQ_OFF + HEADS * KP] * (GLA_DK ** -0.5)
    gk = z_scr[rows, GK_OFF:GK_OFF + HEADS * KP]
    gqe = gq * ea
    lane_head = lax.broadcasted_iota(jnp.int32, (SG, HEADS * KP), 1) // KP
    qk_prod = gq * gk
    gvs = [z_scr[rows, GV_OFF + h * HP:GV_OFF + h * HP + GLA_DV] for h in range(HEADS)]
    for pair in range(HEADS // 2):
        pl_ = slice(pair * LANES, (pair + 1) * LANES)
        for j in range(SG):
            qecol = _column_bcast(gqe[j:j + 1, pl_])
            kcol = _column_bcast(gk[j:j + 1, pl_])
            eacol = _column_bcast(ea[j:j + 1, pl_])
            for hh in range(2):
                h = 2 * pair + hh
                kr = slice(hh * KP, hh * KP + GLA_DK)
                s_old = s_ref[j, h]
                vrow = gvs[h][j:j + 1, :]
                og_scr[j:j + 1, h * HP:h * HP + GLA_DV] = jnp.sum(
                    qecol[kr, 0:GLA_DV] * s_old, axis=0, keepdims=True)
                s_o_ref[j, h] = eacol[kr, 0:GLA_DV] * s_old + kcol[kr, 0:GLA_DV] * vrow
    for h in range(HEADS):
        gv = gvs[h]
        qk = jnp.sum(jnp.where(lane_head == h, qk_prod, 0.0), axis=1, keepdims=True)
        o = og_scr[:, h * HP:h * HP + GLA_DV] + qk * gv
        ms = jnp.mean(o * o, axis=1, keepdims=True)
        on = o * lax.rsqrt(ms + LN_EPS) * gnorm_ref[:, h * HP:h * HP + GLA_DV]
        gg = z_scr[rows, GG_OFF + h * HP:GG_OFF + h * HP + GLA_DV]
        mix_scr[rows, YG_OFF + h * HP:YG_OFF + h * HP + GLA_DV] = gg * jax.nn.sigmoid(gg) * on

    @pl.when(gi == n_g - 1)
    def _():
        y = ALPHA * x_ref[...] + _dot(mix_scr[...].astype(BF16), w_out_ref[...])
        xo_ref[...] = _layer_norm_rows(y, lng_ref[...], lnb_ref[...])


def _sample_mixer(x, states, consts, carried, layer):
    B = x.shape[0]
    out_shape = (jax.ShapeDtypeStruct((B, D_MODEL), F32),) + tuple(
        jax.ShapeDtypeStruct(s.shape, F32) for s in states)
    state_specs = [
        pl.BlockSpec((None, SG, POOL_BUF, D_POOL), lambda g: (layer, g, 0, 0)),
        pl.BlockSpec((None, SG, HEADS, MLSTM_DH, MLSTM_DH), lambda g: (layer, g, 0, 0, 0)),
        pl.BlockSpec((None, SG, HEADS * MLSTM_DH), lambda g: (layer, g, 0)),
        pl.BlockSpec((None, SG, HEADS), lambda g: (layer, g, 0)),
        pl.BlockSpec((None, SG, HEADS, GLA_DK, GLA_DV), lambda g: (layer, g, 0, 0, 0)),
    ]
    x_spec = pl.BlockSpec((B, D_MODEL), lambda g: (0, 0))
    n_in = 1 + len(states) + len(consts)
    return pl.pallas_call(
        functools.partial(_sample_mixer_kernel, len(carried)),
        grid=(B // SG,),
        in_specs=[x_spec] + state_specs + [_layer_spec(cst.shape, layer) for cst in consts]
        + [pl.BlockSpec(memory_space=pl.ANY)] * len(carried),
        out_specs=[x_spec] + state_specs,
        out_shape=out_shape,
        input_output_aliases={n_in + i: 1 + i for i in range(len(carried))},
        scratch_shapes=[
            pltpu.VMEM((B, N_Z), F32),
            pltpu.VMEM((B, N_MIX), F32),
            pltpu.VMEM((SG, HEADS * HP), F32),
            pltpu.VMEM((SG, HEADS * HP), F32),
            pltpu.VMEM((SG, D_POOL), F32),
        ],
        compiler_params=pltpu.CompilerParams(
            dimension_semantics=("arbitrary",), vmem_limit_bytes=VMEM_LIMIT),
        name="sample_mixer",
    )(x, *states, *consts, *carried)


K_SPLIT = 2
CK = MLSTM_DH // K_SPLIT
SK = GLA_DK // K_SPLIT


def _row_of(a, idx):
    sub = lax.broadcasted_iota(jnp.int32, a.shape, 0)
    return jnp.sum(jnp.where(sub == idx, a, 0.0), axis=0, keepdims=True)


def _sample_step_kernel(n_carried, *refs):
    (x_ref, pool_ref, c_ref, n_ref, m_ref, s_ref,
     w_in_ref, w_out_ref, gbias_ref, a2t_ref, abias_ref, mnorm_ref, gnorm_ref,
     pmix_ref, pscale_ref, lng_ref, lnb_ref) = refs[:17]
    (xo_ref, pool_o_ref, c_o_ref, n_o_ref, m_o_ref, s_o_ref,
     zt_scr, mixt_scr, gate_scr, ea_scr, num_scr, oin_scr) = refs[17 + n_carried:]
    h = pl.program_id(0)
    kh = pl.program_id(1)
    B = x_ref.shape[0]
    first = jnp.logical_and(h == 0, kh == 0)
    last = jnp.logical_and(h == HEADS - 1, kh == K_SPLIT - 1)

    @pl.when(first)
    def _():
        mixt_scr[...] = jnp.zeros(mixt_scr.shape, F32)
        xb = x_ref[...].astype(BF16)
        for c0 in range(0, N_Z, LANES):
            zt_scr[c0:c0 + LANES, :] = _dot_nt(w_in_ref[c0:c0 + LANES, :], xb)
        g = zt_scr[GATE_OFF:GATE_OFF + LANES, :]
        gb = g + gbias_ref[...]
        log_i = gb[GATE_I:GATE_I + HEADS]
        log_f = _log_sigmoid(gb[GATE_F:GATE_F + HEADS])
        inter = log_f + m_ref[...]
        m_t = jnp.maximum(inter, log_i)
        m_o_ref[...] = m_t
        gate_scr[0:HEADS, :] = jnp.exp(inter - m_t)
        gate_scr[HEADS:2 * HEADS, :] = jnp.exp(log_i - m_t)
        gate_scr[2 * HEADS:3 * HEADS, :] = jnp.exp(-m_t)
        ya = _dot(a2t_ref[...], g.astype(BF16)) + abias_ref[...]
        ea_scr[...] = jnp.exp(_log_sigmoid(ya) * (1.0 / GLA_TAU))

    gates = gate_scr[...]
    dec = _row_of(gates, h)
    wig = _row_of(gates, h + HEADS)
    floor = _row_of(gates, h + 2 * HEADS)

    def head_rows(off, width, count, start=0):
        return zt_scr[pl.ds(pl.multiple_of(off + h * width + start, SUBLANES), count), :]

    k0 = kh * CK
    q_sl = head_rows(MQ_OFF, HP, CK, k0) * (MLSTM_DH ** -0.5)
    k_sl = head_rows(MK_OFF, HP, CK, k0) * wig
    v_t = head_rows(MV_OFF, HP, MLSTM_DH)
    acc = jnp.zeros((MLSTM_DH, B), F32)
    for k in range(CK):
        c_old = c_ref[k]
        acc = acc + q_sl[k:k + 1, :] * c_old
        c_o_ref[k] = dec * c_old + k_sl[k:k + 1, :] * v_t

    @pl.when(kh == 0)
    def _():
        num_scr[...] = acc

    @pl.when(kh > 0)
    def _():
        num_scr[...] += acc

    @pl.when(kh == K_SPLIT - 1)
    def _():
        q_t = head_rows(MQ_OFF, HP, MLSTM_DH) * (MLSTM_DH ** -0.5)
        k_t = head_rows(MK_OFF, HP, MLSTM_DH)
        n_old = n_ref[...]
        sc = jnp.sum(q_t * k_t, axis=0, keepdims=True) * wig
        num = dec * num_scr[...] + sc * v_t
        nq = dec * jnp.sum(q_t * n_old, axis=0, keepdims=True) + sc
        hval = num / jnp.maximum(jnp.abs(nq), floor)
        n_o_ref[...] = dec * n_old + wig * k_t
        mu = jnp.mean(hval, axis=0, keepdims=True)
        dlt = hval - mu
        var = jnp.mean(dlt * dlt, axis=0, keepdims=True)
        w_norm = mnorm_ref[pl.ds(pl.multiple_of(h * HP, SUBLANES), MLSTM_DH), :]
        mo = head_rows(MO_OFF, HP, MLSTM_DH)
        mixt_scr[pl.ds(pl.multiple_of(YM_OFF + h * HP, SUBLANES), MLSTM_DH), :] = (
            jax.nn.sigmoid(mo) * dlt * lax.rsqrt(var + LN_EPS) * w_norm)

    s0 = kh * SK
    ea_sl = ea_scr[pl.ds(pl.multiple_of(h * KP + s0, SUBLANES), SK), :]
    gq_sl = head_rows(GQ_OFF, KP, SK, s0) * (GLA_DK ** -0.5)
    gk_sl = head_rows(GK_OFF, KP, SK, s0)
    gv_t = head_rows(GV_OFF, HP, GLA_DV)
    gqe_sl = gq_sl * ea_sl
    oacc = jnp.zeros((GLA_DV, B), F32)
    for k in range(SK):
        s_old = s_ref[k]
        oacc = oacc + gqe_sl[k:k + 1, :] * s_old
        s_o_ref[k] = ea_sl[k:k + 1, :] * s_old + gk_sl[k:k + 1, :] * gv_t

    @pl.when(kh == 0)
    def _():
        oin_scr[...] = oacc

    @pl.when(kh > 0)
    def _():
        oin_scr[...] += oacc

    @pl.when(kh == K_SPLIT - 1)
    def _():
        gq_t = head_rows(GQ_OFF, KP, GLA_DK) * (GLA_DK ** -0.5)
        gk_t = head_rows(GK_OFF, KP, GLA_DK)
        qk = jnp.sum(gq_t * gk_t, axis=0, keepdims=True)
        o = oin_scr[...] + qk * gv_t
        ms = jnp.mean(o * o, axis=0, keepdims=True)
        w_norm = gnorm_ref[pl.ds(pl.multiple_of(h * HP, SUBLANES), GLA_DV), :]
        gg = head_rows(GG_OFF, HP, GLA_DV)
        mixt_scr[pl.ds(pl.multiple_of(YG_OFF + h * HP, SUBLANES), GLA_DV), :] = (
            gg * jax.nn.sigmoid(gg) * o * lax.rsqrt(ms + LN_EPS) * w_norm)

    @pl.when(last)
    def _():
        u = jnp.concatenate([jnp.transpose(zt_scr[c0:c0 + LANES, :]) for c0 in range(0, D_POOL, LANES)], axis=1)
        run = u
        sums = {}
        for j in range(1, max(POOL_WINDOWS)):
            run = run + pool_ref[POOL_BUF - j]
            if j + 1 in POOL_WINDOWS:
                sums[j + 1] = run
        grp = lax.broadcasted_iota(jnp.int32, (B, D_POOL), 1) // POOL_GROUP
        wmean = jnp.zeros((B, D_POOL), F32)
        for gidx, w in enumerate(POOL_WINDOWS):
            wmean = jnp.where(grp == gidx, sums[w] / float(min(w, PAST_LEN + 1)), wmean)
        y_pool = _dot((wmean - u).astype(BF16), pmix_ref[...]) * pscale_ref[...]
        for j in range(POOL_BUF - 1):
            pool_o_ref[j] = pool_ref[j + 1]
        pool_o_ref[POOL_BUF - 1] = u
        mix = jnp.concatenate(
            [y_pool] + [jnp.transpose(mixt_scr[c0:c0 + LANES, :]) for c0 in range(YM_OFF, N_MIX, LANES)], axis=1)
        y = ALPHA * x_ref[...] + _dot(mix.astype(BF16), w_out_ref[...])
        xo_ref[...] = _layer_norm_rows(y, lng_ref[...], lnb_ref[...])


def _sample_step(x, states, consts, carried, layer):
    B = x.shape[0]
    out_shape = (jax.ShapeDtypeStruct((B, D_MODEL), F32),) + tuple(
        jax.ShapeDtypeStruct(s.shape, F32) for s in states)
    state_specs = [
        pl.BlockSpec((None, POOL_BUF, B, D_POOL), lambda h, k: (layer, 0, 0, 0)),
        pl.BlockSpec((None, None, CK, MLSTM_DH, B), lambda h, k: (layer, h, k, 0, 0)),
        pl.BlockSpec((None, None, MLSTM_DH, B), lambda h, k: (layer, h, 0, 0)),
        pl.BlockSpec((None, HEADS, B), lambda h, k: (layer, 0, 0)),
        pl.BlockSpec((None, None, SK, GLA_DV, B), lambda h, k: (layer, h, k, 0, 0)),
    ]
    x_spec = pl.BlockSpec((B, D_MODEL), lambda h, k: (0, 0))
    n_in = 1 + len(states) + len(consts)
    return pl.pallas_call(
        functools.partial(_sample_step_kernel, len(carried)),
        grid=(HEADS, K_SPLIT),
        in_specs=[x_spec] + state_specs + [_layer_spec(cst.shape, layer) for cst in consts]
        + [pl.BlockSpec(memory_space=pl.ANY)] * len(carried),
        out_specs=[x_spec] + state_specs,
        out_shape=out_shape,
        input_output_aliases={n_in + i: 1 + i for i in range(len(carried))},
        scratch_shapes=[
            pltpu.VMEM((N_Z, B), F32),
            pltpu.VMEM((N_MIX, B), F32),
            pltpu.VMEM((2 * SUBLANES, B), F32),
            pltpu.VMEM((HEADS * KP, B), F32),
            pltpu.VMEM((MLSTM_DH, B), F32),
            pltpu.VMEM((GLA_DV, B), F32),
        ],
        compiler_params=pltpu.CompilerParams(
            dimension_semantics=("arbitrary", "arbitrary"), vmem_limit_bytes=VMEM_LIMIT),
        name="sample_step",
    )(x, *states, *consts, *carried)


def _ffn_kernel(x_ref, p_ref, wg_ref, wu_ref, wd_ref, wp_ref, wpg_ref, lng_ref, lnb_ref, o_ref):
    x = x_ref[...]
    xb = x.astype(BF16)
    acc = ALPHA * x + _dot(p_ref[...].astype(BF16), wp_ref[...]) * jax.nn.sigmoid(_dot(xb, wpg_ref[...]))
    for c0 in range(0, D_FF, FF_CHUNK):
        c1 = min(c0 + FF_CHUNK, D_FF)
        gate = _dot(xb, wg_ref[:, c0:c1])
        up = _dot(xb, wu_ref[:, c0:c1])
        hid = (gate * jax.nn.sigmoid(gate) * up).astype(BF16)
        acc = acc + _dot(hid, wd_ref[c0:c1, :])
    o_ref[...] = _layer_norm_rows(acc, lng_ref[...], lnb_ref[...])


def _ffn(x, p, consts, layer, tm):
    M = x.shape[0]
    return pl.pallas_call(
        _ffn_kernel,
        grid=(M // tm,),
        in_specs=[pl.BlockSpec((tm, D_MODEL), lambda i: (i, 0)),
                  pl.BlockSpec((None, tm, D_PLE), lambda i: (layer, i, 0))]
        + [_layer_spec(c.shape, layer) for c in consts],
        out_specs=pl.BlockSpec((tm, D_MODEL), lambda i: (i, 0)),
        out_shape=jax.ShapeDtypeStruct((M, D_MODEL), F32),
        compiler_params=pltpu.CompilerParams(
            dimension_semantics=("arbitrary",), vmem_limit_bytes=VMEM_LIMIT),
        name="ffn",
    )(x, p, *consts)


def kernel(x_prompt, x_sample, p_prompt, p_sample, state_pool, state_mlstm_C, state_mlstm_n, state_mlstm_m, state_gla_S, w_in, mlstm_i_bias, mlstm_f_bias, mlstm_norm_w, gla_a2, gla_a_bias, gla_norm_w, pool_mix, pool_scale, w_out, ln1_g, ln1_b, w_gate, w_up, w_down, w_ple, w_ple_gate, ln2_g, ln2_b):
    Bp, Tp, _ = x_prompt.shape
    Bs = x_sample.shape[0]

    w_in_p = _pad_in_proj_t(w_in).astype(BF16)
    w_out_p = _pad_out_proj(w_out).astype(BF16)
    gbias = jnp.zeros((DEPTH, 1, LANES), F32)
    gbias = gbias.at[:, 0, GATE_I:GATE_I + HEADS].set(mlstm_i_bias.astype(F32))
    gbias = gbias.at[:, 0, GATE_F:GATE_F + HEADS].set(mlstm_f_bias.astype(F32))
    a2_p = jnp.zeros((DEPTH, LANES, HEADS * KP), F32)
    a2_p = a2_p.at[:, GATE_A:GATE_A + GLA_RANK, :].set(_pad_heads(gla_a2.astype(F32), GLA_DK, KP)).astype(BF16)
    abias_p = _pad_heads(gla_a_bias.astype(F32), GLA_DK, KP)[:, None, :]
    mnorm_p = _pad_heads(mlstm_norm_w.astype(F32), MLSTM_DH, HP)[:, None, :]
    gnorm_p = _pad_heads(gla_norm_w.astype(F32), GLA_DV, HP)[:, None, :]
    n_grp = len(POOL_WINDOWS)
    eye = jnp.eye(n_grp, dtype=F32)
    pmix_bd = (pool_mix.astype(F32)[:, :, :, None, :] * eye[None, :, None, :, None]).reshape(
        DEPTH, D_POOL, D_POOL).astype(BF16)
    pscale = pool_scale.astype(F32)[:, None, :]
    row = lambda a: a.astype(F32)[:, None, :]
    ln1_g_r, ln1_b_r, ln2_g_r, ln2_b_r = row(ln1_g), row(ln1_b), row(ln2_g), row(ln2_b)
    wg_b, wu_b, wd_b = w_gate.astype(BF16), w_up.astype(BF16), w_down.astype(BF16)
    wp_b, wpg_b = w_ple.astype(BF16), w_ple_gate.astype(BF16)

    mixer_consts = (w_in_p, w_out_p, gbias, a2_p, abias_p, mnorm_p, gnorm_p, pmix_bd, pscale, ln1_g_r, ln1_b_r)
    ffn_consts = (wg_b, wu_b, wd_b, wp_b, wpg_b, ln2_g_r, ln2_b_r)
    xp = x_prompt
    xs = x_sample.reshape(Bs, D_MODEL)
    pp = p_prompt.reshape(DEPTH, Bp * Tp, D_PLE)
    ps = p_sample.reshape(DEPTH, Bs, D_PLE)
    sample_states = (jnp.transpose(state_pool, (0, 2, 1, 3)), jnp.transpose(state_mlstm_C, (0, 2, 3, 4, 1)),
                     jnp.transpose(state_mlstm_n, (0, 2, 3, 1)), jnp.transpose(state_mlstm_m, (0, 2, 1)),
                     jnp.transpose(state_gla_S, (0, 2, 3, 4, 1)))
    col = lambda a: jnp.transpose(a, (0, 2, 1))
    sample_consts = (w_in_p, w_out_p, col(gbias), col(a2_p), col(abias_p), col(mnorm_p), col(gnorm_p),
                     pmix_bd, pscale, ln1_g_r, ln1_b_r)
    outs_p = []
    carried = ()
    for l in range(DEPTH):
        xp, pool_p, c_p, n_p, m_p, s_p = _prompt_mixer(xp, mixer_consts, l)
        xp = _ffn(xp.reshape(Bp * Tp, D_MODEL), pp, ffn_consts, l, tm=512).reshape(Bp, Tp, D_MODEL)
        outs_p.append((pool_p, c_p, n_p[:, :, :MLSTM_DH], m_p[:, 0, GATE_F:GATE_F + HEADS], s_p))
        xs, *carried = _sample_step(xs, sample_states, sample_consts, tuple(carried), l)
        xs = _ffn(xs, ps, ffn_consts, l, tm=Bs)

    pool_s, c_s, n_s, m_s, s_s = carried
    stack = lambda j: jnp.stack([o[j] for o in outs_p], axis=0)
    return ((xp, xs.reshape(Bs, 1, D_MODEL)) + tuple(stack(j) for j in range(5))
            + (jnp.transpose(pool_s, (0, 2, 1, 3)), jnp.transpose(c_s, (0, 4, 1, 2, 3)),
               jnp.transpose(n_s, (0, 3, 1, 2)), jnp.transpose(m_s, (0, 2, 1)),
               jnp.transpose(s_s, (0, 4, 1, 2, 3))))
```

```python
import functools

import numpy as np
import jax
import jax.numpy as jnp
from jax import lax
from jax.experimental import pallas as pl
from jax.experimental.pallas import tpu as pltpu

F32 = jnp.float32
BF16 = jnp.bfloat16

D_MODEL = 1024
DEPTH = 4
PAST_LEN = 16384
D_POOL = 256
POOL_WINDOWS = (2, 4, 8, 16)
POOL_GROUP = 64
POOL_BUF = 15
D_MLSTM = 384
HEADS = 4
MLSTM_DH = 96
D_GLA = 384
GLA_DV = 96
GLA_DK = 48
GLA_RANK = 16
GLA_TAU = 16.0
D_MIX = 1024
D_FF = 2816
D_PLE = 256
CHUNK = 64
ALPHA = (2 * DEPTH) ** 0.25
LN_EPS = 1e-5

LANES = 128
SUBLANES = 8
VMEM_LIMIT = 56 * 1024 * 1024

HP = LANES
KP = 64
U_OFF = 0
MQ_OFF = U_OFF + D_POOL
MK_OFF = MQ_OFF + HEADS * HP
MV_OFF = MK_OFF + HEADS * HP
MO_OFF = MV_OFF + HEADS * HP
GQ_OFF = MO_OFF + HEADS * HP
GK_OFF = GQ_OFF + HEADS * KP
GV_OFF = GK_OFF + HEADS * KP
GG_OFF = GV_OFF + HEADS * HP
GATE_OFF = GG_OFF + HEADS * HP
N_Z = GATE_OFF + LANES
GATE_I = 0
GATE_F = 4
GATE_A = 8
YM_OFF = D_POOL
YG_OFF = YM_OFF + HEADS * HP
N_MIX = YG_OFF + HEADS * HP

TT = 512
CHUNK_UNROLL = 2
SG = SUBLANES
FF_CHUNK = 512


def _pad_heads(a, width, padded):
    lead = a.shape[:-1]
    a = a.reshape(lead + (HEADS, width))
    a = jnp.pad(a, [(0, 0)] * len(lead) + [(0, 0), (0, padded - width)])
    return a.reshape(lead + (HEADS * padded,))


def _pad_head_rows(a, width, padded):
    L, _, K = a.shape
    a = jnp.pad(a.reshape(L, HEADS, width, K), ((0, 0), (0, 0), (0, padded - width), (0, 0)))
    return a.reshape(L, HEADS * padded, K)


def _pad_in_proj_t(w_in):
    sizes = (D_POOL, D_MLSTM, D_MLSTM, D_MLSTM, HEADS, HEADS, D_MLSTM,
             HEADS * GLA_DK, HEADS * GLA_DK, D_GLA, GLA_RANK, D_GLA)
    w_t = jnp.transpose(w_in, (0, 2, 1))
    u, mq, mk, mv, mi, mf, mo, gq, gk, gv, ga, gg = jnp.split(w_t, np.cumsum(sizes)[:-1].tolist(), axis=1)
    gate_pad = jnp.zeros((w_t.shape[0], LANES - GATE_A - GLA_RANK, w_t.shape[2]), w_t.dtype)
    wide = lambda a: _pad_head_rows(a, MLSTM_DH, HP)
    narrow = lambda a: _pad_head_rows(a, GLA_DK, KP)
    return jnp.concatenate([u, wide(mq), wide(mk), wide(mv), wide(mo), narrow(gq), narrow(gk), wide(gv),
                            wide(gg), mi, mf, ga, gate_pad], axis=1)


def _pad_out_proj(w_out):
    pool = w_out[:, :D_POOL]
    heads = w_out[:, D_POOL:].reshape(w_out.shape[0], 2 * HEADS, MLSTM_DH, D_MODEL)
    heads = jnp.pad(heads, ((0, 0), (0, 0), (0, HP - MLSTM_DH), (0, 0)))
    return jnp.concatenate([pool, heads.reshape(w_out.shape[0], 2 * HEADS * HP, D_MODEL)], axis=1)


def _dot(a, b):
    return jnp.dot(a, b, preferred_element_type=F32)


def _dot_nt(a, b):
    return lax.dot_general(a, b, (((1,), (1,)), ((), ())), preferred_element_type=F32)


def _dot_tn(a, b):
    return lax.dot_general(a, b, (((0,), (0,)), ((), ())), preferred_element_type=F32)


def _log_sigmoid(x):
    return jnp.minimum(x, 0.0) - jnp.log1p(jnp.exp(-jnp.abs(x)))


def _cummax_rows(x):
    n = x.shape[0]
    rid = lax.broadcasted_iota(jnp.int32, x.shape, 0)
    sh = 1
    while sh < n:
        x = jnp.maximum(x, jnp.where(rid >= sh, pltpu.roll(x, sh, axis=0), -jnp.inf))
        sh *= 2
    return x


def _cumsum_rows(x):
    n = x.shape[0]
    rid = lax.broadcasted_iota(jnp.int32, x.shape, 0)
    sh = 1
    while sh < n:
        x = x + jnp.where(rid >= sh, pltpu.roll(x, sh, axis=0), 0.0)
        sh *= 2
    return x


def _layer_norm_rows(y, g, b):
    mu = jnp.mean(y, axis=-1, keepdims=True)
    d = y - mu
    var = jnp.mean(d * d, axis=-1, keepdims=True)
    return d * lax.rsqrt(var + LN_EPS) * g + b


def _pool_select(lane, a, b):
    return jnp.where((lane & (LANES - 1)) < POOL_GROUP, a, b)


def _prompt_mixer_kernel(x_ref, w_in_ref, w_out_ref, gbias_ref, a2_ref, abias_ref, mnorm_ref,
                         gnorm_ref, pmix_ref, pscale_ref, lng_ref, lnb_ref,
                         xo_ref, pool_o_ref, c_o_ref, n_o_ref, m_o_ref, s_o_ref,
                         z_scr, mix_scr, ext_scr, c_scr, m_scr, s_scr):
    ti = pl.program_id(1)
    n_t = pl.num_programs(1)

    @pl.when(ti == 0)
    def _():
        ext_scr[0:2 * SUBLANES, :] = jnp.zeros((2 * SUBLANES, D_POOL), F32)
        c_scr[...] = jnp.zeros(c_scr.shape, F32)
        m_scr[...] = jnp.zeros(m_scr.shape, F32)
        s_scr[...] = jnp.zeros(s_scr.shape, F32)

    xb = x_ref[0].astype(BF16)
    for c0 in range(0, N_Z, 512):
        c1 = min(c0 + 512, N_Z)
        z_scr[:, c0:c1] = _dot_nt(xb, w_in_ref[c0:c1, :])

    hist = 2 * SUBLANES
    u = z_scr[:, U_OFF:U_OFF + D_POOL]
    ext_scr[hist:hist + TT, :] = u
    ext = ext_scr[...]
    e0 = ext[:, 0:LANES]
    e1 = ext[:, LANES:2 * LANES]
    a2s = e0 + pltpu.roll(e0, 1, axis=0)
    a4s = a2s + pltpu.roll(a2s, 2, axis=0)
    b2s = e1 + pltpu.roll(e1, 1, axis=0)
    b4s = b2s + pltpu.roll(b2s, 2, axis=0)
    b8s = b4s + pltpu.roll(b4s, 4, axis=0)
    b16s = b8s + pltpu.roll(b8s, 8, axis=0)
    lane = lax.broadcasted_iota(jnp.int32, (TT, LANES), 1)
    pos1 = (lax.broadcasted_iota(jnp.int32, (TT, LANES), 0) + ti * TT + 1).astype(F32)
    ws0 = _pool_select(lane, a2s[hist:], a4s[hist:])
    ws1 = _pool_select(lane, b8s[hist:], b16s[hist:])
    cnt0 = jnp.minimum(_pool_select(lane, 2.0, 4.0), pos1)
    cnt1 = jnp.minimum(_pool_select(lane, 8.0, 16.0), pos1)
    d0 = ws0 / cnt0 - u[:, 0:LANES]
    d1 = ws1 / cnt1 - u[:, LANES:]
    dpool = jnp.concatenate([d0, d1], axis=1).astype(BF16)
    y_pool = _dot(dpool, pmix_ref[...]) * pscale_ref[...]
    mix_scr[:, 0:D_POOL] = y_pool.astype(BF16)
    tail = ext_scr[TT:TT + hist, :]
    ext_scr[0:hist, :] = tail

    @pl.when(ti == n_t - 1)
    def _():
        pool_o_ref[0] = ext_scr[1:hist, :]

    gbias = gbias_ref[...]
    abias = abias_ref[...]
    lane_g = lax.broadcasted_iota(jnp.int32, (CHUNK, LANES), 1)
    tcol = lax.broadcasted_iota(jnp.int32, (CHUNK, CHUNK), 0)
    srow = lax.broadcasted_iota(jnp.int32, (CHUNK, CHUNK), 1)
    causal = srow <= tcol

    def chunk_body(c, carry):
        r0 = pl.multiple_of(c * CHUNK, CHUNK)
        rows = pl.ds(r0, CHUNK)
        graw = z_scr[rows, GATE_OFF:GATE_OFF + LANES]

        ya = _dot(graw.astype(BF16), a2_ref[...]) + abias
        la = _log_sigmoid(ya) * (1.0 / GLA_TAU)
        bc = _cumsum_rows(la)
        gq = z_scr[rows, GQ_OFF:GQ_OFF + HEADS * KP] * (GLA_DK ** -0.5)
        gk = z_scr[rows, GK_OFF:GK_OFF + HEADS * KP]
        width = HEADS * KP
        t_id = lax.broadcasted_iota(jnp.int32, (CHUNK, width), 0)
        lane_head = lax.broadcasted_iota(jnp.int32, (CHUNK, width), 1) // KP

        def stack_heads(a):
            return jnp.concatenate([jnp.where(lane_head == h, a, 0.0) for h in range(HEADS)],
                                   axis=0).astype(BF16)

        trow = lax.broadcasted_iota(jnp.int32, (HEADS * CHUNK, CHUNK), 0) & (CHUNK - 1)
        scol = lax.broadcasted_iota(jnp.int32, (HEADS * CHUNK, CHUNK), 1)
        amat = jnp.where(trow == scol, _dot_nt(stack_heads(gq), gk.astype(BF16)), 0.0)
        half = CHUNK // 2
        while half >= 1:
            blk = 2 * half
            upper = (t_id & half) != 0
            if blk >= SUBLANES:
                ref_rows = jnp.concatenate(
                    [jnp.broadcast_to(bc[b0 + half - 1:b0 + half, :], (blk, width))
                     for b0 in range(0, CHUNK, blk)], axis=0)
            else:
                off = t_id & (blk - 1)
                ref_rows = bc
                for o in range(blk):
                    sh = (o - (half - 1)) % CHUNK
                    if sh != 0:
                        ref_rows = jnp.where(off == o, pltpu.roll(bc, sh, axis=0), ref_rows)
            dlt = bc - ref_rows
            qe = jnp.where(upper, gq * jnp.exp(jnp.minimum(dlt, 0.0)), 0.0)
            ke = jnp.where(upper, 0.0, gk * jnp.exp(jnp.minimum(-dlt, 0.0)))
            p = _dot_nt(stack_heads(qe), ke.astype(BF16))
            shift = blk.bit_length() - 1
            amat = amat + jnp.where((trow >> shift) == (scol >> shift), p, 0.0)
            half //= 2

        o_inter = _dot(stack_heads(gq * jnp.exp(bc)), s_scr[...].astype(BF16))
        bc_t = jnp.transpose(bc)
        gk_t = jnp.transpose(gk)
        last_t = bc_t[:, CHUNK - 1:CHUNK]
        ke_t = (gk_t * jnp.exp(last_t - bc_t)).astype(BF16)
        s_dec = jnp.exp(last_t)
        for h in range(HEADS):
            hr = slice(h * CHUNK, (h + 1) * CHUNK)
            gv = z_scr[rows, GV_OFF + h * HP:GV_OFF + (h + 1) * HP].astype(BF16)
            o = o_inter[hr] + _dot(amat[hr].astype(BF16), gv)
            ms = jnp.sum(o * o, axis=1, keepdims=True) * (1.0 / GLA_DV)
            on = o * lax.rsqrt(ms + LN_EPS) * gnorm_ref[:, h * HP:(h + 1) * HP]
            gg = z_scr[rows, GG_OFF + h * HP:GG_OFF + (h + 1) * HP]
            mix_scr[rows, YG_OFF + h * HP:YG_OFF + (h + 1) * HP] = (gg * jax.nn.sigmoid(gg) * on).astype(BF16)
            kr = slice(h * KP, (h + 1) * KP)
            s_scr[kr, :] = s_dec[kr] * s_scr[kr, :] + _dot(ke_t[kr], gv)

        g = graw + gbias
        b_all = _cumsum_rows(_log_sigmoid(g))
        i_sh = pltpu.roll(g, GATE_F - GATE_I, axis=1)
        r_all = i_sh - b_all
        m_row = m_scr[0:1, :]
        inter_all = b_all + m_row
        m_t_all = jnp.maximum(inter_all, b_all + _cummax_rows(r_all))
        a_all = b_all - m_t_all
        w_inter_all = jnp.exp(inter_all - m_t_all)
        floor_all = jnp.exp(-m_t_all)
        b_last = b_all[CHUNK - 1:CHUNK, :]
        m_new = m_t_all[CHUNK - 1:CHUNK, :]
        m_scr[0:1, :] = m_new
        decay_all = jnp.exp(b_last + m_row - m_new)
        w_end_all = jnp.exp(b_last - b_all + i_sh - m_new)
        r_t = jnp.transpose(r_all)
        for h in range(HEADS):
            gl = GATE_F + h
            e = jnp.exp(jnp.where(causal, a_all[:, gl:gl + 1] + r_t[gl:gl + 1, :], -jnp.inf))
            q = (z_scr[rows, MQ_OFF + h * HP:MQ_OFF + (h + 1) * HP] * (MLSTM_DH ** -0.5)).astype(BF16)
            kf = z_scr[rows, MK_OFF + h * HP:MK_OFF + (h + 1) * HP]
            v = z_scr[rows, MV_OFF + h * HP:MV_OFF + (h + 1) * HP]
            v_aug = jnp.concatenate([jnp.where(lane_g == MLSTM_DH, 1.0, v), jnp.ones((CHUNK, LANES), F32)],
                                    axis=1).astype(BF16)
            s = _dot_nt(q, kf.astype(BF16)) * e
            c_aug = c_scr[h]
            num2 = w_inter_all[:, gl:gl + 1] * _dot(q, c_aug.astype(BF16)) + _dot(s.astype(BF16), v_aug)
            num = num2[:, 0:LANES]
            hval = num / jnp.maximum(jnp.abs(num2[:, LANES:]), floor_all[:, gl:gl + 1])
            valid = lane_g < MLSTM_DH
            mu = jnp.sum(jnp.where(valid, hval, 0.0), axis=1, keepdims=True) * (1.0 / MLSTM_DH)
            dlt = jnp.where(valid, hval - mu, 0.0)
            var = jnp.sum(dlt * dlt, axis=1, keepdims=True) * (1.0 / MLSTM_DH)
            hn = dlt * lax.rsqrt(var + LN_EPS) * mnorm_ref[:, h * HP:(h + 1) * HP]
            mo = z_scr[rows, MO_OFF + h * HP:MO_OFF + (h + 1) * HP]
            mix_scr[rows, YM_OFF + h * HP:YM_OFF + (h + 1) * HP] = (jax.nn.sigmoid(mo) * hn).astype(BF16)
            kw = (kf * w_end_all[:, gl:gl + 1]).astype(BF16)
            c_scr[h] = decay_all[:, gl:gl + 1] * c_aug + _dot_tn(kw, v_aug)
        return carry

    lax.fori_loop(0, TT // CHUNK, chunk_body, 0, unroll=CHUNK_UNROLL)

    y = ALPHA * x_ref[0] + _dot(mix_scr[...], w_out_ref[...])
    xo_ref[0] = _layer_norm_rows(y, lng_ref[...], lnb_ref[...])

    @pl.when(ti == n_t - 1)
    def _():
        for h in range(HEADS):
            c_aug = c_scr[h, :, 0:HP]
            c_o_ref[0, h] = c_aug[0:MLSTM_DH, 0:MLSTM_DH]
            n_o_ref[0, h:h + 1, :] = jnp.transpose(c_aug)[MLSTM_DH:MLSTM_DH + 1, :]
            s_o_ref[0, h] = s_scr[h * KP:h * KP + GLA_DK, 0:GLA_DV]
        m_o_ref[0] = m_scr[...]


def _layer_spec(shape, layer):
    nd = len(shape)
    return pl.BlockSpec((None,) + tuple(shape[1:]), lambda *_: (layer,) + (0,) * (nd - 1),
                        pipeline_mode=pl.Buffered(1))


def _prompt_mixer(x, consts, layer):
    B, T, _ = x.shape
    out_shape = (
        jax.ShapeDtypeStruct((B, T, D_MODEL), F32),
        jax.ShapeDtypeStruct((B, POOL_BUF, D_POOL), F32),
        jax.ShapeDtypeStruct((B, HEADS, MLSTM_DH, MLSTM_DH), F32),
        jax.ShapeDtypeStruct((B, HEADS, HP), F32),
        jax.ShapeDtypeStruct((B, SUBLANES, LANES), F32),
        jax.ShapeDtypeStruct((B, HEADS, GLA_DK, GLA_DV), F32),
    )
    out_specs = (
        pl.BlockSpec((1, TT, D_MODEL), lambda b, t: (b, t, 0)),
        pl.BlockSpec((1, POOL_BUF, D_POOL), lambda b, t: (b, 0, 0)),
        pl.BlockSpec((1, HEADS, MLSTM_DH, MLSTM_DH), lambda b, t: (b, 0, 0, 0)),
        pl.BlockSpec((1, HEADS, HP), lambda b, t: (b, 0, 0)),
        pl.BlockSpec((1, SUBLANES, LANES), lambda b, t: (b, 0, 0)),
        pl.BlockSpec((1, HEADS, GLA_DK, GLA_DV), lambda b, t: (b, 0, 0, 0)),
    )
    return pl.pallas_call(
        _prompt_mixer_kernel,
        grid=(B, T // TT),
        in_specs=[pl.BlockSpec((1, TT, D_MODEL), lambda b, t: (b, t, 0))]
        + [_layer_spec(c.shape, layer) for c in consts],
        out_specs=out_specs,
        out_shape=out_shape,
        scratch_shapes=[
            pltpu.VMEM((TT, N_Z), F32),
            pltpu.VMEM((TT, N_MIX), BF16),
            pltpu.VMEM((TT + 2 * SUBLANES, D_POOL), F32),
            pltpu.VMEM((HEADS, HP, 2 * HP), F32),
            pltpu.VMEM((SUBLANES, LANES), F32),
            pltpu.VMEM((HEADS * KP, HP), F32),
        ],
        compiler_params=pltpu.CompilerParams(
            dimension_semantics=("arbitrary", "arbitrary"), vmem_limit_bytes=VMEM_LIMIT),
        name="prompt_mixer",
    )(x, *consts)


def _column_bcast(row):
    return jnp.transpose(jnp.broadcast_to(row, (LANES, LANES)))


def _sample_mixer_kernel(n_carried, *refs):
    (x_ref, pool_ref, c_ref, n_ref, m_ref, s_ref,
     w_in_ref, w_out_ref, gbias_ref, a2_ref, abias_ref, mnorm_ref,
     gnorm_ref, pmix_ref, pscale_ref, lng_ref, lnb_ref) = refs[:17]
    (xo_ref, pool_o_ref, c_o_ref, n_o_ref, m_o_ref, s_o_ref,
     z_scr, mix_scr, qc_scr, og_scr, psum_scr) = refs[17 + n_carried:]
    gi = pl.program_id(0)
    n_g = pl.num_programs(0)

    @pl.when(gi == 0)
    def _():
        mix_scr[...] = jnp.zeros(mix_scr.shape, F32)
        xb = x_ref[...].astype(BF16)
        for c0 in range(0, N_Z, 512):
            c1 = min(c0 + 512, N_Z)
            z_scr[:, c0:c1] = _dot(xb, w_in_ref[:, c0:c1])

    r0 = pl.multiple_of(gi * SG, SG)
    rows = pl.ds(r0, SG)

    u = z_scr[rows, U_OFF:U_OFF + D_POOL]

    def window_of(lane):
        grp = lane // POOL_GROUP
        w = jnp.full(lane.shape, POOL_WINDOWS[0], jnp.int32)
        for gidx in range(1, len(POOL_WINDOWS)):
            w = jnp.where(grp == gidx, POOL_WINDOWS[gidx], w)
        return w

    hist_row = lax.broadcasted_iota(jnp.int32, (POOL_BUF, D_POOL), 0)
    in_window = hist_row > POOL_BUF - window_of(lax.broadcasted_iota(jnp.int32, (POOL_BUF, D_POOL), 1))
    for j in range(SG):
        psum_scr[j:j + 1, :] = jnp.sum(jnp.where(in_window, pool_ref[j], 0.0), axis=0, keepdims=True)
        pool_o_ref[j, 0:POOL_BUF - 1, :] = pool_ref[j, 1:POOL_BUF, :]
        pool_o_ref[j, POOL_BUF - 1:POOL_BUF, :] = u[j:j + 1, :]
    cnt = jnp.minimum(window_of(lax.broadcasted_iota(jnp.int32, (SG, D_POOL), 1)), PAST_LEN + 1).astype(F32)
    dpool = ((psum_scr[...] + u) / cnt - u).astype(BF16)
    mix_scr[rows, 0:D_POOL] = _dot(dpool, pmix_ref[...]) * pscale_ref[...]

    graw = z_scr[rows, GATE_OFF:GATE_OFF + LANES]
    g = graw + gbias_ref[...]

    log_i = g[:, GATE_I:GATE_I + HEADS]
    log_f = _log_sigmoid(g[:, GATE_F:GATE_F + HEADS])
    inter = log_f + m_ref[...]
    m_t = jnp.maximum(inter, log_i)
    w_inter = jnp.exp(inter - m_t)
    w_in_gate = jnp.exp(log_i - m_t)
    m_o_ref[...] = m_t
    for h in range(HEADS):
        q = z_scr[rows, MQ_OFF + h * HP:MQ_OFF + (h + 1) * HP] * (MLSTM_DH ** -0.5)
        k = z_scr[rows, MK_OFF + h * HP:MK_OFF + (h + 1) * HP]
        v = z_scr[rows, MV_OFF + h * HP:MV_OFF + (h + 1) * HP]
        n_old = n_ref[:, h * MLSTM_DH:(h + 1) * MLSTM_DH]
        dec = w_inter[:, h:h + 1]
        wig = w_in_gate[:, h:h + 1]
        for j in range(SG):
            c_old = c_ref[j, h]
            qcol = _column_bcast(q[j:j + 1, :])[0:MLSTM_DH, 0:MLSTM_DH]
            kcol = _column_bcast(k[j:j + 1, :])[0:MLSTM_DH, 0:MLSTM_DH]
            qc_scr[j:j + 1, h * HP:h * HP + MLSTM_DH] = jnp.sum(qcol * c_old, axis=0, keepdims=True)
            c_o_ref[j, h] = dec[j:j + 1, :] * c_old + kcol * (wig[j:j + 1, :] * v[j:j + 1, 0:MLSTM_DH])
        q96 = q[:, 0:MLSTM_DH]
        k96 = k[:, 0:MLSTM_DH]
        v96 = v[:, 0:MLSTM_DH]
        sc = jnp.sum(q96 * k96, axis=1, keepdims=True) * wig
        num = dec * qc_scr[:, h * HP:h * HP + MLSTM_DH] + sc * v96
        nq = dec * jnp.sum(q96 * n_old, axis=1, keepdims=True) + sc
        hval = num / jnp.maximum(jnp.abs(nq), jnp.exp(-m_t[:, h:h + 1]))
        n_o_ref[:, h * MLSTM_DH:(h + 1) * MLSTM_DH] = dec * n_old + wig * k96
        mu = jnp.mean(hval, axis=1, keepdims=True)
        dlt = hval - mu
        var = jnp.mean(dlt * dlt, axis=1, keepdims=True)
        hn = dlt * lax.rsqrt(var + LN_EPS) * mnorm_ref[:, h * HP:h * HP + MLSTM_DH]
        mo = z_scr[rows, MO_OFF + h * HP:MO_OFF + h * HP + MLSTM_DH]
        mix_scr[rows, YM_OFF + h * HP:YM_OFF + h * HP + MLSTM_DH] = jax.nn.sigmoid(mo) * hn

    ya = _dot(graw.astype(BF16), a2_ref[...]) + abias_ref[...]
    ea = jnp.exp(_log_sigmoid(ya) * (1.0 / GLA_TAU))
    gq = z_scr[rows, GQ_OFF:GQ_OFF + HEADS * KP] * (GLA_DK ** -0.5)
    gk = z_scr[rows, GK_OFF:GK_OFF + HEADS * KP]
    gqe = gq * ea
    lane_head = lax.broadcasted_iota(jnp.int32, (SG, HEADS * KP), 1) // KP
    qk_prod = gq * gk
    gvs = [z_scr[rows, GV_OFF + h * HP:GV_OFF + h * HP + GLA_DV] for h in range(HEADS)]
    for pair in range(HEADS // 2):
        pl_ = slice(pair * LANES, (pair + 1) * LANES)
        for j in range(SG):
            qecol = _column_bcast(gqe[j:j + 1, pl_])
            kcol = _column_bcast(gk[j:j + 1, pl_])
            eacol = _column_bcast(ea[j:j + 1, pl_])
            for hh in range(2):
                h = 2 * pair + hh
                kr = slice(hh * KP, hh * KP + GLA_DK)
                s_old = s_ref[j, h]
                vrow = gvs[h][j:j + 1, :]
                og_scr[j:j + 1, h * HP:h * HP + GLA_DV] = jnp.sum(
                    qecol[kr, 0:GLA_DV] * s_old, axis=0, keepdims=True)
                s_o_ref[j, h] = eacol[kr, 0:GLA_DV] * s_old + kcol[kr, 0:GLA_DV] * vrow
    for h in range(HEADS):
        gv = gvs[h]
        qk = jnp.sum(jnp.where(lane_head == h, qk_prod, 0.0), axis=1, keepdims=True)
        o = og_scr[:, h * HP:h * HP + GLA_DV] + qk * gv
        ms = jnp.mean(o * o, axis=1, keepdims=True)
        on = o * lax.rsqrt(ms + LN_EPS) * gnorm_ref[:, h * HP:h * HP + GLA_DV]
        gg = z_scr[rows, GG_OFF + h * HP:GG_OFF + h * HP + GLA_DV]
        mix_scr[rows, YG_OFF + h * HP:YG_OFF + h * HP + GLA_DV] = gg * jax.nn.sigmoid(gg) * on

    @pl.when(gi == n_g - 1)
    def _():
        y = ALPHA * x_ref[...] + _dot(mix_scr[...].astype(BF16), w_out_ref[...])
        xo_ref[...] = _layer_norm_rows(y, lng_ref[...], lnb_ref[...])


def _sample_mixer(x, states, consts, carried, layer):
    B = x.shape[0]
    out_shape = (jax.ShapeDtypeStruct((B, D_MODEL), F32),) + tuple(
        jax.ShapeDtypeStruct(s.shape, F32) for s in states)
    state_specs = [
        pl.BlockSpec((None, SG, POOL_BUF, D_POOL), lambda g: (layer, g, 0, 0)),
        pl.BlockSpec((None, SG, HEADS, MLSTM_DH, MLSTM_DH), lambda g: (layer, g, 0, 0, 0)),
        pl.BlockSpec((None, SG, HEADS * MLSTM_DH), lambda g: (layer, g, 0)),
        pl.BlockSpec((None, SG, HEADS), lambda g: (layer, g, 0)),
        pl.BlockSpec((None, SG, HEADS, GLA_DK, GLA_DV), lambda g: (layer, g, 0, 0, 0)),
    ]
    x_spec = pl.BlockSpec((B, D_MODEL), lambda g: (0, 0))
    n_in = 1 + len(states) + len(consts)
    return pl.pallas_call(
        functools.partial(_sample_mixer_kernel, len(carried)),
        grid=(B // SG,),
        in_specs=[x_spec] + state_specs + [_layer_spec(cst.shape, layer) for cst in consts]
        + [pl.BlockSpec(memory_space=pl.ANY)] * len(carried),
        out_specs=[x_spec] + state_specs,
        out_shape=out_shape,
        input_output_aliases={n_in + i: 1 + i for i in range(len(carried))},
        scratch_shapes=[
            pltpu.VMEM((B, N_Z), F32),
            pltpu.VMEM((B, N_MIX), F32),
            pltpu.VMEM((SG, HEADS * HP), F32),
            pltpu.VMEM((SG, HEADS * HP), F32),
            pltpu.VMEM((SG, D_POOL), F32),
        ],
        compiler_params=pltpu.CompilerParams(
            dimension_semantics=("arbitrary",), vmem_limit_bytes=VMEM_LIMIT),
        name="sample_mixer",
    )(x, *states, *consts, *carried)


K_SPLIT = 2
CK = MLSTM_DH // K_SPLIT
SK = GLA_DK // K_SPLIT


def _row_of(a, idx):
    sub = lax.broadcasted_iota(jnp.int32, a.shape, 0)
    return jnp.sum(jnp.where(sub == idx, a, 0.0), axis=0, keepdims=True)


def _sample_step_kernel(n_carried, *refs):
    (x_ref, pool_ref, c_ref, n_ref, m_ref, s_ref,
     w_in_ref, w_out_ref, gbias_ref, a2t_ref, abias_ref, mnorm_ref, gnorm_ref,
     pmix_ref, pscale_ref, lng_ref, lnb_ref) = refs[:17]
    (xo_ref, pool_o_ref, c_o_ref, n_o_ref, m_o_ref, s_o_ref,
     zt_scr, mixt_scr, gate_scr, ea_scr, num_scr, oin_scr) = refs[17 + n_carried:]
    h = pl.program_id(0)
    kh = pl.program_id(1)
    B = x_ref.shape[0]
    first = jnp.logical_and(h == 0, kh == 0)
    last = jnp.logical_and(h == HEADS - 1, kh == K_SPLIT - 1)

    @pl.when(first)
    def _():
        mixt_scr[...] = jnp.zeros(mixt_scr.shape, F32)
        xb = x_ref[...].astype(BF16)
        for c0 in range(0, N_Z, LANES):
            zt_scr[c0:c0 + LANES, :] = _dot_nt(w_in_ref[c0:c0 + LANES, :], xb)
        g = zt_scr[GATE_OFF:GATE_OFF + LANES, :]
        gb = g + gbias_ref[...]
        log_i = gb[GATE_I:GATE_I + HEADS]
        log_f = _log_sigmoid(gb[GATE_F:GATE_F + HEADS])
        inter = log_f + m_ref[...]
        m_t = jnp.maximum(inter, log_i)
        m_o_ref[...] = m_t
        gate_scr[0:HEADS, :] = jnp.exp(inter - m_t)
        gate_scr[HEADS:2 * HEADS, :] = jnp.exp(log_i - m_t)
        gate_scr[2 * HEADS:3 * HEADS, :] = jnp.exp(-m_t)
        ya = _dot(a2t_ref[...], g.astype(BF16)) + abias_ref[...]
        ea_scr[...] = jnp.exp(_log_sigmoid(ya) * (1.0 / GLA_TAU))

    gates = gate_scr[...]
    dec = _row_of(gates, h)
    wig = _row_of(gates, h + HEADS)
    floor = _row_of(gates, h + 2 * HEADS)

    def head_rows(off, width, count, start=0):
        return zt_scr[pl.ds(pl.multiple_of(off + h * width + start, SUBLANES), count), :]

    k0 = kh * CK
    q_sl = head_rows(MQ_OFF, HP, CK, k0) * (MLSTM_DH ** -0.5)
    k_sl = head_rows(MK_OFF, HP, CK, k0) * wig
    v_t = head_rows(MV_OFF, HP, MLSTM_DH)
    acc = jnp.zeros((MLSTM_DH, B), F32)
    for k in range(CK):
        c_old = c_ref[k]
        acc = acc + q_sl[k:k + 1, :] * c_old
        c_o_ref[k] = dec * c_old + k_sl[k:k + 1, :] * v_t

    @pl.when(kh == 0)
    def _():
        num_scr[...] = acc

    @pl.when(kh > 0)
    def _():
        num_scr[...] += acc

    @pl.when(kh == K_SPLIT - 1)
    def _():
        q_t = head_rows(MQ_OFF, HP, MLSTM_DH) * (MLSTM_DH ** -0.5)
        k_t = head_rows(MK_OFF, HP, MLSTM_DH)
        n_old = n_ref[...]
        sc = jnp.sum(q_t * k_t, axis=0, keepdims=True) * wig
        num = dec * num_scr[...] + sc * v_t
        nq = dec * jnp.sum(q_t * n_old, axis=0, keepdims=True) + sc
        hval = num / jnp.maximum(jnp.abs(nq), floor)
        n_o_ref[...] = dec * n_old + wig * k_t
        mu = jnp.mean(hval, axis=0, keepdims=True)
        dlt = hval - mu
        var = jnp.mean(dlt * dlt, axis=0, keepdims=True)
        w_norm = mnorm_ref[pl.ds(pl.multiple_of(h * HP, SUBLANES), MLSTM_DH), :]
        mo = head_rows(MO_OFF, HP, MLSTM_DH)
        mixt_scr[pl.ds(pl.multiple_of(YM_OFF + h * HP, SUBLANES), MLSTM_DH), :] = (
            jax.nn.sigmoid(mo) * dlt * lax.rsqrt(var + LN_EPS) * w_norm)

    s0 = kh * SK
    ea_sl = ea_scr[pl.ds(pl.multiple_of(h * KP + s0, SUBLANES), SK), :]
    gq_sl = head_rows(GQ_OFF, KP, SK, s0) * (GLA_DK ** -0.5)
    gk_sl = head_rows(GK_OFF, KP, SK, s0)
    gv_t = head_rows(GV_OFF, HP, GLA_DV)
    gqe_sl = gq_sl * ea_sl
    oacc = jnp.zeros((GLA_DV, B), F32)
    for k in range(SK):
        s_old = s_ref[k]
        oacc = oacc + gqe_sl[k:k + 1, :] * s_old
        s_o_ref[k] = ea_sl[k:k + 1, :] * s_old + gk_sl[k:k + 1, :] * gv_t

    @pl.when(kh == 0)
    def _():
        oin_scr[...] = oacc

    @pl.when(kh > 0)
    def _():
        oin_scr[...] += oacc

    @pl.when(kh == K_SPLIT - 1)
    def _():
        gq_t = head_rows(GQ_OFF, KP, GLA_DK) * (GLA_DK ** -0.5)
        gk_t = head_rows(GK_OFF, KP, GLA_DK)
        qk = jnp.sum(gq_t * gk_t, axis=0, keepdims=True)
        o = oin_scr[...] + qk * gv_t
        ms = jnp.mean(o * o, axis=0, keepdims=True)
        w_norm = gnorm_ref[pl.ds(pl.multiple_of(h * HP, SUBLANES), GLA_DV), :]
        gg = head_rows(GG_OFF, HP, GLA_DV)
        mixt_scr[pl.ds(pl.multiple_of(YG_OFF + h * HP, SUBLANES), GLA_DV), :] = (
            gg * jax.nn.sigmoid(gg) * o * lax.rsqrt(ms + LN_EPS) * w_norm)

    @pl.when(last)
    def _():
        u = jnp.concatenate([jnp.transpose(zt_scr[c0:c0 + LANES, :]) for c0 in range(0, D_POOL, LANES)], axis=1)
        run = u
        sums = {}
        for j in range(1, max(POOL_WINDOWS)):
            run = run + pool_ref[POOL_BUF - j]
            if j + 1 in POOL_WINDOWS:
                sums[j + 1] = run
        grp = lax.broadcasted_iota(jnp.int32, (B, D_POOL), 1) // POOL_GROUP
        wmean = jnp.zeros((B, D_POOL), F32)
        for gidx, w in enumerate(POOL_WINDOWS):
            wmean = jnp.where(grp == gidx, sums[w] / float(min(w, PAST_LEN + 1)), wmean)
        y_pool = _dot((wmean - u).astype(BF16), pmix_ref[...]) * pscale_ref[...]
        for j in range(POOL_BUF - 1):
            pool_o_ref[j] = pool_ref[j + 1]
        pool_o_ref[POOL_BUF - 1] = u
        mix = jnp.concatenate(
            [y_pool] + [jnp.transpose(mixt_scr[c0:c0 + LANES, :]) for c0 in range(YM_OFF, N_MIX, LANES)], axis=1)
        y = ALPHA * x_ref[...] + _dot(mix.astype(BF16), w_out_ref[...])
        xo_ref[...] = _layer_norm_rows(y, lng_ref[...], lnb_ref[...])


def _sample_step(x, states, consts, carried, layer):
    B = x.shape[0]
    out_shape = (jax.ShapeDtypeStruct((B, D_MODEL), F32),) + tuple(
        jax.ShapeDtypeStruct(s.shape, F32) for s in states)
    state_specs = [
        pl.BlockSpec((None, POOL_BUF, B, D_POOL), lambda h, k: (layer, 0, 0, 0)),
        pl.BlockSpec((None, None, CK, MLSTM_DH, B), lambda h, k: (layer, h, k, 0, 0)),
        pl.BlockSpec((None, None, MLSTM_DH, B), lambda h, k: (layer, h, 0, 0)),
        pl.BlockSpec((None, HEADS, B), lambda h, k: (layer, 0, 0)),
        pl.BlockSpec((None, None, SK, GLA_DV, B), lambda h, k: (layer, h, k, 0, 0)),
    ]
    x_spec = pl.BlockSpec((B, D_MODEL), lambda h, k: (0, 0))
    n_in = 1 + len(states) + len(consts)
    return pl.pallas_call(
        functools.partial(_sample_step_kernel, len(carried)),
        grid=(HEADS, K_SPLIT),
        in_specs=[x_spec] + state_specs + [_layer_spec(cst.shape, layer) for cst in consts]
        + [pl.BlockSpec(memory_space=pl.ANY)] * len(carried),
        out_specs=[x_spec] + state_specs,
        out_shape=out_shape,
        input_output_aliases={n_in + i: 1 + i for i in range(len(carried))},
        scratch_shapes=[
            pltpu.VMEM((N_Z, B), F32),
            pltpu.VMEM((N_MIX, B), F32),
            pltpu.VMEM((2 * SUBLANES, B), F32),
            pltpu.VMEM((HEADS * KP, B), F32),
            pltpu.VMEM((MLSTM_DH, B), F32),
            pltpu.VMEM((GLA_DV, B), F32),
        ],
        compiler_params=pltpu.CompilerParams(
            dimension_semantics=("arbitrary", "arbitrary"), vmem_limit_bytes=VMEM_LIMIT),
        name="sample_step",
    )(x, *states, *consts, *carried)


def _ffn_kernel(x_ref, p_ref, wg_ref, wu_ref, wd_ref, wp_ref, wpg_ref, lng_ref, lnb_ref, o_ref):
    x = x_ref[...]
    xb = x.astype(BF16)
    acc = ALPHA * x + _dot(p_ref[...].astype(BF16), wp_ref[...]) * jax.nn.sigmoid(_dot(xb, wpg_ref[...]))
    for c0 in range(0, D_FF, FF_CHUNK):
        c1 = min(c0 + FF_CHUNK, D_FF)
        gate = _dot(xb, wg_ref[:, c0:c1])
        up = _dot(xb, wu_ref[:, c0:c1])
        hid = (gate * jax.nn.sigmoid(gate) * up).astype(BF16)
        acc = acc + _dot(hid, wd_ref[c0:c1, :])
    o_ref[...] = _layer_norm_rows(acc, lng_ref[...], lnb_ref[...])


def _ffn(x, p, consts, layer, tm):
    M = x.shape[0]
    return pl.pallas_call(
        _ffn_kernel,
        grid=(M // tm,),
        in_specs=[pl.BlockSpec((tm, D_MODEL), lambda i: (i, 0)),
                  pl.BlockSpec((None, tm, D_PLE), lambda i: (layer, i, 0))]
        + [_layer_spec(c.shape, layer) for c in consts],
        out_specs=pl.BlockSpec((tm, D_MODEL), lambda i: (i, 0)),
        out_shape=jax.ShapeDtypeStruct((M, D_MODEL), F32),
        compiler_params=pltpu.CompilerParams(
            dimension_semantics=("arbitrary",), vmem_limit_bytes=VMEM_LIMIT),
        name="ffn",
    )(x, p, *consts)


def kernel(x_prompt, x_sample, p_prompt, p_sample, state_pool, state_mlstm_C, state_mlstm_n, state_mlstm_m, state_gla_S, w_in, mlstm_i_bias, mlstm_f_bias, mlstm_norm_w, gla_a2, gla_a_bias, gla_norm_w, pool_mix, pool_scale, w_out, ln1_g, ln1_b, w_gate, w_up, w_down, w_ple, w_ple_gate, ln2_g, ln2_b):
    Bp, Tp, _ = x_prompt.shape
    Bs = x_sample.shape[0]

    w_in_p = _pad_in_proj_t(w_in).astype(BF16)
    w_out_p = _pad_out_proj(w_out).astype(BF16)
    gbias = jnp.zeros((DEPTH, 1, LANES), F32)
    gbias = gbias.at[:, 0, GATE_I:GATE_I + HEADS].set(mlstm_i_bias.astype(F32))
    gbias = gbias.at[:, 0, GATE_F:GATE_F + HEADS].set(mlstm_f_bias.astype(F32))
    a2_p = jnp.zeros((DEPTH, LANES, HEADS * KP), F32)
    a2_p = a2_p.at[:, GATE_A:GATE_A + GLA_RANK, :].set(_pad_heads(gla_a2.astype(F32), GLA_DK, KP)).astype(BF16)
    abias_p = _pad_heads(gla_a_bias.astype(F32), GLA_DK, KP)[:, None, :]
    mnorm_p = _pad_heads(mlstm_norm_w.astype(F32), MLSTM_DH, HP)[:, None, :]
    gnorm_p = _pad_heads(gla_norm_w.astype(F32), GLA_DV, HP)[:, None, :]
    n_grp = len(POOL_WINDOWS)
    eye = jnp.eye(n_grp, dtype=F32)
    pmix_bd = (pool_mix.astype(F32)[:, :, :, None, :] * eye[None, :, None, :, None]).reshape(
        DEPTH, D_POOL, D_POOL).astype(BF16)
    pscale = pool_scale.astype(F32)[:, None, :]
    row = lambda a: a.astype(F32)[:, None, :]
    ln1_g_r, ln1_b_r, ln2_g_r, ln2_b_r = row(ln1_g), row(ln1_b), row(ln2_g), row(ln2_b)
    wg_b, wu_b, wd_b = w_gate.astype(BF16), w_up.astype(BF16), w_down.astype(BF16)
    wp_b, wpg_b = w_ple.astype(BF16), w_ple_gate.astype(BF16)

    mixer_consts = (w_in_p, w_out_p, gbias, a2_p, abias_p, mnorm_p, gnorm_p, pmix_bd, pscale, ln1_g_r, ln1_b_r)
    ffn_consts = (wg_b, wu_b, wd_b, wp_b, wpg_b, ln2_g_r, ln2_b_r)
    xp = x_prompt
    xs = x_sample.reshape(Bs, D_MODEL)
    pp = p_prompt.reshape(DEPTH, Bp * Tp, D_PLE)
    ps = p_sample.reshape(DEPTH, Bs, D_PLE)
    sample_states = (jnp.transpose(state_pool, (0, 2, 1, 3)), jnp.transpose(state_mlstm_C, (0, 2, 3, 4, 1)),
                     jnp.transpose(state_mlstm_n, (0, 2, 3, 1)), jnp.transpose(state_mlstm_m, (0, 2, 1)),
                     jnp.transpose(state_gla_S, (0, 2, 3, 4, 1)))
    col = lambda a: jnp.transpose(a, (0, 2, 1))
    sample_consts = (w_in_p, w_out_p, col(gbias), col(a2_p), col(abias_p), col(mnorm_p), col(gnorm_p),
                     pmix_bd, pscale, ln1_g_r, ln1_b_r)
    outs_p = []
    carried = ()
    for l in range(DEPTH):
        xp, pool_p, c_p, n_p, m_p, s_p = _prompt_mixer(xp, mixer_consts, l)
        xp = _ffn(xp.reshape(Bp * Tp, D_MODEL), pp, ffn_consts, l, tm=512).reshape(Bp, Tp, D_MODEL)
        outs_p.append((pool_p, c_p, n_p[:, :, :MLSTM_DH], m_p[:, 0, GATE_F:GATE_F + HEADS], s_p))
        xs, *carried = _sample_step(xs, sample_states, sample_consts, tuple(carried), l)
        xs = _ffn(xs, ps, ffn_consts, l, tm=Bs)

    pool_s, c_s, n_s, m_s, s_s = carried
    stack = lambda j: jnp.stack([o[j] for o in outs_p], axis=0)
    return ((xp, xs.reshape(Bs, 1, D_MODEL)) + tuple(stack(j) for j in range(5))
            + (jnp.transpose(pool_s, (0, 2, 1, 3)), jnp.transpose(c_s, (0, 4, 1, 2, 3)),
               jnp.transpose(n_s, (0, 3, 1, 2)), jnp.transpose(m_s, (0, 2, 1)),
               jnp.transpose(s_s, (0, 4, 1, 2, 3))))
```

```python
import functools

import numpy as np
import jax
import jax.numpy as jnp
from jax import lax
from jax.experimental import pallas as pl
from jax.experimental.pallas import tpu as pltpu

F32 = jnp.float32
BF16 = jnp.bfloat16

D_MODEL = 1024
DEPTH = 4
PAST_LEN = 16384
D_POOL = 256
POOL_WINDOWS = (2, 4, 8, 16)
POOL_GROUP = 64
POOL_BUF = 15
D_MLSTM = 384
HEADS = 4
MLSTM_DH = 96
D_GLA = 384
GLA_DV = 96
GLA_DK = 48
GLA_RANK = 16
GLA_TAU = 16.0
D_MIX = 1024
D_FF = 2816
D_PLE = 256
CHUNK = 64
ALPHA = (2 * DEPTH) ** 0.25
LN_EPS = 1e-5

LANES = 128
SUBLANES = 8
VMEM_LIMIT = 56 * 1024 * 1024

HP = LANES
KP = 64
U_OFF = 0
MQ_OFF = U_OFF + D_POOL
MK_OFF = MQ_OFF + HEADS * HP
MV_OFF = MK_OFF + HEADS * HP
MO_OFF = MV_OFF + HEADS * HP
GQ_OFF = MO_OFF + HEADS * HP
GK_OFF = GQ_OFF + HEADS * KP
GV_OFF = GK_OFF + HEADS * KP
GG_OFF = GV_OFF + HEADS * HP
GATE_OFF = GG_OFF + HEADS * HP
N_Z = GATE_OFF + LANES
GATE_I = 0
GATE_F = 4
GATE_A = 8
YM_OFF = D_POOL
YG_OFF = YM_OFF + HEADS * HP
N_MIX = YG_OFF + HEADS * HP

TT = 512
PAIR = 2 * CHUNK
ZT_Q = 0
ZT_V = ZT_Q + HEADS * HP
ZT_O = ZT_V + HEADS * HP
ZT_ROWS = ZT_O + HEADS * HP
SG = SUBLANES
FF_CHUNK = 512


def _pad_heads(a, width, padded):
    lead = a.shape[:-1]
    a = a.reshape(lead + (HEADS, width))
    a = jnp.pad(a, [(0, 0)] * len(lead) + [(0, 0), (0, padded - width)])
    return a.reshape(lead + (HEADS * padded,))


def _pad_head_rows(a, width, padded):
    L, _, K = a.shape
    a = jnp.pad(a.reshape(L, HEADS, width, K), ((0, 0), (0, 0), (0, padded - width), (0, 0)))
    return a.reshape(L, HEADS * padded, K)


def _pad_in_proj_t(w_in):
    sizes = (D_POOL, D_MLSTM, D_MLSTM, D_MLSTM, HEADS, HEADS, D_MLSTM,
             HEADS * GLA_DK, HEADS * GLA_DK, D_GLA, GLA_RANK, D_GLA)
    w_t = jnp.transpose(w_in, (0, 2, 1))
    u, mq, mk, mv, mi, mf, mo, gq, gk, gv, ga, gg = jnp.split(w_t, np.cumsum(sizes)[:-1].tolist(), axis=1)
    gate_pad = jnp.zeros((w_t.shape[0], LANES - GATE_A - GLA_RANK, w_t.shape[2]), w_t.dtype)
    wide = lambda a: _pad_head_rows(a, MLSTM_DH, HP)
    narrow = lambda a: _pad_head_rows(a, GLA_DK, KP)
    return jnp.concatenate([u, wide(mq), wide(mk), wide(mv), wide(mo), narrow(gq), narrow(gk), wide(gv),
                            wide(gg), mi, mf, ga, gate_pad], axis=1)


def _pad_out_proj(w_out):
    pool = w_out[:, :D_POOL]
    heads = w_out[:, D_POOL:].reshape(w_out.shape[0], 2 * HEADS, MLSTM_DH, D_MODEL)
    heads = jnp.pad(heads, ((0, 0), (0, 0), (0, HP - MLSTM_DH), (0, 0)))
    return jnp.concatenate([pool, heads.reshape(w_out.shape[0], 2 * HEADS * HP, D_MODEL)], axis=1)


def _dot(a, b):
    return jnp.dot(a, b, preferred_element_type=F32)


def _dot_nt(a, b):
    return lax.dot_general(a, b, (((1,), (1,)), ((), ())), preferred_element_type=F32)


def _dot_tn(a, b):
    return lax.dot_general(a, b, (((0,), (0,)), ((), ())), preferred_element_type=F32)


def _log_sigmoid(x):
    return jnp.minimum(x, 0.0) - jnp.log1p(jnp.exp(-jnp.abs(x)))


def _cummax_rows(x, segment=None):
    n = x.shape[0] if segment is None else segment
    rid = lax.broadcasted_iota(jnp.int32, x.shape, 0) & (n - 1)
    sh = 1
    while sh < n:
        x = jnp.maximum(x, jnp.where(rid >= sh, pltpu.roll(x, sh, axis=0), -jnp.inf))
        sh *= 2
    return x


def _cumsum_rows(x, segment=None):
    n = x.shape[0] if segment is None else segment
    rid = lax.broadcasted_iota(jnp.int32, x.shape, 0) & (n - 1)
    sh = 1
    while sh < n:
        x = x + jnp.where(rid >= sh, pltpu.roll(x, sh, axis=0), 0.0)
        sh *= 2
    return x


def _layer_norm_rows(y, g, b):
    mu = jnp.mean(y, axis=-1, keepdims=True)
    d = y - mu
    var = jnp.mean(d * d, axis=-1, keepdims=True)
    return d * lax.rsqrt(var + LN_EPS) * g + b


def _pool_select(lane, a, b):
    return jnp.where((lane & (LANES - 1)) < POOL_GROUP, a, b)


def _prompt_mixer_kernel(x_ref, w_in_ref, w_out_ref, gbias_ref, a2_ref, abias_ref, mnorm_t_ref,
                         gnorm_ref, pmix_ref, pscale_ref, lng_ref, lnb_ref,
                         xo_ref, pool_o_ref, c_o_ref, n_o_ref, m_o_ref, s_o_ref,
                         z_scr, zt_scr, mix_scr, mixt_scr, ext_scr, c_scr, m_scr, s_scr):
    ti = pl.program_id(1)
    n_t = pl.num_programs(1)

    @pl.when(ti == 0)
    def _():
        ext_scr[0:2 * SUBLANES, :] = jnp.zeros((2 * SUBLANES, D_POOL), F32)
        c_scr[...] = jnp.zeros(c_scr.shape, F32)
        m_scr[...] = jnp.zeros(m_scr.shape, F32)
        s_scr[...] = jnp.zeros(s_scr.shape, F32)

    xb = x_ref[0].astype(BF16)
    for lo, hi in ((U_OFF, MQ_OFF), (MK_OFF, MV_OFF), (GQ_OFF, N_Z)):
        for c0 in range(lo, hi, 512):
            c1 = min(c0 + 512, hi)
            z_scr[:, c0:c1] = _dot_nt(xb, w_in_ref[c0:c1, :])
    for src, dst in ((MQ_OFF, ZT_Q), (MV_OFF, ZT_V), (MO_OFF, ZT_O)):
        zt_scr[dst:dst + HEADS * HP, :] = _dot_nt(w_in_ref[src:src + HEADS * HP, :], xb)

    hist = 2 * SUBLANES
    u = z_scr[:, U_OFF:U_OFF + D_POOL]
    ext_scr[hist:hist + TT, :] = u
    ext = ext_scr[...]
    e0 = ext[:, 0:LANES]
    e1 = ext[:, LANES:2 * LANES]
    a2s = e0 + pltpu.roll(e0, 1, axis=0)
    a4s = a2s + pltpu.roll(a2s, 2, axis=0)
    b2s = e1 + pltpu.roll(e1, 1, axis=0)
    b4s = b2s + pltpu.roll(b2s, 2, axis=0)
    b8s = b4s + pltpu.roll(b4s, 4, axis=0)
    b16s = b8s + pltpu.roll(b8s, 8, axis=0)
    lane = lax.broadcasted_iota(jnp.int32, (TT, LANES), 1)
    pos1 = (lax.broadcasted_iota(jnp.int32, (TT, LANES), 0) + ti * TT + 1).astype(F32)
    ws0 = _pool_select(lane, a2s[hist:], a4s[hist:])
    ws1 = _pool_select(lane, b8s[hist:], b16s[hist:])
    cnt0 = jnp.minimum(_pool_select(lane, 2.0, 4.0), pos1)
    cnt1 = jnp.minimum(_pool_select(lane, 8.0, 16.0), pos1)
    d0 = ws0 / cnt0 - u[:, 0:LANES]
    d1 = ws1 / cnt1 - u[:, LANES:]
    dpool = jnp.concatenate([d0, d1], axis=1).astype(BF16)
    y_pool = _dot(dpool, pmix_ref[...]) * pscale_ref[...]
    mix_scr[:, 0:D_POOL] = y_pool.astype(BF16)
    tail = ext_scr[TT:TT + hist, :]
    ext_scr[0:hist, :] = tail

    @pl.when(ti == n_t - 1)
    def _():
        pool_o_ref[0] = ext_scr[1:hist, :]

    gbias = gbias_ref[...]
    abias = abias_ref[...]

    def chunk_body(c):
        r0 = pl.multiple_of(c * CHUNK, CHUNK)
        rows = pl.ds(r0, CHUNK)
        graw = z_scr[rows, GATE_OFF:GATE_OFF + LANES]

        ya = _dot(graw.astype(BF16), a2_ref[...]) + abias
        la = _log_sigmoid(ya) * (1.0 / GLA_TAU)
        bc = _cumsum_rows(la)
        gq = z_scr[rows, GQ_OFF:GQ_OFF + HEADS * KP] * (GLA_DK ** -0.5)
        gk = z_scr[rows, GK_OFF:GK_OFF + HEADS * KP]
        width = HEADS * KP
        t_id = lax.broadcasted_iota(jnp.int32, (CHUNK, width), 0)
        lane_head = lax.broadcasted_iota(jnp.int32, (CHUNK, width), 1) // KP

        def stack_heads(a):
            return jnp.concatenate([jnp.where(lane_head == h, a, 0.0) for h in range(HEADS)],
                                   axis=0).astype(BF16)

        trow = lax.broadcasted_iota(jnp.int32, (HEADS * CHUNK, CHUNK), 0) & (CHUNK - 1)
        scol = lax.broadcasted_iota(jnp.int32, (HEADS * CHUNK, CHUNK), 1)
        amat = jnp.where(trow == scol, _dot_nt(stack_heads(gq), gk.astype(BF16)), 0.0)
        half = CHUNK // 2
        while half >= 1:
            blk = 2 * half
            upper = (t_id & half) != 0
            if blk >= SUBLANES:
                ref_rows = jnp.concatenate(
                    [jnp.broadcast_to(bc[b0 + half - 1:b0 + half, :], (blk, width))
                     for b0 in range(0, CHUNK, blk)], axis=0)
            else:
                off = t_id & (blk - 1)
                ref_rows = bc
                for o in range(blk):
                    sh = (o - (half - 1)) % CHUNK
                    if sh != 0:
                        ref_rows = jnp.where(off == o, pltpu.roll(bc, sh, axis=0), ref_rows)
            dlt = bc - ref_rows
            qe = jnp.where(upper, gq * jnp.exp(jnp.minimum(dlt, 0.0)), 0.0)
            ke = jnp.where(upper, 0.0, gk * jnp.exp(jnp.minimum(-dlt, 0.0)))
            p = _dot_nt(stack_heads(qe), ke.astype(BF16))
            shift = blk.bit_length() - 1
            amat = amat + jnp.where((trow >> shift) == (scol >> shift), p, 0.0)
            half //= 2

        o_inter = _dot(stack_heads(gq * jnp.exp(bc)), s_scr[...].astype(BF16))
        bc_t = jnp.transpose(bc)
        gk_t = jnp.transpose(gk)
        last_t = bc_t[:, CHUNK - 1:CHUNK]
        ke_t = (gk_t * jnp.exp(last_t - bc_t)).astype(BF16)
        s_dec = jnp.exp(last_t)
        for h in range(HEADS):
            hr = slice(h * CHUNK, (h + 1) * CHUNK)
            gv = z_scr[rows, GV_OFF + h * HP:GV_OFF + (h + 1) * HP].astype(BF16)
            o = o_inter[hr] + _dot(amat[hr].astype(BF16), gv)
            ms = jnp.sum(o * o, axis=1, keepdims=True) * (1.0 / GLA_DV)
            on = o * lax.rsqrt(ms + LN_EPS) * gnorm_ref[:, h * HP:(h + 1) * HP]
            gg = z_scr[rows, GG_OFF + h * HP:GG_OFF + (h + 1) * HP]
            mix_scr[rows, YG_OFF + h * HP:YG_OFF + (h + 1) * HP] = (gg * jax.nn.sigmoid(gg) * on).astype(BF16)
            kr = slice(h * KP, (h + 1) * KP)
            s_scr[kr, :] = s_dec[kr] * s_scr[kr, :] + _dot(ke_t[kr], gv)

    first_r = lax.broadcasted_iota(jnp.int32, (PAIR, LANES), 0) < CHUNK
    src_id = lax.broadcasted_iota(jnp.int32, (PAIR, PAIR), 0)
    tok_id = lax.broadcasted_iota(jnp.int32, (PAIR, PAIR), 1)
    pair_mask = ((src_id // CHUNK) == (tok_id // CHUNK)) & (src_id <= tok_id)
    first = tok_id < CHUNK
    valid = src_id < MLSTM_DH

    def pair_body(p, carry):
        chunk_body(2 * p)
        chunk_body(2 * p + 1)
        t0 = pl.multiple_of(p * PAIR, PAIR)
        lanes = pl.ds(t0, PAIR)
        rows = pl.ds(t0, PAIR)
        g_r = z_scr[rows, GATE_OFF:GATE_OFF + LANES] + gbias
        b = _cumsum_rows(_log_sigmoid(g_r), segment=CHUNK)
        i_sh = pltpu.roll(g_r, GATE_F - GATE_I, axis=1)
        r_r = i_sh - b
        bcm = b + _cummax_rows(r_r, segment=CHUNK)
        m_in = m_scr[0:1, :]
        b_mid = b[CHUNK - 1:CHUNK, :]
        b_end = b[PAIR - 1:PAIR, :]
        m_mid = jnp.maximum(b_mid + m_in, bcm[CHUNK - 1:CHUNK, :])
        m_in2 = jnp.where(first_r, m_in, m_mid)
        inter = b + m_in2
        m_t = jnp.maximum(inter, bcm)
        m_end = m_t[PAIR - 1:PAIR, :]
        m_scr[0:1, :] = m_end
        m_out2 = jnp.where(first_r, m_mid, m_end)
        b_last2 = jnp.where(first_r, b_mid, b_end)
        a = jnp.transpose(b - m_t)
        w_inter = jnp.transpose(jnp.exp(inter - m_t))
        floor = jnp.transpose(jnp.exp(-m_t))
        w_end = jnp.transpose(jnp.exp(b_last2 - b + i_sh - m_out2))
        dec_a = jnp.exp(b_mid + m_in - m_mid)
        dec_b = jnp.exp(b_end + m_mid - m_end)
        for h in range(HEADS):
            gl = GATE_F + h
            hr = slice(h * HP, (h + 1) * HP)
            q_t = (zt_scr[ZT_Q + h * HP:ZT_Q + (h + 1) * HP, lanes] * (MLSTM_DH ** -0.5)).astype(BF16)
            v_aug = jnp.where(src_id == MLSTM_DH, 1.0,
                              zt_scr[ZT_V + h * HP:ZT_V + (h + 1) * HP, lanes])
            k_r = z_scr[rows, MK_OFF + h * HP:MK_OFF + (h + 1) * HP].astype(BF16)
            e_t = jnp.exp(jnp.where(pair_mask, a[gl:gl + 1, :] + r_r[:, gl:gl + 1], -jnp.inf))
            s_t = (_dot(k_r, q_t) * e_t).astype(BF16)
            vw = v_aug * w_end[gl:gl + 1, :]
            ct_a = c_scr[h]
            ct_b = dec_a[:, gl:gl + 1] * ct_a + _dot(jnp.where(first, vw, 0.0).astype(BF16), k_r)
            c_scr[h] = dec_b[:, gl:gl + 1] * ct_b + _dot(jnp.where(first, 0.0, vw).astype(BF16), k_r)
            qc = jnp.where(first, _dot(ct_a.astype(BF16), q_t), _dot(ct_b.astype(BF16), q_t))
            num = w_inter[gl:gl + 1, :] * qc + _dot(v_aug.astype(BF16), s_t)
            hval = num / jnp.maximum(jnp.abs(num[MLSTM_DH:MLSTM_DH + 1, :]), floor[gl:gl + 1, :])
            mu = jnp.sum(jnp.where(valid, hval, 0.0), axis=0, keepdims=True) * (1.0 / MLSTM_DH)
            dlt = jnp.where(valid, hval - mu, 0.0)
            var = jnp.sum(dlt * dlt, axis=0, keepdims=True) * (1.0 / MLSTM_DH)
            mo_t = zt_scr[ZT_O + h * HP:ZT_O + (h + 1) * HP, lanes]
            mixt_scr[hr, lanes] = (jax.nn.sigmoid(mo_t) * dlt * lax.rsqrt(var + LN_EPS)
                                   * mnorm_t_ref[hr, :]).astype(BF16)
        return carry

    lax.fori_loop(0, TT // PAIR, pair_body, 0)

    y = (ALPHA * x_ref[0] + _dot(mix_scr[:, 0:YM_OFF], w_out_ref[0:YM_OFF, :])
         + _dot_tn(mixt_scr[...], w_out_ref[YM_OFF:YG_OFF, :])
         + _dot(mix_scr[:, YG_OFF:N_MIX], w_out_ref[YG_OFF:N_MIX, :]))
    xo_ref[0] = _layer_norm_rows(y, lng_ref[...], lnb_ref[...])

    @pl.when(ti == n_t - 1)
    def _():
        for h in range(HEADS):
            ct = c_scr[h]
            c_o_ref[0, h] = jnp.transpose(ct)[0:MLSTM_DH, 0:MLSTM_DH]
            n_o_ref[0, h:h + 1, :] = ct[MLSTM_DH:MLSTM_DH + 1, :]
            s_o_ref[0, h] = s_scr[h * KP:h * KP + GLA_DK, 0:GLA_DV]
        m_o_ref[0] = m_scr[...]


def _layer_spec(shape, layer):
    nd = len(shape)
    return pl.BlockSpec((None,) + tuple(shape[1:]), lambda *_: (layer,) + (0,) * (nd - 1),
                        pipeline_mode=pl.Buffered(1))


def _prompt_mixer(x, consts, layer):
    B, T, _ = x.shape
    out_shape = (
        jax.ShapeDtypeStruct((B, T, D_MODEL), F32),
        jax.ShapeDtypeStruct((B, POOL_BUF, D_POOL), F32),
        jax.ShapeDtypeStruct((B, HEADS, MLSTM_DH, MLSTM_DH), F32),
        jax.ShapeDtypeStruct((B, HEADS, HP), F32),
        jax.ShapeDtypeStruct((B, SUBLANES, LANES), F32),
        jax.ShapeDtypeStruct((B, HEADS, GLA_DK, GLA_DV), F32),
    )
    out_specs = (
        pl.BlockSpec((1, TT, D_MODEL), lambda b, t: (b, t, 0)),
        pl.BlockSpec((1, POOL_BUF, D_POOL), lambda b, t: (b, 0, 0)),
        pl.BlockSpec((1, HEADS, MLSTM_DH, MLSTM_DH), lambda b, t: (b, 0, 0, 0)),
        pl.BlockSpec((1, HEADS, HP), lambda b, t: (b, 0, 0)),
        pl.BlockSpec((1, SUBLANES, LANES), lambda b, t: (b, 0, 0)),
        pl.BlockSpec((1, HEADS, GLA_DK, GLA_DV), lambda b, t: (b, 0, 0, 0)),
    )
    return pl.pallas_call(
        _prompt_mixer_kernel,
        grid=(B, T // TT),
        in_specs=[pl.BlockSpec((1, TT, D_MODEL), lambda b, t: (b, t, 0))]
        + [_layer_spec(c.shape, layer) for c in consts],
        out_specs=out_specs,
        out_shape=out_shape,
        scratch_shapes=[
            pltpu.VMEM((TT, N_Z), F32),
            pltpu.VMEM((ZT_ROWS, TT), F32),
            pltpu.VMEM((TT, N_MIX), BF16),
            pltpu.VMEM((HEADS * HP, TT), BF16),
            pltpu.VMEM((TT + 2 * SUBLANES, D_POOL), F32),
            pltpu.VMEM((HEADS, HP, HP), F32),
            pltpu.VMEM((SUBLANES, LANES), F32),
            pltpu.VMEM((HEADS * KP, HP), F32),
        ],
        compiler_params=pltpu.CompilerParams(
            dimension_semantics=("arbitrary", "arbitrary"), vmem_limit_bytes=VMEM_LIMIT),
        name="prompt_mixer",
    )(x, *consts)


def _column_bcast(row):
    return jnp.transpose(jnp.broadcast_to(row, (LANES, LANES)))


def _sample_mixer_kernel(n_carried, *refs):
    (x_ref, pool_ref, c_ref, n_ref, m_ref, s_ref,
     w_in_ref, w_out_ref, gbias_ref, a2_ref, abias_ref, mnorm_ref,
     gnorm_ref, pmix_ref, pscale_ref, lng_ref, lnb_ref) = refs[:17]
    (xo_ref, pool_o_ref, c_o_ref, n_o_ref, m_o_ref, s_o_ref,
     z_scr, mix_scr, qc_scr, og_scr, psum_scr) = refs[17 + n_carried:]
    gi = pl.program_id(0)
    n_g = pl.num_programs(0)

    @pl.when(gi == 0)
    def _():
        mix_scr[...] = jnp.zeros(mix_scr.shape, F32)
        xb = x_ref[...].astype(BF16)
        for c0 in range(0, N_Z, 512):
            c1 = min(c0 + 512, N_Z)
            z_scr[:, c0:c1] = _dot(xb, w_in_ref[:, c0:c1])

    r0 = pl.multiple_of(gi * SG, SG)
    rows = pl.ds(r0, SG)

    u = z_scr[rows, U_OFF:U_OFF + D_POOL]

    def window_of(lane):
        grp = lane // POOL_GROUP
        w = jnp.full(lane.shape, POOL_WINDOWS[0], jnp.int32)
        for gidx in range(1, len(POOL_WINDOWS)):
            w = jnp.where(grp == gidx, POOL_WINDOWS[gidx], w)
        return w

    hist_row = lax.broadcasted_iota(jnp.int32, (POOL_BUF, D_POOL), 0)
    in_window = hist_row > POOL_BUF - window_of(lax.broadcasted_iota(jnp.int32, (POOL_BUF, D_POOL), 1))
    for j in range(SG):
        psum_scr[j:j + 1, :] = jnp.sum(jnp.where(in_window, pool_ref[j], 0.0), axis=0, keepdims=True)
        pool_o_ref[j, 0:POOL_BUF - 1, :] = pool_ref[j, 1:POOL_BUF, :]
        pool_o_ref[j, POOL_BUF - 1:POOL_BUF, :] = u[j:j + 1, :]
    cnt = jnp.minimum(window_of(lax.broadcasted_iota(jnp.int32, (SG, D_POOL), 1)), PAST_LEN + 1).astype(F32)
    dpool = ((psum_scr[...] + u) / cnt - u).astype(BF16)
    mix_scr[rows, 0:D_POOL] = _dot(dpool, pmix_ref[...]) * pscale_ref[...]

    graw = z_scr[rows, GATE_OFF:GATE_OFF + LANES]
    g = graw + gbias_ref[...]

    log_i = g[:, GATE_I:GATE_I + HEADS]
    log_f = _log_sigmoid(g[:, GATE_F:GATE_F + HEADS])
    inter = log_f + m_ref[...]
    m_t = jnp.maximum(inter, log_i)
    w_inter = jnp.exp(inter - m_t)
    w_in_gate = jnp.exp(log_i - m_t)
    m_o_ref[...] = m_t
    for h in range(HEADS):
        q = z_scr[rows, MQ_OFF + h * HP:MQ_OFF + (h + 1) * HP] * (MLSTM_DH ** -0.5)
        k = z_scr[rows, MK_OFF + h * HP:MK_OFF + (h + 1) * HP]
        v = z_scr[rows, MV_OFF + h * HP:MV_OFF + (h + 1) * HP]
        n_old = n_ref[:, h * MLSTM_DH:(h + 1) * MLSTM_DH]
        dec = w_inter[:, h:h + 1]
        wig = w_in_gate[:, h:h + 1]
        for j in range(SG):
            c_old = c_ref[j, h]
            qcol = _column_bcast(q[j:j + 1, :])[0:MLSTM_DH, 0:MLSTM_DH]
            kcol = _column_bcast(k[j:j + 1, :])[0:MLSTM_DH, 0:MLSTM_DH]
            qc_scr[j:j + 1, h * HP:h * HP + MLSTM_DH] = jnp.sum(qcol * c_old, axis=0, keepdims=True)
            c_o_ref[j, h] = dec[j:j + 1, :] * c_old + kcol * (wig[j:j + 1, :] * v[j:j + 1, 0:MLSTM_DH])
        q96 = q[:, 0:MLSTM_DH]
        k96 = k[:, 0:MLSTM_DH]
        v96 = v[:, 0:MLSTM_DH]
        sc = jnp.sum(q96 * k96, axis=1, keepdims=True) * wig
        num = dec * qc_scr[:, h * HP:h * HP + MLSTM_DH] + sc * v96
        nq = dec * jnp.sum(q96 * n_old, axis=1, keepdims=True) + sc
        hval = num / jnp.maximum(jnp.abs(nq), jnp.exp(-m_t[:, h:h + 1]))
        n_o_ref[:, h * MLSTM_DH:(h + 1) * MLSTM_DH] = dec * n_old + wig * k96
        mu = jnp.mean(hval, axis=1, keepdims=True)
        dlt = hval - mu
        var = jnp.mean(dlt * dlt, axis=1, keepdims=True)
        hn = dlt * lax.rsqrt(var + LN_EPS) * mnorm_ref[:, h * HP:h * HP + MLSTM_DH]
        mo = z_scr[rows, MO_OFF + h * HP:MO_OFF + h * HP + MLSTM_DH]
        mix_scr[rows, YM_OFF + h * HP:YM_OFF + h * HP + MLSTM_DH] = jax.nn.sigmoid(mo) * hn

    ya = _dot(graw.astype(BF16), a2_ref[...]) + abias_ref[...]
    ea = jnp.exp(_log_sigmoid(ya) * (1.0 / GLA_TAU))
    gq = z_scr[rows, GQ_OFF:GQ_OFF + HEADS * KP] * (GLA_DK ** -0.5)
    gk = z_scr[rows, GK_OFF:GK_OFF + HEADS * KP]
    gqe = gq * ea
    lane_head = lax.broadcasted_iota(jnp.int32, (SG, HEADS * KP), 1) // KP
    qk_prod = gq * gk
    gvs = [z_scr[rows, GV_OFF + h * HP:GV_OFF + h * HP + GLA_DV] for h in range(HEADS)]
    for pair in range(HEADS // 2):
        pl_ = slice(pair * LANES, (pair + 1) * LANES)
        for j in range(SG):
            qecol = _column_bcast(gqe[j:j + 1, pl_])
            kcol = _column_bcast(gk[j:j + 1, pl_])
            eacol = _column_bcast(ea[j:j + 1, pl_])
            for hh in range(2):
                h = 2 * pair + hh
                kr = slice(hh * KP, hh * KP + GLA_DK)
                s_old = s_ref[j, h]
                vrow = gvs[h][j:j + 1, :]
                og_scr[j:j + 1, h * HP:h * HP + GLA_DV] = jnp.sum(
                    qecol[kr, 0:GLA_DV] * s_old, axis=0, keepdims=True)
                s_o_ref[j, h] = eacol[kr, 0:GLA_DV] * s_old + kcol[kr, 0:GLA_DV] * vrow
    for h in range(HEADS):
        gv = gvs[h]
        qk = jnp.sum(jnp.where(lane_head == h, qk_prod, 0.0), axis=1, keepdims=True)
        o = og_scr[:, h * HP:h * HP + GLA_DV] + qk * gv
        ms = jnp.mean(o * o, axis=1, keepdims=True)
        on = o * lax.rsqrt(ms + LN_EPS) * gnorm_ref[:, h * HP:h * HP + GLA_DV]
        gg = z_scr[rows, GG_OFF + h * HP:GG_OFF + h * HP + GLA_DV]
        mix_scr[rows, YG_OFF + h * HP:YG_OFF + h * HP + GLA_DV] = gg * jax.nn.sigmoid(gg) * on

    @pl.when(gi == n_g - 1)
    def _():
        y = ALPHA * x_ref[...] + _dot(mix_scr[...].astype(BF16), w_out_ref[...])
        xo_ref[...] = _layer_norm_rows(y, lng_ref[...], lnb_ref[...])


def _sample_mixer(x, states, consts, carried, layer):
    B = x.shape[0]
    out_shape = (jax.ShapeDtypeStruct((B, D_MODEL), F32),) + tuple(
        jax.ShapeDtypeStruct(s.shape, F32) for s in states)
    state_specs = [
        pl.BlockSpec((None, SG, POOL_BUF, D_POOL), lambda g: (layer, g, 0, 0)),
        pl.BlockSpec((None, SG, HEADS, MLSTM_DH, MLSTM_DH), lambda g: (layer, g, 0, 0, 0)),
        pl.BlockSpec((None, SG, HEADS * MLSTM_DH), lambda g: (layer, g, 0)),
        pl.BlockSpec((None, SG, HEADS), lambda g: (layer, g, 0)),
        pl.BlockSpec((None, SG, HEADS, GLA_DK, GLA_DV), lambda g: (layer, g, 0, 0, 0)),
    ]
    x_spec = pl.BlockSpec((B, D_MODEL), lambda g: (0, 0))
    n_in = 1 + len(states) + len(consts)
    return pl.pallas_call(
        functools.partial(_sample_mixer_kernel, len(carried)),
        grid=(B // SG,),
        in_specs=[x_spec] + state_specs + [_layer_spec(cst.shape, layer) for cst in consts]
        + [pl.BlockSpec(memory_space=pl.ANY)] * len(carried),
        out_specs=[x_spec] + state_specs,
        out_shape=out_shape,
        input_output_aliases={n_in + i: 1 + i for i in range(len(carried))},
        scratch_shapes=[
            pltpu.VMEM((B, N_Z), F32),
            pltpu.VMEM((B, N_MIX), F32),
            pltpu.VMEM((SG, HEADS * HP), F32),
            pltpu.VMEM((SG, HEADS * HP), F32),
            pltpu.VMEM((SG, D_POOL), F32),
        ],
        compiler_params=pltpu.CompilerParams(
            dimension_semantics=("arbitrary",), vmem_limit_bytes=VMEM_LIMIT),
        name="sample_mixer",
    )(x, *states, *consts, *carried)


K_SPLIT = 2
CK = MLSTM_DH // K_SPLIT
SK = GLA_DK // K_SPLIT


def _row_of(a, idx):
    sub = lax.broadcasted_iota(jnp.int32, a.shape, 0)
    return jnp.sum(jnp.where(sub == idx, a, 0.0), axis=0, keepdims=True)


def _sample_step_kernel(n_carried, *refs):
    (x_ref, pool_ref, c_ref, n_ref, m_ref, s_ref,
     w_in_ref, w_out_ref, gbias_ref, a2t_ref, abias_ref, mnorm_ref, gnorm_ref,
     pmix_ref, pscale_ref, lng_ref, lnb_ref) = refs[:17]
    (xo_ref, pool_o_ref, c_o_ref, n_o_ref, m_o_ref, s_o_ref,
     zt_scr, mixt_scr, gate_scr, ea_scr, num_scr, oin_scr) = refs[17 + n_carried:]
    h = pl.program_id(0)
    kh = pl.program_id(1)
    B = x_ref.shape[0]
    first = jnp.logical_and(h == 0, kh == 0)
    last = jnp.logical_and(h == HEADS - 1, kh == K_SPLIT - 1)

    @pl.when(first)
    def _():
        mixt_scr[...] = jnp.zeros(mixt_scr.shape, F32)
        xb = x_ref[...].astype(BF16)
        for c0 in range(0, N_Z, LANES):
            zt_scr[c0:c0 + LANES, :] = _dot_nt(w_in_ref[c0:c0 + LANES, :], xb)
        g = zt_scr[GATE_OFF:GATE_OFF + LANES, :]
        gb = g + gbias_ref[...]
        log_i = gb[GATE_I:GATE_I + HEADS]
        log_f = _log_sigmoid(gb[GATE_F:GATE_F + HEADS])
        inter = log_f + m_ref[...]
        m_t = jnp.maximum(inter, log_i)
        m_o_ref[...] = m_t
        gate_scr[0:HEADS, :] = jnp.exp(inter - m_t)
        gate_scr[HEADS:2 * HEADS, :] = jnp.exp(log_i - m_t)
        gate_scr[2 * HEADS:3 * HEADS, :] = jnp.exp(-m_t)
        ya = _dot(a2t_ref[...], g.astype(BF16)) + abias_ref[...]
        ea_scr[...] = jnp.exp(_log_sigmoid(ya) * (1.0 / GLA_TAU))

    gates = gate_scr[...]
    dec = _row_of(gates, h)
    wig = _row_of(gates, h + HEADS)
    floor = _row_of(gates, h + 2 * HEADS)

    def head_rows(off, width, count, start=0):
        return zt_scr[pl.ds(pl.multiple_of(off + h * width + start, SUBLANES), count), :]

    k0 = kh * CK
    q_sl = head_rows(MQ_OFF, HP, CK, k0) * (MLSTM_DH ** -0.5)
    k_sl = head_rows(MK_OFF, HP, CK, k0) * wig
    v_t = head_rows(MV_OFF, HP, MLSTM_DH)
    acc = jnp.zeros((MLSTM_DH, B), F32)
    for k in range(CK):
        c_old = c_ref[k]
        acc = acc + q_sl[k:k + 1, :] * c_old
        c_o_ref[k] = dec * c_old + k_sl[k:k + 1, :] * v_t

    @pl.when(kh == 0)
    def _():
        num_scr[...] = acc

    @pl.when(kh > 0)
    def _():
        num_scr[...] += acc

    @pl.when(kh == K_SPLIT - 1)
    def _():
        q_t = head_rows(MQ_OFF, HP, MLSTM_DH) * (MLSTM_DH ** -0.5)
        k_t = head_rows(MK_OFF, HP, MLSTM_DH)
        n_old = n_ref[...]
        sc = jnp.sum(q_t * k_t, axis=0, keepdims=True) * wig
        num = dec * num_scr[...] + sc * v_t
        nq = dec * jnp.sum(q_t * n_old, axis=0, keepdims=True) + sc
        hval = num / jnp.maximum(jnp.abs(nq), floor)
        n_o_ref[...] = dec * n_old + wig * k_t
        mu = jnp.mean(hval, axis=0, keepdims=True)
        dlt = hval - mu
        var = jnp.mean(dlt * dlt, axis=0, keepdims=True)
        w_norm = mnorm_ref[pl.ds(pl.multiple_of(h * HP, SUBLANES), MLSTM_DH), :]
        mo = head_rows(MO_OFF, HP, MLSTM_DH)
        mixt_scr[pl.ds(pl.multiple_of(YM_OFF + h * HP, SUBLANES), MLSTM_DH), :] = (
            jax.nn.sigmoid(mo) * dlt * lax.rsqrt(var + LN_EPS) * w_norm)

    s0 = kh * SK
    ea_sl = ea_scr[pl.ds(pl.multiple_of(h * KP + s0, SUBLANES), SK), :]
    gq_sl = head_rows(GQ_OFF, KP, SK, s0) * (GLA_DK ** -0.5)
    gk_sl = head_rows(GK_OFF, KP, SK, s0)
    gv_t = head_rows(GV_OFF, HP, GLA_DV)
    gqe_sl = gq_sl * ea_sl
    oacc = jnp.zeros((GLA_DV, B), F32)
    for k in range(SK):
        s_old = s_ref[k]
        oacc = oacc + gqe_sl[k:k + 1, :] * s_old
        s_o_ref[k] = ea_sl[k:k + 1, :] * s_old + gk_sl[k:k + 1, :] * gv_t

    @pl.when(kh == 0)
    def _():
        oin_scr[...] = oacc

    @pl.when(kh > 0)
    def _():
        oin_scr[...] += oacc

    @pl.when(kh == K_SPLIT - 1)
    def _():
        gq_t = head_rows(GQ_OFF, KP, GLA_DK) * (GLA_DK ** -0.5)
        gk_t = head_rows(GK_OFF, KP, GLA_DK)
        qk = jnp.sum(gq_t * gk_t, axis=0, keepdims=True)
        o = oin_scr[...] + qk * gv_t
        ms = jnp.mean(o * o, axis=0, keepdims=True)
        w_norm = gnorm_ref[pl.ds(pl.multiple_of(h * HP, SUBLANES), GLA_DV), :]
        gg = head_rows(GG_OFF, HP, GLA_DV)
        mixt_scr[pl.ds(pl.multiple_of(YG_OFF + h * HP, SUBLANES), GLA_DV), :] = (
            gg * jax.nn.sigmoid(gg) * o * lax.rsqrt(ms + LN_EPS) * w_norm)

    @pl.when(last)
    def _():
        u = jnp.concatenate([jnp.transpose(zt_scr[c0:c0 + LANES, :]) for c0 in range(0, D_POOL, LANES)], axis=1)
        run = u
        sums = {}
        for j in range(1, max(POOL_WINDOWS)):
            run = run + pool_ref[POOL_BUF - j]
            if j + 1 in POOL_WINDOWS:
                sums[j + 1] = run
        grp = lax.broadcasted_iota(jnp.int32, (B, D_POOL), 1) // POOL_GROUP
        wmean = jnp.zeros((B, D_POOL), F32)
        for gidx, w in enumerate(POOL_WINDOWS):
            wmean = jnp.where(grp == gidx, sums[w] / float(min(w, PAST_LEN + 1)), wmean)
        y_pool = _dot((wmean - u).astype(BF16), pmix_ref[...]) * pscale_ref[...]
        for j in range(POOL_BUF - 1):
            pool_o_ref[j] = pool_ref[j + 1]
        pool_o_ref[POOL_BUF - 1] = u
        mix = jnp.concatenate(
            [y_pool] + [jnp.transpose(mixt_scr[c0:c0 + LANES, :]) for c0 in range(YM_OFF, N_MIX, LANES)], axis=1)
        y = ALPHA * x_ref[...] + _dot(mix.astype(BF16), w_out_ref[...])
        xo_ref[...] = _layer_norm_rows(y, lng_ref[...], lnb_ref[...])


def _sample_step(x, states, consts, carried, layer):
    B = x.shape[0]
    out_shape = (jax.ShapeDtypeStruct((B, D_MODEL), F32),) + tuple(
        jax.ShapeDtypeStruct(s.shape, F32) for s in states)
    state_specs = [
        pl.BlockSpec((None, POOL_BUF, B, D_POOL), lambda h, k: (layer, 0, 0, 0)),
        pl.BlockSpec((None, None, CK, MLSTM_DH, B), lambda h, k: (layer, h, k, 0, 0)),
        pl.BlockSpec((None, None, MLSTM_DH, B), lambda h, k: (layer, h, 0, 0)),
        pl.BlockSpec((None, HEADS, B), lambda h, k: (layer, 0, 0)),
        pl.BlockSpec((None, None, SK, GLA_DV, B), lambda h, k: (layer, h, k, 0, 0)),
    ]
    x_spec = pl.BlockSpec((B, D_MODEL), lambda h, k: (0, 0))
    n_in = 1 + len(states) + len(consts)
    return pl.pallas_call(
        functools.partial(_sample_step_kernel, len(carried)),
        grid=(HEADS, K_SPLIT),
        in_specs=[x_spec] + state_specs + [_layer_spec(cst.shape, layer) for cst in consts]
        + [pl.BlockSpec(memory_space=pl.ANY)] * len(carried),
        out_specs=[x_spec] + state_specs,
        out_shape=out_shape,
        input_output_aliases={n_in + i: 1 + i for i in range(len(carried))},
        scratch_shapes=[
            pltpu.VMEM((N_Z, B), F32),
            pltpu.VMEM((N_MIX, B), F32),
            pltpu.VMEM((2 * SUBLANES, B), F32),
            pltpu.VMEM((HEADS * KP, B), F32),
            pltpu.VMEM((MLSTM_DH, B), F32),
            pltpu.VMEM((GLA_DV, B), F32),
        ],
        compiler_params=pltpu.CompilerParams(
            dimension_semantics=("arbitrary", "arbitrary"), vmem_limit_bytes=VMEM_LIMIT),
        name="sample_step",
    )(x, *states, *consts, *carried)


def _ffn_kernel(x_ref, p_ref, wg_ref, wu_ref, wd_ref, wp_ref, wpg_ref, lng_ref, lnb_ref, o_ref):
    x = x_ref[...]
    xb = x.astype(BF16)
    acc = ALPHA * x + _dot(p_ref[...].astype(BF16), wp_ref[...]) * jax.nn.sigmoid(_dot(xb, wpg_ref[...]))
    for c0 in range(0, D_FF, FF_CHUNK):
        c1 = min(c0 + FF_CHUNK, D_FF)
        gate = _dot(xb, wg_ref[:, c0:c1])
        up = _dot(xb, wu_ref[:, c0:c1])
        hid = (gate * jax.nn.sigmoid(gate) * up).astype(BF16)
        acc = acc + _dot(hid, wd_ref[c0:c1, :])
    o_ref[...] = _layer_norm_rows(acc, lng_ref[...], lnb_ref[...])


def _ffn(x, p, consts, layer, tm):
    M = x.shape[0]
    return pl.pallas_call(
        _ffn_kernel,
        grid=(M // tm,),
        in_specs=[pl.BlockSpec((tm, D_MODEL), lambda i: (i, 0)),
                  pl.BlockSpec((None, tm, D_PLE), lambda i: (layer, i, 0))]
        + [_layer_spec(c.shape, layer) for c in consts],
        out_specs=pl.BlockSpec((tm, D_MODEL), lambda i: (i, 0)),
        out_shape=jax.ShapeDtypeStruct((M, D_MODEL), F32),
        compiler_params=pltpu.CompilerParams(
            dimension_semantics=("arbitrary",), vmem_limit_bytes=VMEM_LIMIT),
        name="ffn",
    )(x, p, *consts)


def kernel(x_prompt, x_sample, p_prompt, p_sample, state_pool, state_mlstm_C, state_mlstm_n, state_mlstm_m, state_gla_S, w_in, mlstm_i_bias, mlstm_f_bias, mlstm_norm_w, gla_a2, gla_a_bias, gla_norm_w, pool_mix, pool_scale, w_out, ln1_g, ln1_b, w_gate, w_up, w_down, w_ple, w_ple_gate, ln2_g, ln2_b):
    Bp, Tp, _ = x_prompt.shape
    Bs = x_sample.shape[0]

    w_in_p = _pad_in_proj_t(w_in).astype(BF16)
    w_out_p = _pad_out_proj(w_out).astype(BF16)
    gbias = jnp.zeros((DEPTH, 1, LANES), F32)
    gbias = gbias.at[:, 0, GATE_I:GATE_I + HEADS].set(mlstm_i_bias.astype(F32))
    gbias = gbias.at[:, 0, GATE_F:GATE_F + HEADS].set(mlstm_f_bias.astype(F32))
    a2_p = jnp.zeros((DEPTH, LANES, HEADS * KP), F32)
    a2_p = a2_p.at[:, GATE_A:GATE_A + GLA_RANK, :].set(_pad_heads(gla_a2.astype(F32), GLA_DK, KP)).astype(BF16)
    abias_p = _pad_heads(gla_a_bias.astype(F32), GLA_DK, KP)[:, None, :]
    mnorm_p = _pad_heads(mlstm_norm_w.astype(F32), MLSTM_DH, HP)[:, None, :]
    gnorm_p = _pad_heads(gla_norm_w.astype(F32), GLA_DV, HP)[:, None, :]
    n_grp = len(POOL_WINDOWS)
    eye = jnp.eye(n_grp, dtype=F32)
    pmix_bd = (pool_mix.astype(F32)[:, :, :, None, :] * eye[None, :, None, :, None]).reshape(
        DEPTH, D_POOL, D_POOL).astype(BF16)
    pscale = pool_scale.astype(F32)[:, None, :]
    row = lambda a: a.astype(F32)[:, None, :]
    ln1_g_r, ln1_b_r, ln2_g_r, ln2_b_r = row(ln1_g), row(ln1_b), row(ln2_g), row(ln2_b)
    wg_b, wu_b, wd_b = w_gate.astype(BF16), w_up.astype(BF16), w_down.astype(BF16)
    wp_b, wpg_b = w_ple.astype(BF16), w_ple_gate.astype(BF16)

    mnorm_t = jnp.broadcast_to(jnp.transpose(mnorm_p, (0, 2, 1)), (DEPTH, HEADS * HP, LANES))
    mixer_consts = (w_in_p, w_out_p, gbias, a2_p, abias_p, mnorm_t, gnorm_p, pmix_bd, pscale, ln1_g_r, ln1_b_r)
    ffn_consts = (wg_b, wu_b, wd_b, wp_b, wpg_b, ln2_g_r, ln2_b_r)
    xp = x_prompt
    xs = x_sample.reshape(Bs, D_MODEL)
    pp = p_prompt.reshape(DEPTH, Bp * Tp, D_PLE)
    ps = p_sample.reshape(DEPTH, Bs, D_PLE)
    sample_states = (jnp.transpose(state_pool, (0, 2, 1, 3)), jnp.transpose(state_mlstm_C, (0, 2, 3, 4, 1)),
                     jnp.transpose(state_mlstm_n, (0, 2, 3, 1)), jnp.transpose(state_mlstm_m, (0, 2, 1)),
                     jnp.transpose(state_gla_S, (0, 2, 3, 4, 1)))
    col = lambda a: jnp.transpose(a, (0, 2, 1))
    sample_consts = (w_in_p, w_out_p, col(gbias), col(a2_p), col(abias_p), col(mnorm_p), col(gnorm_p),
                     pmix_bd, pscale, ln1_g_r, ln1_b_r)
    outs_p = []
    carried = ()
    for l in range(DEPTH):
        xp, pool_p, c_p, n_p, m_p, s_p = _prompt_mixer(xp, mixer_consts, l)
        xp = _ffn(xp.reshape(Bp * Tp, D_MODEL), pp, ffn_consts, l, tm=512).reshape(Bp, Tp, D_MODEL)
        outs_p.append((pool_p, c_p, n_p[:, :, :MLSTM_DH], m_p[:, 0, GATE_F:GATE_F + HEADS], s_p))
        xs, *carried = _sample_step(xs, sample_states, sample_consts, tuple(carried), l)
        xs = _ffn(xs, ps, ffn_consts, l, tm=Bs)

    pool_s, c_s, n_s, m_s, s_s = carried
    stack = lambda j: jnp.stack([o[j] for o in outs_p], axis=0)
    return ((xp, xs.reshape(Bs, 1, D_MODEL)) + tuple(stack(j) for j in range(5))
            + (jnp.transpose(pool_s, (0, 2, 1, 3)), jnp.transpose(c_s, (0, 4, 1, 2, 3)),
               jnp.transpose(n_s, (0, 3, 1, 2)), jnp.transpose(m_s, (0, 2, 1)),
               jnp.transpose(s_s, (0, 4, 1, 2, 3))))
```

```python
import functools

import numpy as np
import jax
import jax.numpy as jnp
from jax import lax
from jax.experimental import pallas as pl
from jax.experimental.pallas import tpu as pltpu

F32 = jnp.float32
BF16 = jnp.bfloat16

D_MODEL = 1024
DEPTH = 4
PAST_LEN = 16384
D_POOL = 256
POOL_WINDOWS = (2, 4, 8, 16)
POOL_GROUP = 64
POOL_BUF = 15
D_MLSTM = 384
HEADS = 4
MLSTM_DH = 96
D_GLA = 384
GLA_DV = 96
GLA_DK = 48
GLA_RANK = 16
GLA_TAU = 16.0
D_MIX = 1024
D_FF = 2816
D_PLE = 256
CHUNK = 64
ALPHA = (2 * DEPTH) ** 0.25
LN_EPS = 1e-5

LANES = 128
SUBLANES = 8
VMEM_LIMIT = 56 * 1024 * 1024

HP = LANES
KP = 64
U_OFF = 0
MQ_OFF = U_OFF + D_POOL
MK_OFF = MQ_OFF + HEADS * HP
MV_OFF = MK_OFF + HEADS * HP
MO_OFF = MV_OFF + HEADS * HP
GQ_OFF = MO_OFF + HEADS * HP
GK_OFF = GQ_OFF + HEADS * KP
GV_OFF = GK_OFF + HEADS * KP
GG_OFF = GV_OFF + HEADS * HP
GATE_OFF = GG_OFF + HEADS * HP
N_Z = GATE_OFF + LANES
GATE_I = 0
GATE_F = 4
GATE_A = 8
YM_OFF = D_POOL
YG_OFF = YM_OFF + HEADS * HP
N_MIX = YG_OFF + HEADS * HP

TT = 512
PAIR = 2 * CHUNK
ZT_Q = 0
ZT_V = ZT_Q + HEADS * HP
ZT_O = ZT_V + HEADS * HP
ZT_ROWS = ZT_O + HEADS * HP
FF_CHUNK = 512


def _pad_heads(a, width, padded):
    lead = a.shape[:-1]
    a = a.reshape(lead + (HEADS, width))
    a = jnp.pad(a, [(0, 0)] * len(lead) + [(0, 0), (0, padded - width)])
    return a.reshape(lead + (HEADS * padded,))


def _pad_head_rows(a, width, padded):
    L, _, K = a.shape
    a = jnp.pad(a.reshape(L, HEADS, width, K), ((0, 0), (0, 0), (0, padded - width), (0, 0)))
    return a.reshape(L, HEADS * padded, K)


def _pad_in_proj_t(w_in):
    sizes = (D_POOL, D_MLSTM, D_MLSTM, D_MLSTM, HEADS, HEADS, D_MLSTM,
             HEADS * GLA_DK, HEADS * GLA_DK, D_GLA, GLA_RANK, D_GLA)
    w_t = jnp.transpose(w_in, (0, 2, 1))
    u, mq, mk, mv, mi, mf, mo, gq, gk, gv, ga, gg = jnp.split(w_t, np.cumsum(sizes)[:-1].tolist(), axis=1)
    gate_pad = jnp.zeros((w_t.shape[0], LANES - GATE_A - GLA_RANK, w_t.shape[2]), w_t.dtype)
    wide = lambda a: _pad_head_rows(a, MLSTM_DH, HP)
    narrow = lambda a: _pad_head_rows(a, GLA_DK, KP)
    return jnp.concatenate([u, wide(mq), wide(mk), wide(mv), wide(mo), narrow(gq), narrow(gk), wide(gv),
                            wide(gg), mi, mf, ga, gate_pad], axis=1)


def _pad_out_proj(w_out):
    pool = w_out[:, :D_POOL]
    heads = w_out[:, D_POOL:].reshape(w_out.shape[0], 2 * HEADS, MLSTM_DH, D_MODEL)
    heads = jnp.pad(heads, ((0, 0), (0, 0), (0, HP - MLSTM_DH), (0, 0)))
    return jnp.concatenate([pool, heads.reshape(w_out.shape[0], 2 * HEADS * HP, D_MODEL)], axis=1)


def _dot(a, b):
    return jnp.dot(a, b, preferred_element_type=F32)


def _dot_nt(a, b):
    return lax.dot_general(a, b, (((1,), (1,)), ((), ())), preferred_element_type=F32)


def _dot_tn(a, b):
    return lax.dot_general(a, b, (((0,), (0,)), ((), ())), preferred_element_type=F32)


def _log_sigmoid(x):
    return jnp.minimum(x, 0.0) - jnp.log1p(jnp.exp(-jnp.abs(x)))


def _cummax_rows(x, segment=None):
    n = x.shape[0] if segment is None else segment
    rid = lax.broadcasted_iota(jnp.int32, x.shape, 0) & (n - 1)
    sh = 1
    while sh < n:
        x = jnp.maximum(x, jnp.where(rid >= sh, pltpu.roll(x, sh, axis=0), -jnp.inf))
        sh *= 2
    return x


def _cumsum_rows(x, segment=None):
    n = x.shape[0] if segment is None else segment
    rid = lax.broadcasted_iota(jnp.int32, x.shape, 0) & (n - 1)
    sh = 1
    while sh < n:
        x = x + jnp.where(rid >= sh, pltpu.roll(x, sh, axis=0), 0.0)
        sh *= 2
    return x


def _layer_norm_rows(y, g, b):
    mu = jnp.mean(y, axis=-1, keepdims=True)
    d = y - mu
    var = jnp.mean(d * d, axis=-1, keepdims=True)
    return d * lax.rsqrt(var + LN_EPS) * g + b


def _pool_select(lane, a, b):
    return jnp.where((lane & (LANES - 1)) < POOL_GROUP, a, b)


def _prompt_mixer_kernel(x_ref, w_in_ref, w_out_ref, gbias_ref, a2_ref, abias_ref, mnorm_t_ref,
                         gnorm_ref, pmix_ref, pscale_ref, lng_ref, lnb_ref,
                         xo_ref, pool_o_ref, c_o_ref, n_o_ref, m_o_ref, s_o_ref,
                         z_scr, zt_scr, mix_scr, mixt_scr, ext_scr, c_scr, m_scr, s_scr, lvl_scr):
    ti = pl.program_id(1)
    n_t = pl.num_programs(1)

    @pl.when(ti == 0)
    def _():
        ext_scr[0:2 * SUBLANES, :] = jnp.zeros((2 * SUBLANES, D_POOL), F32)
        c_scr[...] = jnp.zeros(c_scr.shape, F32)
        m_scr[...] = jnp.zeros(m_scr.shape, F32)
        s_scr[...] = jnp.zeros(s_scr.shape, F32)

    xb = x_ref[0].astype(BF16)
    for lo, hi in ((U_OFF, MQ_OFF), (MK_OFF, MV_OFF), (GQ_OFF, N_Z)):
        for c0 in range(lo, hi, 512):
            c1 = min(c0 + 512, hi)
            z_scr[:, c0:c1] = _dot_nt(xb, w_in_ref[c0:c1, :])
    for src, dst in ((MQ_OFF, ZT_Q), (MV_OFF, ZT_V), (MO_OFF, ZT_O)):
        zt_scr[dst:dst + HEADS * HP, :] = _dot_nt(w_in_ref[src:src + HEADS * HP, :], xb)

    hist = 2 * SUBLANES
    u = z_scr[:, U_OFF:U_OFF + D_POOL]
    ext_scr[hist:hist + TT, :] = u
    ext = ext_scr[...]
    e0 = ext[:, 0:LANES]
    e1 = ext[:, LANES:2 * LANES]
    a2s = e0 + pltpu.roll(e0, 1, axis=0)
    a4s = a2s + pltpu.roll(a2s, 2, axis=0)
    b2s = e1 + pltpu.roll(e1, 1, axis=0)
    b4s = b2s + pltpu.roll(b2s, 2, axis=0)
    b8s = b4s + pltpu.roll(b4s, 4, axis=0)
    b16s = b8s + pltpu.roll(b8s, 8, axis=0)
    lane = lax.broadcasted_iota(jnp.int32, (TT, LANES), 1)
    pos1 = (lax.broadcasted_iota(jnp.int32, (TT, LANES), 0) + ti * TT + 1).astype(F32)
    ws0 = _pool_select(lane, a2s[hist:], a4s[hist:])
    ws1 = _pool_select(lane, b8s[hist:], b16s[hist:])
    cnt0 = jnp.minimum(_pool_select(lane, 2.0, 4.0), pos1)
    cnt1 = jnp.minimum(_pool_select(lane, 8.0, 16.0), pos1)
    d0 = ws0 / cnt0 - u[:, 0:LANES]
    d1 = ws1 / cnt1 - u[:, LANES:]
    dpool = jnp.concatenate([d0, d1], axis=1).astype(BF16)
    y_pool = _dot(dpool, pmix_ref[...]) * pscale_ref[...]
    mix_scr[:, 0:D_POOL] = y_pool.astype(BF16)
    tail = ext_scr[TT:TT + hist, :]
    ext_scr[0:hist, :] = tail

    @pl.when(ti == n_t - 1)
    def _():
        pool_o_ref[0] = ext_scr[1:hist, :]

    gbias = gbias_ref[...]
    abias = abias_ref[...]
    trow = lax.broadcasted_iota(jnp.int32, (HEADS * CHUNK, CHUNK), 0) & (CHUNK - 1)
    scol = lax.broadcasted_iota(jnp.int32, (HEADS * CHUNK, CHUNK), 1)
    differ = trow ^ scol
    level = jnp.where(scol > trow, -2, -1)
    for bit in range(CHUNK.bit_length() - 1):
        level = level + jnp.where((scol < trow) & (differ >= (1 << bit)), 1, 0)
    lvl_scr[...] = level

    def chunk_body(c):
        rows = pl.ds(c * CHUNK, CHUNK)
        graw = z_scr[rows, GATE_OFF:GATE_OFF + LANES]

        ya = _dot(graw.astype(BF16), a2_ref[...]) + abias
        la = _log_sigmoid(ya) * (1.0 / GLA_TAU)
        bc = _cumsum_rows(la)
        gq = z_scr[rows, GQ_OFF:GQ_OFF + HEADS * KP] * (GLA_DK ** -0.5)
        gk = z_scr[rows, GK_OFF:GK_OFF + HEADS * KP]
        width = HEADS * KP
        t_id = lax.broadcasted_iota(jnp.int32, (CHUNK, width), 0)
        lane_head = lax.broadcasted_iota(jnp.int32, (CHUNK, width), 1) // KP

        def stack_heads(a):
            return jnp.concatenate([jnp.where(lane_head == h, a, jnp.zeros_like(a)) for h in range(HEADS)], axis=0)

        amat = jnp.where(lvl_scr[...] == -1, _dot_nt(stack_heads(gq.astype(BF16)), gk.astype(BF16)), 0.0)
        half = CHUNK // 2
        while half >= 1:
            blk = 2 * half
            upper = (t_id & half) != 0
            if blk >= SUBLANES:
                ref_rows = jnp.concatenate(
                    [jnp.broadcast_to(bc[b0 + half - 1:b0 + half, :], (blk, width))
                     for b0 in range(0, CHUNK, blk)], axis=0)
            else:
                off = t_id & (blk - 1)
                ref_rows = bc
                for o in range(blk):
                    sh = (o - (half - 1)) % CHUNK
                    if sh != 0:
                        ref_rows = jnp.where(off == o, pltpu.roll(bc, sh, axis=0), ref_rows)
            dlt = bc - ref_rows
            fac = (jnp.where(upper, gq, gk) * jnp.exp(jnp.where(upper, dlt, -dlt))).astype(BF16)
            amat = jnp.where(lvl_scr[...] == half.bit_length() - 1, _dot_nt(stack_heads(fac), fac), amat)
            half //= 2

        o_inter = _dot(stack_heads((gq * jnp.exp(bc)).astype(BF16)), s_scr[...].astype(BF16))
        bc_t = jnp.transpose(bc)
        gk_t = jnp.transpose(gk)
        last_t = bc_t[:, CHUNK - 1:CHUNK]
        ke_t = (gk_t * jnp.exp(last_t - bc_t)).astype(BF16)
        s_dec = jnp.exp(last_t)
        for h in range(HEADS):
            hr = slice(h * CHUNK, (h + 1) * CHUNK)
            gv = z_scr[rows, GV_OFF + h * HP:GV_OFF + (h + 1) * HP].astype(BF16)
            o = o_inter[hr] + _dot(amat[hr].astype(BF16), gv)
            ms = jnp.sum(o * o, axis=1, keepdims=True) * (1.0 / GLA_DV)
            on = o * lax.rsqrt(ms + LN_EPS) * gnorm_ref[:, h * HP:(h + 1) * HP]
            gg = z_scr[rows, GG_OFF + h * HP:GG_OFF + (h + 1) * HP]
            mix_scr[rows, YG_OFF + h * HP:YG_OFF + (h + 1) * HP] = (gg * jax.nn.sigmoid(gg) * on).astype(BF16)
            kr = slice(h * KP, (h + 1) * KP)
            s_scr[kr, :] = s_dec[kr] * s_scr[kr, :] + _dot(ke_t[kr], gv)

    first_r = lax.broadcasted_iota(jnp.int32, (PAIR, LANES), 0) < CHUNK
    src_id = lax.broadcasted_iota(jnp.int32, (PAIR, PAIR), 0)
    tok_id = lax.broadcasted_iota(jnp.int32, (PAIR, PAIR), 1)
    pair_mask = ((src_id // CHUNK) == (tok_id // CHUNK)) & (src_id <= tok_id)
    first = tok_id < CHUNK
    valid = src_id < MLSTM_DH

    def pair_body(p):
        chunk_body(2 * p)
        chunk_body(2 * p + 1)
        lanes = pl.ds(p * PAIR, PAIR)
        rows = pl.ds(p * PAIR, PAIR)
        g_r = z_scr[rows, GATE_OFF:GATE_OFF + LANES] + gbias
        b = _cumsum_rows(_log_sigmoid(g_r), segment=CHUNK)
        i_sh = pltpu.roll(g_r, GATE_F - GATE_I, axis=1)
        r_r = i_sh - b
        bcm = b + _cummax_rows(r_r, segment=CHUNK)
        m_in = m_scr[0:1, :]
        b_mid = b[CHUNK - 1:CHUNK, :]
        b_end = b[PAIR - 1:PAIR, :]
        m_mid = jnp.maximum(b_mid + m_in, bcm[CHUNK - 1:CHUNK, :])
        m_in2 = jnp.where(first_r, m_in, m_mid)
        inter = b + m_in2
        m_t = jnp.maximum(inter, bcm)
        m_end = m_t[PAIR - 1:PAIR, :]
        m_scr[0:1, :] = m_end
        m_out2 = jnp.where(first_r, m_mid, m_end)
        b_last2 = jnp.where(first_r, b_mid, b_end)
        a = jnp.transpose(b - m_t)
        w_inter = jnp.transpose(jnp.exp(inter - m_t))
        floor = jnp.transpose(jnp.exp(-m_t))
        w_end = jnp.transpose(jnp.exp(b_last2 - b + i_sh - m_out2))
        dec_a = jnp.exp(b_mid + m_in - m_mid)
        dec_b = jnp.exp(b_end + m_mid - m_end)
        for h in range(HEADS):
            gl = GATE_F + h
            hr = slice(h * HP, (h + 1) * HP)
            q_t = (zt_scr[ZT_Q + h * HP:ZT_Q + (h + 1) * HP, lanes] * (MLSTM_DH ** -0.5)).astype(BF16)
            v_aug = jnp.where(src_id == MLSTM_DH, 1.0,
                              zt_scr[ZT_V + h * HP:ZT_V + (h + 1) * HP, lanes])
            k_r = z_scr[rows, MK_OFF + h * HP:MK_OFF + (h + 1) * HP].astype(BF16)
            e_t = jnp.exp(jnp.where(pair_mask, a[gl:gl + 1, :] + r_r[:, gl:gl + 1], -jnp.inf))
            s_t = (_dot(k_r, q_t) * e_t).astype(BF16)
            vw = v_aug * w_end[gl:gl + 1, :]
            ct_a = c_scr[h]
            ct_b = dec_a[:, gl:gl + 1] * ct_a + _dot(jnp.where(first, vw, 0.0).astype(BF16), k_r)
            c_scr[h] = dec_b[:, gl:gl + 1] * ct_b + _dot(jnp.where(first, 0.0, vw).astype(BF16), k_r)
            qc = jnp.where(first, _dot(ct_a.astype(BF16), q_t), _dot(ct_b.astype(BF16), q_t))
            num = w_inter[gl:gl + 1, :] * qc + _dot(v_aug.astype(BF16), s_t)
            hval = num / jnp.maximum(jnp.abs(num[MLSTM_DH:MLSTM_DH + 1, :]), floor[gl:gl + 1, :])
            mu = jnp.sum(jnp.where(valid, hval, 0.0), axis=0, keepdims=True) * (1.0 / MLSTM_DH)
            dlt = jnp.where(valid, hval - mu, 0.0)
            var = jnp.sum(dlt * dlt, axis=0, keepdims=True) * (1.0 / MLSTM_DH)
            mo_t = zt_scr[ZT_O + h * HP:ZT_O + (h + 1) * HP, lanes]
            mixt_scr[hr, lanes] = (jax.nn.sigmoid(mo_t) * dlt * lax.rsqrt(var + LN_EPS)
                                   * mnorm_t_ref[hr, :]).astype(BF16)

    for p in range(TT // PAIR):
        pair_body(p)

    y = (ALPHA * x_ref[0] + _dot(mix_scr[:, 0:YM_OFF], w_out_ref[0:YM_OFF, :])
         + _dot_tn(mixt_scr[...], w_out_ref[YM_OFF:YG_OFF, :])
         + _dot(mix_scr[:, YG_OFF:N_MIX], w_out_ref[YG_OFF:N_MIX, :]))
    xo_ref[0] = _layer_norm_rows(y, lng_ref[...], lnb_ref[...])

    @pl.when(ti == n_t - 1)
    def _():
        for h in range(HEADS):
            ct = c_scr[h]
            c_o_ref[0, h] = jnp.transpose(ct)[0:MLSTM_DH, 0:MLSTM_DH]
            n_o_ref[0, h:h + 1, :] = ct[MLSTM_DH:MLSTM_DH + 1, :]
            s_o_ref[0, h] = s_scr[h * KP:h * KP + GLA_DK, 0:GLA_DV]
        m_o_ref[0] = m_scr[...]


def _layer_spec(shape, layer):
    nd = len(shape)
    return pl.BlockSpec((None,) + tuple(shape[1:]), lambda *_: (layer,) + (0,) * (nd - 1),
                        pipeline_mode=pl.Buffered(1))


def _prompt_mixer(x, consts, layer):
    B, T, _ = x.shape
    out_shape = (
        jax.ShapeDtypeStruct((B, T, D_MODEL), F32),
        jax.ShapeDtypeStruct((B, POOL_BUF, D_POOL), F32),
        jax.ShapeDtypeStruct((B, HEADS, MLSTM_DH, MLSTM_DH), F32),
        jax.ShapeDtypeStruct((B, HEADS, HP), F32),
        jax.ShapeDtypeStruct((B, SUBLANES, LANES), F32),
        jax.ShapeDtypeStruct((B, HEADS, GLA_DK, GLA_DV), F32),
    )
    out_specs = (
        pl.BlockSpec((1, TT, D_MODEL), lambda b, t: (b, t, 0)),
        pl.BlockSpec((1, POOL_BUF, D_POOL), lambda b, t: (b, 0, 0)),
        pl.BlockSpec((1, HEADS, MLSTM_DH, MLSTM_DH), lambda b, t: (b, 0, 0, 0)),
        pl.BlockSpec((1, HEADS, HP), lambda b, t: (b, 0, 0)),
        pl.BlockSpec((1, SUBLANES, LANES), lambda b, t: (b, 0, 0)),
        pl.BlockSpec((1, HEADS, GLA_DK, GLA_DV), lambda b, t: (b, 0, 0, 0)),
    )
    return pl.pallas_call(
        _prompt_mixer_kernel,
        grid=(B, T // TT),
        in_specs=[pl.BlockSpec((1, TT, D_MODEL), lambda b, t: (b, t, 0))]
        + [_layer_spec(c.shape, layer) for c in consts],
        out_specs=out_specs,
        out_shape=out_shape,
        scratch_shapes=[
            pltpu.VMEM((TT, N_Z), F32),
            pltpu.VMEM((ZT_ROWS, TT), F32),
            pltpu.VMEM((TT, N_MIX), BF16),
            pltpu.VMEM((HEADS * HP, TT), BF16),
            pltpu.VMEM((TT + 2 * SUBLANES, D_POOL), F32),
            pltpu.VMEM((HEADS, HP, HP), F32),
            pltpu.VMEM((SUBLANES, LANES), F32),
            pltpu.VMEM((HEADS * KP, HP), F32),
            pltpu.VMEM((HEADS * CHUNK, CHUNK), jnp.int32),
        ],
        compiler_params=pltpu.CompilerParams(
            dimension_semantics=("arbitrary", "arbitrary"), vmem_limit_bytes=VMEM_LIMIT),
        name="prompt_mixer",
    )(x, *consts)


K_SPLIT = 2
CK = MLSTM_DH // K_SPLIT
SK = GLA_DK // K_SPLIT


def _row_of(a, idx):
    sub = lax.broadcasted_iota(jnp.int32, a.shape, 0)
    return jnp.sum(jnp.where(sub == idx, a, 0.0), axis=0, keepdims=True)


def _sample_step_kernel(n_carried, *refs):
    (x_ref, pool_ref, c_ref, n_ref, m_ref, s_ref,
     w_in_ref, w_out_ref, gbias_ref, a2t_ref, abias_ref, mnorm_ref, gnorm_ref,
     pmix_ref, pscale_ref, lng_ref, lnb_ref) = refs[:17]
    (xo_ref, pool_o_ref, c_o_ref, n_o_ref, m_o_ref, s_o_ref,
     zt_scr, mixt_scr, gate_scr, ea_scr, num_scr, oin_scr) = refs[17 + n_carried:]
    h = pl.program_id(0)
    kh = pl.program_id(1)
    B = x_ref.shape[0]
    first = jnp.logical_and(h == 0, kh == 0)
    last = jnp.logical_and(h == HEADS - 1, kh == K_SPLIT - 1)

    @pl.when(first)
    def _():
        mixt_scr[...] = jnp.zeros(mixt_scr.shape, F32)
        xb = x_ref[...].astype(BF16)
        for c0 in range(0, N_Z, LANES):
            zt_scr[c0:c0 + LANES, :] = _dot_nt(w_in_ref[c0:c0 + LANES, :], xb)
        g = zt_scr[GATE_OFF:GATE_OFF + LANES, :]
        gb = g + gbias_ref[...]
        log_i = gb[GATE_I:GATE_I + HEADS]
        log_f = _log_sigmoid(gb[GATE_F:GATE_F + HEADS])
        inter = log_f + m_ref[...]
        m_t = jnp.maximum(inter, log_i)
        m_o_ref[...] = m_t
        gate_scr[0:HEADS, :] = jnp.exp(inter - m_t)
        gate_scr[HEADS:2 * HEADS, :] = jnp.exp(log_i - m_t)
        gate_scr[2 * HEADS:3 * HEADS, :] = jnp.exp(-m_t)
        ya = _dot(a2t_ref[...], g.astype(BF16)) + abias_ref[...]
        ea_scr[...] = jnp.exp(_log_sigmoid(ya) * (1.0 / GLA_TAU))

    gates = gate_scr[...]
    dec = _row_of(gates, h)
    wig = _row_of(gates, h + HEADS)
    floor = _row_of(gates, h + 2 * HEADS)

    def head_rows(off, width, count, start=0):
        return zt_scr[pl.ds(pl.multiple_of(off + h * width + start, SUBLANES), count), :]

    k0 = kh * CK
    q_sl = head_rows(MQ_OFF, HP, CK, k0) * (MLSTM_DH ** -0.5)
    k_sl = head_rows(MK_OFF, HP, CK, k0) * wig
    v_t = head_rows(MV_OFF, HP, MLSTM_DH)
    acc = jnp.zeros((MLSTM_DH, B), F32)
    for k in range(CK):
        c_old = c_ref[k]
        acc = acc + q_sl[k:k + 1, :] * c_old
        c_o_ref[k] = dec * c_old + k_sl[k:k + 1, :] * v_t

    @pl.when(kh == 0)
    def _():
        num_scr[...] = acc

    @pl.when(kh > 0)
    def _():
        num_scr[...] += acc

    @pl.when(kh == K_SPLIT - 1)
    def _():
        q_t = head_rows(MQ_OFF, HP, MLSTM_DH) * (MLSTM_DH ** -0.5)
        k_t = head_rows(MK_OFF, HP, MLSTM_DH)
        n_old = n_ref[...]
        sc = jnp.sum(q_t * k_t, axis=0, keepdims=True) * wig
        num = dec * num_scr[...] + sc * v_t
        nq = dec * jnp.sum(q_t * n_old, axis=0, keepdims=True) + sc
        hval = num / jnp.maximum(jnp.abs(nq), floor)
        n_o_ref[...] = dec * n_old + wig * k_t
        mu = jnp.mean(hval, axis=0, keepdims=True)
        dlt = hval - mu
        var = jnp.mean(dlt * dlt, axis=0, keepdims=True)
        w_norm = mnorm_ref[pl.ds(pl.multiple_of(h * HP, SUBLANES), MLSTM_DH), :]
        mo = head_rows(MO_OFF, HP, MLSTM_DH)
        mixt_scr[pl.ds(pl.multiple_of(YM_OFF + h * HP, SUBLANES), MLSTM_DH), :] = (
            jax.nn.sigmoid(mo) * dlt * lax.rsqrt(var + LN_EPS) * w_norm)

    s0 = kh * SK
    ea_sl = ea_scr[pl.ds(pl.multiple_of(h * KP + s0, SUBLANES), SK), :]
    gq_sl = head_rows(GQ_OFF, KP, SK, s0) * (GLA_DK ** -0.5)
    gk_sl = head_rows(GK_OFF, KP, SK, s0)
    gv_t = head_rows(GV_OFF, HP, GLA_DV)
    gqe_sl = gq_sl * ea_sl
    oacc = jnp.zeros((GLA_DV, B), F32)
    for k in range(SK):
        s_old = s_ref[k]
        oacc = oacc + gqe_sl[k:k + 1, :] * s_old
        s_o_ref[k] = ea_sl[k:k + 1, :] * s_old + gk_sl[k:k + 1, :] * gv_t

    @pl.when(kh == 0)
    def _():
        oin_scr[...] = oacc

    @pl.when(kh > 0)
    def _():
        oin_scr[...] += oacc

    @pl.when(kh == K_SPLIT - 1)
    def _():
        gq_t = head_rows(GQ_OFF, KP, GLA_DK) * (GLA_DK ** -0.5)
        gk_t = head_rows(GK_OFF, KP, GLA_DK)
        qk = jnp.sum(gq_t * gk_t, axis=0, keepdims=True)
        o = oin_scr[...] + qk * gv_t
        ms = jnp.mean(o * o, axis=0, keepdims=True)
        w_norm = gnorm_ref[pl.ds(pl.multiple_of(h * HP, SUBLANES), GLA_DV), :]
        gg = head_rows(GG_OFF, HP, GLA_DV)
        mixt_scr[pl.ds(pl.multiple_of(YG_OFF + h * HP, SUBLANES), GLA_DV), :] = (
            gg * jax.nn.sigmoid(gg) * o * lax.rsqrt(ms + LN_EPS) * w_norm)

    @pl.when(last)
    def _():
        u = jnp.concatenate([jnp.transpose(zt_scr[c0:c0 + LANES, :]) for c0 in range(0, D_POOL, LANES)], axis=1)
        run = u
        sums = {}
        for j in range(1, max(POOL_WINDOWS)):
            run = run + pool_ref[POOL_BUF - j]
            if j + 1 in POOL_WINDOWS:
                sums[j + 1] = run
        grp = lax.broadcasted_iota(jnp.int32, (B, D_POOL), 1) // POOL_GROUP
        wmean = jnp.zeros((B, D_POOL), F32)
        for gidx, w in enumerate(POOL_WINDOWS):
            wmean = jnp.where(grp == gidx, sums[w] / float(min(w, PAST_LEN + 1)), wmean)
        y_pool = _dot((wmean - u).astype(BF16), pmix_ref[...]) * pscale_ref[...]
        for j in range(POOL_BUF - 1):
            pool_o_ref[j] = pool_ref[j + 1]
        pool_o_ref[POOL_BUF - 1] = u
        mix = jnp.concatenate(
            [y_pool] + [jnp.transpose(mixt_scr[c0:c0 + LANES, :]) for c0 in range(YM_OFF, N_MIX, LANES)], axis=1)
        y = ALPHA * x_ref[...] + _dot(mix.astype(BF16), w_out_ref[...])
        xo_ref[...] = _layer_norm_rows(y, lng_ref[...], lnb_ref[...])


def _sample_step(x, states, consts, carried, layer):
    B = x.shape[0]
    out_shape = (jax.ShapeDtypeStruct((B, D_MODEL), F32),) + tuple(
        jax.ShapeDtypeStruct(s.shape, F32) for s in states)
    state_specs = [
        pl.BlockSpec((None, POOL_BUF, B, D_POOL), lambda h, k: (layer, 0, 0, 0)),
        pl.BlockSpec((None, None, CK, MLSTM_DH, B), lambda h, k: (layer, h, k, 0, 0)),
        pl.BlockSpec((None, None, MLSTM_DH, B), lambda h, k: (layer, h, 0, 0)),
        pl.BlockSpec((None, HEADS, B), lambda h, k: (layer, 0, 0)),
        pl.BlockSpec((None, None, SK, GLA_DV, B), lambda h, k: (layer, h, k, 0, 0)),
    ]
    x_spec = pl.BlockSpec((B, D_MODEL), lambda h, k: (0, 0))
    n_in = 1 + len(states) + len(consts)
    return pl.pallas_call(
        functools.partial(_sample_step_kernel, len(carried)),
        grid=(HEADS, K_SPLIT),
        in_specs=[x_spec] + state_specs + [_layer_spec(cst.shape, layer) for cst in consts]
        + [pl.BlockSpec(memory_space=pl.ANY)] * len(carried),
        out_specs=[x_spec] + state_specs,
        out_shape=out_shape,
        input_output_aliases={n_in + i: 1 + i for i in range(len(carried))},
        scratch_shapes=[
            pltpu.VMEM((N_Z, B), F32),
            pltpu.VMEM((N_MIX, B), F32),
            pltpu.VMEM((2 * SUBLANES, B), F32),
            pltpu.VMEM((HEADS * KP, B), F32),
            pltpu.VMEM((MLSTM_DH, B), F32),
            pltpu.VMEM((GLA_DV, B), F32),
        ],
        compiler_params=pltpu.CompilerParams(
            dimension_semantics=("arbitrary", "arbitrary"), vmem_limit_bytes=VMEM_LIMIT),
        name="sample_step",
    )(x, *states, *consts, *carried)


def _ffn_kernel(x_ref, p_ref, wg_ref, wu_ref, wd_ref, wp_ref, wpg_ref, lng_ref, lnb_ref, o_ref):
    x = x_ref[...]
    xb = x.astype(BF16)
    acc = ALPHA * x + _dot(p_ref[...].astype(BF16), wp_ref[...]) * jax.nn.sigmoid(_dot(xb, wpg_ref[...]))
    for c0 in range(0, D_FF, FF_CHUNK):
        c1 = min(c0 + FF_CHUNK, D_FF)
        gate = _dot(xb, wg_ref[:, c0:c1])
        up = _dot(xb, wu_ref[:, c0:c1])
        hid = (gate * jax.nn.sigmoid(gate) * up).astype(BF16)
        acc = acc + _dot(hid, wd_ref[c0:c1, :])
    o_ref[...] = _layer_norm_rows(acc, lng_ref[...], lnb_ref[...])


def _ffn(x, p, consts, layer, tm):
    M = x.shape[0]
    return pl.pallas_call(
        _ffn_kernel,
        grid=(M // tm,),
        in_specs=[pl.BlockSpec((tm, D_MODEL), lambda i: (i, 0)),
                  pl.BlockSpec((None, tm, D_PLE), lambda i: (layer, i, 0))]
        + [_layer_spec(c.shape, layer) for c in consts],
        out_specs=pl.BlockSpec((tm, D_MODEL), lambda i: (i, 0)),
        out_shape=jax.ShapeDtypeStruct((M, D_MODEL), F32),
        compiler_params=pltpu.CompilerParams(
            dimension_semantics=("arbitrary",), vmem_limit_bytes=VMEM_LIMIT),
        name="ffn",
    )(x, p, *consts)


def kernel(x_prompt, x_sample, p_prompt, p_sample, state_pool, state_mlstm_C, state_mlstm_n, state_mlstm_m, state_gla_S, w_in, mlstm_i_bias, mlstm_f_bias, mlstm_norm_w, gla_a2, gla_a_bias, gla_norm_w, pool_mix, pool_scale, w_out, ln1_g, ln1_b, w_gate, w_up, w_down, w_ple, w_ple_gate, ln2_g, ln2_b):
    Bp, Tp, _ = x_prompt.shape
    Bs = x_sample.shape[0]

    w_in_p = _pad_in_proj_t(w_in).astype(BF16)
    w_out_p = _pad_out_proj(w_out).astype(BF16)
    gbias = jnp.zeros((DEPTH, 1, LANES), F32)
    gbias = gbias.at[:, 0, GATE_I:GATE_I + HEADS].set(mlstm_i_bias.astype(F32))
    gbias = gbias.at[:, 0, GATE_F:GATE_F + HEADS].set(mlstm_f_bias.astype(F32))
    a2_p = jnp.zeros((DEPTH, LANES, HEADS * KP), F32)
    a2_p = a2_p.at[:, GATE_A:GATE_A + GLA_RANK, :].set(_pad_heads(gla_a2.astype(F32), GLA_DK, KP)).astype(BF16)
    abias_p = _pad_heads(gla_a_bias.astype(F32), GLA_DK, KP)[:, None, :]
    mnorm_p = _pad_heads(mlstm_norm_w.astype(F32), MLSTM_DH, HP)[:, None, :]
    gnorm_p = _pad_heads(gla_norm_w.astype(F32), GLA_DV, HP)[:, None, :]
    n_grp = len(POOL_WINDOWS)
    eye = jnp.eye(n_grp, dtype=F32)
    pmix_bd = (pool_mix.astype(F32)[:, :, :, None, :] * eye[None, :, None, :, None]).reshape(
        DEPTH, D_POOL, D_POOL).astype(BF16)
    pscale = pool_scale.astype(F32)[:, None, :]
    row = lambda a: a.astype(F32)[:, None, :]
    ln1_g_r, ln1_b_r, ln2_g_r, ln2_b_r = row(ln1_g), row(ln1_b), row(ln2_g), row(ln2_b)
    wg_b, wu_b, wd_b = w_gate.astype(BF16), w_up.astype(BF16), w_down.astype(BF16)
    wp_b, wpg_b = w_ple.astype(BF16), w_ple_gate.astype(BF16)

    mnorm_t = jnp.broadcast_to(jnp.transpose(mnorm_p, (0, 2, 1)), (DEPTH, HEADS * HP, LANES))
    mixer_consts = (w_in_p, w_out_p, gbias, a2_p, abias_p, mnorm_t, gnorm_p, pmix_bd, pscale, ln1_g_r, ln1_b_r)
    ffn_consts = (wg_b, wu_b, wd_b, wp_b, wpg_b, ln2_g_r, ln2_b_r)
    xp = x_prompt
    xs = x_sample.reshape(Bs, D_MODEL)
    pp = p_prompt.reshape(DEPTH, Bp * Tp, D_PLE)
    ps = p_sample.reshape(DEPTH, Bs, D_PLE)
    sample_states = (jnp.transpose(state_pool, (0, 2, 1, 3)), jnp.transpose(state_mlstm_C, (0, 2, 3, 4, 1)),
                     jnp.transpose(state_mlstm_n, (0, 2, 3, 1)), jnp.transpose(state_mlstm_m, (0, 2, 1)),
                     jnp.transpose(state_gla_S, (0, 2, 3, 4, 1)))
    col = lambda a: jnp.transpose(a, (0, 2, 1))
    sample_consts = (w_in_p, w_out_p, col(gbias), col(a2_p), col(abias_p), col(mnorm_p), col(gnorm_p),
                     pmix_bd, pscale, ln1_g_r, ln1_b_r)
    outs_p = []
    carried = ()
    for l in range(DEPTH):
        xp, pool_p, c_p, n_p, m_p, s_p = _prompt_mixer(xp, mixer_consts, l)
        xp = _ffn(xp.reshape(Bp * Tp, D_MODEL), pp, ffn_consts, l, tm=512).reshape(Bp, Tp, D_MODEL)
        outs_p.append((pool_p, c_p, n_p[:, :, :MLSTM_DH], m_p[:, 0, GATE_F:GATE_F + HEADS], s_p))
        xs, *carried = _sample_step(xs, sample_states, sample_consts, tuple(carried), l)
        xs = _ffn(xs, ps, ffn_consts, l, tm=Bs)

    pool_s, c_s, n_s, m_s, s_s = carried
    stack = lambda j: jnp.stack([o[j] for o in outs_p], axis=0)
    return ((xp, xs.reshape(Bs, 1, D_MODEL)) + tuple(stack(j) for j in range(5))
            + (jnp.transpose(pool_s, (0, 2, 1, 3)), jnp.transpose(c_s, (0, 4, 1, 2, 3)),
               jnp.transpose(n_s, (0, 3, 1, 2)), jnp.transpose(m_s, (0, 2, 1)),
               jnp.transpose(s_s, (0, 4, 1, 2, 3))))
```

```python
import functools

import numpy as np
import jax
import jax.numpy as jnp
from jax import lax
from jax.experimental import pallas as pl
from jax.experimental.pallas import tpu as pltpu

F32 = jnp.float32
BF16 = jnp.bfloat16

D_MODEL = 1024
DEPTH = 4
PAST_LEN = 16384
D_POOL = 256
POOL_WINDOWS = (2, 4, 8, 16)
POOL_GROUP = 64
POOL_BUF = 15
D_MLSTM = 384
HEADS = 4
MLSTM_DH = 96
D_GLA = 384
GLA_DV = 96
GLA_DK = 48
GLA_RANK = 16
GLA_TAU = 16.0
D_MIX = 1024
D_FF = 2816
D_PLE = 256
CHUNK = 64
ALPHA = (2 * DEPTH) ** 0.25
LN_EPS = 1e-5

LANES = 128
SUBLANES = 8
VMEM_LIMIT = 56 * 1024 * 1024

HP = LANES
FH = MLSTM_DH
KP = 64
U_OFF = 0
MQ_OFF = U_OFF + D_POOL
MK_OFF = MQ_OFF + HEADS * FH
MV_OFF = MK_OFF + HEADS * HP
MO_OFF = MV_OFF + HEADS * FH
GQ_OFF = MO_OFF + HEADS * FH
GK_OFF = GQ_OFF + HEADS * KP
GV_OFF = GK_OFF + HEADS * KP
GG_OFF = GV_OFF + HEADS * HP
GATE_OFF = GG_OFF + HEADS * HP
N_Z = GATE_OFF + LANES
GATE_I = 0
GATE_F = 4
GATE_A = 8
YM_OFF = D_POOL
YG_OFF = YM_OFF + HEADS * FH
N_MIX = YG_OFF + HEADS * HP

TT = 512
PAIR = 2 * CHUNK
ZT_Q = 0
ZT_V = ZT_Q + HEADS * FH
ZT_O = ZT_V + HEADS * FH
ZT_ROWS = ZT_O + HEADS * FH
FF_CHUNK = 512


def _pad_heads(a, width, padded):
    lead = a.shape[:-1]
    a = a.reshape(lead + (HEADS, width))
    a = jnp.pad(a, [(0, 0)] * len(lead) + [(0, 0), (0, padded - width)])
    return a.reshape(lead + (HEADS * padded,))


def _pad_head_rows(a, width, padded):
    L, _, K = a.shape
    a = jnp.pad(a.reshape(L, HEADS, width, K), ((0, 0), (0, 0), (0, padded - width), (0, 0)))
    return a.reshape(L, HEADS * padded, K)


def _pad_in_proj_t(w_in):
    sizes = (D_POOL, D_MLSTM, D_MLSTM, D_MLSTM, HEADS, HEADS, D_MLSTM,
             HEADS * GLA_DK, HEADS * GLA_DK, D_GLA, GLA_RANK, D_GLA)
    w_t = jnp.transpose(w_in, (0, 2, 1))
    u, mq, mk, mv, mi, mf, mo, gq, gk, gv, ga, gg = jnp.split(w_t, np.cumsum(sizes)[:-1].tolist(), axis=1)
    gate_pad = jnp.zeros((w_t.shape[0], LANES - GATE_A - GLA_RANK, w_t.shape[2]), w_t.dtype)
    wide = lambda a: _pad_head_rows(a, MLSTM_DH, HP)
    narrow = lambda a: _pad_head_rows(a, GLA_DK, KP)
    return jnp.concatenate([u, mq, wide(mk), mv, mo, narrow(gq), narrow(gk), wide(gv),
                            wide(gg), mi, mf, ga, gate_pad], axis=1)


def _pad_out_proj(w_out):
    dense = w_out[:, :D_POOL + D_MLSTM]
    heads = w_out[:, D_POOL + D_MLSTM:].reshape(w_out.shape[0], HEADS, GLA_DV, D_MODEL)
    heads = jnp.pad(heads, ((0, 0), (0, 0), (0, HP - GLA_DV), (0, 0)))
    return jnp.concatenate([dense, heads.reshape(w_out.shape[0], HEADS * HP, D_MODEL)], axis=1)


def _dot(a, b):
    return jnp.dot(a, b, preferred_element_type=F32)


def _dot_nt(a, b):
    return lax.dot_general(a, b, (((1,), (1,)), ((), ())), preferred_element_type=F32)


def _dot_tn(a, b):
    return lax.dot_general(a, b, (((0,), (0,)), ((), ())), preferred_element_type=F32)


def _log_sigmoid(x):
    return jnp.minimum(x, 0.0) - jnp.log1p(jnp.exp(-jnp.abs(x)))


def _cummax_rows(x, segment=None):
    n = x.shape[0] if segment is None else segment
    rid = lax.broadcasted_iota(jnp.int32, x.shape, 0) & (n - 1)
    sh = 1
    while sh < n:
        x = jnp.maximum(x, jnp.where(rid >= sh, pltpu.roll(x, sh, axis=0), -jnp.inf))
        sh *= 2
    return x


def _cumsum_rows(x, segment=None):
    n = x.shape[0] if segment is None else segment
    rid = lax.broadcasted_iota(jnp.int32, x.shape, 0) & (n - 1)
    sh = 1
    while sh < n:
        x = x + jnp.where(rid >= sh, pltpu.roll(x, sh, axis=0), 0.0)
        sh *= 2
    return x


def _layer_norm_rows(y, g, b):
    mu = jnp.mean(y, axis=-1, keepdims=True)
    d = y - mu
    var = jnp.mean(d * d, axis=-1, keepdims=True)
    return d * lax.rsqrt(var + LN_EPS) * g + b


def _pool_select(lane, a, b):
    return jnp.where((lane & (LANES - 1)) < POOL_GROUP, a, b)


def _prompt_mixer_kernel(x_ref, w_in_ref, w_out_ref, gbias_ref, a2_ref, abias_ref, mnorm_t_ref,
                         gnorm_ref, pmix_ref, pscale_ref, lng_ref, lnb_ref,
                         xo_ref, pool_o_ref, c_o_ref, n_o_ref, m_o_ref, s_o_ref,
                         z_scr, zt_scr, mix_scr, mixt_scr, ext_scr, c_scr, m_scr, s_scr, lvl_scr, tr_scr):
    ti = pl.program_id(1)
    n_t = pl.num_programs(1)

    @pl.when(ti == 0)
    def _():
        ext_scr[0:2 * SUBLANES, :] = jnp.zeros((2 * SUBLANES, D_POOL), F32)
        c_scr[...] = jnp.zeros(c_scr.shape, F32)
        m_scr[...] = jnp.zeros(m_scr.shape, F32)
        s_scr[...] = jnp.zeros(s_scr.shape, F32)

    xb = x_ref[0].astype(BF16)
    for lo, hi in ((U_OFF, MQ_OFF), (MK_OFF, MV_OFF), (GQ_OFF, N_Z)):
        for c0 in range(lo, hi, 512):
            c1 = min(c0 + 512, hi)
            z_scr[:, c0:c1] = _dot_nt(xb, w_in_ref[c0:c1, :])
    for src, dst in ((MQ_OFF, ZT_Q), (MV_OFF, ZT_V), (MO_OFF, ZT_O)):
        zt_scr[dst:dst + HEADS * FH, :] = _dot_nt(w_in_ref[src:src + HEADS * FH, :], xb)

    hist = 2 * SUBLANES
    u = z_scr[:, U_OFF:U_OFF + D_POOL]
    ext_scr[hist:hist + TT, :] = u
    ext = ext_scr[...]
    e0 = ext[:, 0:LANES]
    e1 = ext[:, LANES:2 * LANES]
    a2s = e0 + pltpu.roll(e0, 1, axis=0)
    a4s = a2s + pltpu.roll(a2s, 2, axis=0)
    b2s = e1 + pltpu.roll(e1, 1, axis=0)
    b4s = b2s + pltpu.roll(b2s, 2, axis=0)
    b8s = b4s + pltpu.roll(b4s, 4, axis=0)
    b16s = b8s + pltpu.roll(b8s, 8, axis=0)
    lane = lax.broadcasted_iota(jnp.int32, (TT, LANES), 1)
    pos1 = (lax.broadcasted_iota(jnp.int32, (TT, LANES), 0) + ti * TT + 1).astype(F32)
    ws0 = _pool_select(lane, a2s[hist:], a4s[hist:])
    ws1 = _pool_select(lane, b8s[hist:], b16s[hist:])
    cnt0 = jnp.minimum(_pool_select(lane, 2.0, 4.0), pos1)
    cnt1 = jnp.minimum(_pool_select(lane, 8.0, 16.0), pos1)
    d0 = ws0 / cnt0 - u[:, 0:LANES]
    d1 = ws1 / cnt1 - u[:, LANES:]
    dpool = jnp.concatenate([d0, d1], axis=1).astype(BF16)
    y_pool = _dot(dpool, pmix_ref[...]) * pscale_ref[...]
    mix_scr[:, 0:D_POOL] = y_pool.astype(BF16)
    tail = ext_scr[TT:TT + hist, :]
    ext_scr[0:hist, :] = tail

    @pl.when(ti == n_t - 1)
    def _():
        pool_o_ref[0] = ext_scr[1:hist, :]

    gbias = gbias_ref[...]
    abias = abias_ref[...]
    trow = lax.broadcasted_iota(jnp.int32, (HEADS * CHUNK, CHUNK), 0) & (CHUNK - 1)
    scol = lax.broadcasted_iota(jnp.int32, (HEADS * CHUNK, CHUNK), 1)
    differ = trow ^ scol
    level = jnp.where(scol > trow, -2, -1)
    for bit in range(CHUNK.bit_length() - 1):
        level = level + jnp.where((scol < trow) & (differ >= (1 << bit)), 1, 0)
    lvl_scr[...] = level

    def chunk_body(c):
        rows = pl.ds(c * CHUNK, CHUNK)
        graw = z_scr[rows, GATE_OFF:GATE_OFF + LANES]

        ya = _dot(graw.astype(BF16), a2_ref[...]) + abias
        la = _log_sigmoid(ya) * (1.0 / GLA_TAU)
        bc = _cumsum_rows(la)
        gq = z_scr[rows, GQ_OFF:GQ_OFF + HEADS * KP] * (GLA_DK ** -0.5)
        gk = z_scr[rows, GK_OFF:GK_OFF + HEADS * KP]
        width = HEADS * KP
        t_id = lax.broadcasted_iota(jnp.int32, (CHUNK, width), 0)
        lane_head = lax.broadcasted_iota(jnp.int32, (CHUNK, width), 1) // KP

        def stack_heads(a):
            return jnp.concatenate([jnp.where(lane_head == h, a, jnp.zeros_like(a)) for h in range(HEADS)], axis=0)

        amat = jnp.where(lvl_scr[...] == -1, _dot_nt(stack_heads(gq.astype(BF16)), gk.astype(BF16)), 0.0)
        half = CHUNK // 2
        while half >= 1:
            blk = 2 * half
            upper = (t_id & half) != 0
            if blk >= SUBLANES:
                ref_rows = jnp.concatenate(
                    [jnp.broadcast_to(bc[b0 + half - 1:b0 + half, :], (blk, width))
                     for b0 in range(0, CHUNK, blk)], axis=0)
            else:
                off = t_id & (blk - 1)
                ref_rows = bc
                for o in range(blk):
                    sh = (o - (half - 1)) % CHUNK
                    if sh != 0:
                        ref_rows = jnp.where(off == o, pltpu.roll(bc, sh, axis=0), ref_rows)
            dlt = bc - ref_rows
            fac = (jnp.where(upper, gq, gk) * jnp.exp(jnp.where(upper, dlt, -dlt))).astype(BF16)
            amat = jnp.where(lvl_scr[...] == half.bit_length() - 1, _dot_nt(stack_heads(fac), fac), amat)
            half //= 2

        o_inter = _dot(stack_heads((gq * jnp.exp(bc)).astype(BF16)), s_scr[...].astype(BF16))
        bc_t = jnp.transpose(bc)
        gk_t = jnp.transpose(gk)
        last_t = bc_t[:, CHUNK - 1:CHUNK]
        ke_t = (gk_t * jnp.exp(last_t - bc_t)).astype(BF16)
        s_dec = jnp.exp(last_t)
        for h in range(HEADS):
            hr = slice(h * CHUNK, (h + 1) * CHUNK)
            gv = z_scr[rows, GV_OFF + h * HP:GV_OFF + (h + 1) * HP].astype(BF16)
            o = o_inter[hr] + _dot(amat[hr].astype(BF16), gv)
            ms = jnp.sum(o * o, axis=1, keepdims=True) * (1.0 / GLA_DV)
            on = o * lax.rsqrt(ms + LN_EPS) * gnorm_ref[:, h * HP:(h + 1) * HP]
            gg = z_scr[rows, GG_OFF + h * HP:GG_OFF + (h + 1) * HP]
            mix_scr[rows, YG_OFF + h * HP:YG_OFF + (h + 1) * HP] = (gg * jax.nn.sigmoid(gg) * on).astype(BF16)
            kr = slice(h * KP, (h + 1) * KP)
            s_scr[kr, :] = s_dec[kr] * s_scr[kr, :] + _dot(ke_t[kr], gv)

    first_r = lax.broadcasted_iota(jnp.int32, (PAIR, LANES), 0) < CHUNK
    src_id = lax.broadcasted_iota(jnp.int32, (PAIR, PAIR), 0)
    tok_id = lax.broadcasted_iota(jnp.int32, (PAIR, PAIR), 1)
    pair_mask = ((src_id // CHUNK) == (tok_id // CHUNK)) & (src_id <= tok_id)
    first_v = lax.broadcasted_iota(jnp.int32, (FH + SUBLANES, PAIR), 1) < CHUNK
    ones_rows = jnp.where(lax.broadcasted_iota(jnp.int32, (SUBLANES, PAIR), 0) == 0, 1.0, 0.0)

    def pair_body(p):
        chunk_body(2 * p)
        chunk_body(2 * p + 1)
        lanes = pl.ds(p * PAIR, PAIR)
        rows = pl.ds(p * PAIR, PAIR)
        g_r = z_scr[rows, GATE_OFF:GATE_OFF + LANES] + gbias
        b = _cumsum_rows(_log_sigmoid(g_r), segment=CHUNK)
        i_sh = pltpu.roll(g_r, GATE_F - GATE_I, axis=1)
        r_r = i_sh - b
        bcm = b + _cummax_rows(r_r, segment=CHUNK)
        m_in = m_scr[0:1, :]
        b_mid = b[CHUNK - 1:CHUNK, :]
        b_end = b[PAIR - 1:PAIR, :]
        m_mid = jnp.maximum(b_mid + m_in, bcm[CHUNK - 1:CHUNK, :])
        m_in2 = jnp.where(first_r, m_in, m_mid)
        inter = b + m_in2
        m_t = jnp.maximum(inter, bcm)
        m_end = m_t[PAIR - 1:PAIR, :]
        m_scr[0:1, :] = m_end
        m_out2 = jnp.where(first_r, m_mid, m_end)
        b_last2 = jnp.where(first_r, b_mid, b_end)
        a = jnp.transpose(b - m_t)
        w_inter = jnp.transpose(jnp.exp(inter - m_t))
        floor = jnp.transpose(jnp.exp(-m_t))
        w_end = jnp.transpose(jnp.exp(b_last2 - b + i_sh - m_out2))
        dec_a = jnp.exp(b_mid + m_in - m_mid)
        dec_b = jnp.exp(b_end + m_mid - m_end)
        for h in range(HEADS):
            gl = GATE_F + h
            fr = slice(h * FH, (h + 1) * FH)
            q_t = (zt_scr[ZT_Q + h * FH:ZT_Q + (h + 1) * FH, lanes] * (MLSTM_DH ** -0.5)).astype(BF16)
            v_aug = jnp.concatenate([zt_scr[ZT_V + h * FH:ZT_V + (h + 1) * FH, lanes], ones_rows], axis=0)
            k_r = z_scr[rows, MK_OFF + h * HP:MK_OFF + h * HP + FH].astype(BF16)
            e_t = jnp.exp(jnp.where(pair_mask, a[gl:gl + 1, :] + r_r[:, gl:gl + 1], -jnp.inf))
            s_t = (_dot(k_r, q_t) * e_t).astype(BF16)
            vw = v_aug * w_end[gl:gl + 1, :]
            ct_a = c_scr[h]
            ct_b = dec_a[:, gl:gl + 1] * ct_a + _dot(jnp.where(first_v, vw, 0.0).astype(BF16), k_r)
            c_scr[h] = dec_b[:, gl:gl + 1] * ct_b + _dot(jnp.where(first_v, 0.0, vw).astype(BF16), k_r)
            qc = jnp.where(first_v, _dot(ct_a.astype(BF16), q_t), _dot(ct_b.astype(BF16), q_t))
            num = w_inter[gl:gl + 1, :] * qc + _dot(v_aug.astype(BF16), s_t)
            hval = num[0:FH] / jnp.maximum(jnp.abs(num[FH:FH + 1, :]), floor[gl:gl + 1, :])
            mu = jnp.mean(hval, axis=0, keepdims=True)
            dlt = hval - mu
            var = jnp.mean(dlt * dlt, axis=0, keepdims=True)
            mo_t = zt_scr[ZT_O + h * FH:ZT_O + (h + 1) * FH, lanes]
            mixt_scr[fr, lanes] = (jax.nn.sigmoid(mo_t) * dlt * lax.rsqrt(var + LN_EPS)
                                   * mnorm_t_ref[fr, :]).astype(BF16)

    for p in range(TT // PAIR):
        pair_body(p)

    y = (ALPHA * x_ref[0] + _dot(mix_scr[:, 0:YM_OFF], w_out_ref[0:YM_OFF, :])
         + _dot_tn(mixt_scr[...], w_out_ref[YM_OFF:YG_OFF, :])
         + _dot(mix_scr[:, YG_OFF:N_MIX], w_out_ref[YG_OFF:N_MIX, :]))
    xo_ref[0] = _layer_norm_rows(y, lng_ref[...], lnb_ref[...])

    @pl.when(ti == n_t - 1)
    def _():
        for h in range(HEADS):
            ct = c_scr[h]
            tr_scr[...] = jnp.zeros(tr_scr.shape, F32)
            tr_scr[0:FH, 0:FH] = ct[0:FH, :]
            c_o_ref[0, h] = jnp.transpose(tr_scr[...])[0:FH, 0:FH]
            n_o_ref[0, h:h + 1, :] = ct[FH:FH + 1, :]
            s_o_ref[0, h] = s_scr[h * KP:h * KP + GLA_DK, 0:GLA_DV]
        m_o_ref[0] = m_scr[...]


def _layer_spec(shape, layer):
    nd = len(shape)
    return pl.BlockSpec((None,) + tuple(shape[1:]), lambda *_: (layer,) + (0,) * (nd - 1),
                        pipeline_mode=pl.Buffered(1))


def _prompt_mixer(x, consts, layer):
    B, T, _ = x.shape
    out_shape = (
        jax.ShapeDtypeStruct((B, T, D_MODEL), F32),
        jax.ShapeDtypeStruct((B, POOL_BUF, D_POOL), F32),
        jax.ShapeDtypeStruct((B, HEADS, MLSTM_DH, MLSTM_DH), F32),
        jax.ShapeDtypeStruct((B, HEADS, FH), F32),
        jax.ShapeDtypeStruct((B, SUBLANES, LANES), F32),
        jax.ShapeDtypeStruct((B, HEADS, GLA_DK, GLA_DV), F32),
    )
    out_specs = (
        pl.BlockSpec((1, TT, D_MODEL), lambda b, t: (b, t, 0)),
        pl.BlockSpec((1, POOL_BUF, D_POOL), lambda b, t: (b, 0, 0)),
        pl.BlockSpec((1, HEADS, MLSTM_DH, MLSTM_DH), lambda b, t: (b, 0, 0, 0)),
        pl.BlockSpec((1, HEADS, FH), lambda b, t: (b, 0, 0)),
        pl.BlockSpec((1, SUBLANES, LANES), lambda b, t: (b, 0, 0)),
        pl.BlockSpec((1, HEADS, GLA_DK, GLA_DV), lambda b, t: (b, 0, 0, 0)),
    )
    return pl.pallas_call(
        _prompt_mixer_kernel,
        grid=(B, T // TT),
        in_specs=[pl.BlockSpec((1, TT, D_MODEL), lambda b, t: (b, t, 0))]
        + [_layer_spec(c.shape, layer) for c in consts],
        out_specs=out_specs,
        out_shape=out_shape,
        scratch_shapes=[
            pltpu.VMEM((TT, N_Z), F32),
            pltpu.VMEM((ZT_ROWS, TT), F32),
            pltpu.VMEM((TT, N_MIX), BF16),
            pltpu.VMEM((HEADS * FH, TT), BF16),
            pltpu.VMEM((TT + 2 * SUBLANES, D_POOL), F32),
            pltpu.VMEM((HEADS, FH + SUBLANES, FH), F32),
            pltpu.VMEM((SUBLANES, LANES), F32),
            pltpu.VMEM((HEADS * KP, HP), F32),
            pltpu.VMEM((HEADS * CHUNK, CHUNK), jnp.int32),
            pltpu.VMEM((LANES, LANES), F32),
        ],
        compiler_params=pltpu.CompilerParams(
            dimension_semantics=("arbitrary", "arbitrary"), vmem_limit_bytes=VMEM_LIMIT),
        name="prompt_mixer",
    )(x, *consts)


K_SPLIT = 2
CK = MLSTM_DH // K_SPLIT
SK = GLA_DK // K_SPLIT


def _row_of(a, idx):
    sub = lax.broadcasted_iota(jnp.int32, a.shape, 0)
    return jnp.sum(jnp.where(sub == idx, a, 0.0), axis=0, keepdims=True)


def _sample_step_kernel(n_carried, *refs):
    (x_ref, pool_ref, c_ref, n_ref, m_ref, s_ref,
     w_in_ref, w_out_ref, gbias_ref, a2t_ref, abias_ref, mnorm_ref, gnorm_ref,
     pmix_ref, pscale_ref, lng_ref, lnb_ref) = refs[:17]
    (xo_ref, pool_o_ref, c_o_ref, n_o_ref, m_o_ref, s_o_ref,
     zt_scr, mixt_scr, gate_scr, ea_scr, num_scr, oin_scr) = refs[17 + n_carried:]
    h = pl.program_id(0)
    kh = pl.program_id(1)
    B = x_ref.shape[0]
    first = jnp.logical_and(h == 0, kh == 0)
    last = jnp.logical_and(h == HEADS - 1, kh == K_SPLIT - 1)

    @pl.when(first)
    def _():
        mixt_scr[...] = jnp.zeros(mixt_scr.shape, F32)
        xb = x_ref[...].astype(BF16)
        for c0 in range(0, N_Z, LANES):
            zt_scr[c0:c0 + LANES, :] = _dot_nt(w_in_ref[c0:c0 + LANES, :], xb)
        g = zt_scr[GATE_OFF:GATE_OFF + LANES, :]
        gb = g + gbias_ref[...]
        log_i = gb[GATE_I:GATE_I + HEADS]
        log_f = _log_sigmoid(gb[GATE_F:GATE_F + HEADS])
        inter = log_f + m_ref[...]
        m_t = jnp.maximum(inter, log_i)
        m_o_ref[...] = m_t
        gate_scr[0:HEADS, :] = jnp.exp(inter - m_t)
        gate_scr[HEADS:2 * HEADS, :] = jnp.exp(log_i - m_t)
        gate_scr[2 * HEADS:3 * HEADS, :] = jnp.exp(-m_t)
        ya = _dot(a2t_ref[...], g.astype(BF16)) + abias_ref[...]
        ea_scr[...] = jnp.exp(_log_sigmoid(ya) * (1.0 / GLA_TAU))

    gates = gate_scr[...]
    dec = _row_of(gates, h)
    wig = _row_of(gates, h + HEADS)
    floor = _row_of(gates, h + 2 * HEADS)

    def head_rows(off, width, count, start=0):
        return zt_scr[pl.ds(pl.multiple_of(off + h * width + start, SUBLANES), count), :]

    k0 = kh * CK
    q_sl = head_rows(MQ_OFF, FH, CK, k0) * (MLSTM_DH ** -0.5)
    k_sl = head_rows(MK_OFF, HP, CK, k0) * wig
    v_t = head_rows(MV_OFF, FH, MLSTM_DH)
    acc = jnp.zeros((MLSTM_DH, B), F32)
    for k in range(CK):
        c_old = c_ref[k]
        acc = acc + q_sl[k:k + 1, :] * c_old
        c_o_ref[k] = dec * c_old + k_sl[k:k + 1, :] * v_t

    @pl.when(kh == 0)
    def _():
        num_scr[...] = acc

    @pl.when(kh > 0)
    def _():
        num_scr[...] += acc

    @pl.when(kh == K_SPLIT - 1)
    def _():
        q_t = head_rows(MQ_OFF, FH, MLSTM_DH) * (MLSTM_DH ** -0.5)
        k_t = head_rows(MK_OFF, HP, MLSTM_DH)
        n_old = n_ref[...]
        sc = jnp.sum(q_t * k_t, axis=0, keepdims=True) * wig
        num = dec * num_scr[...] + sc * v_t
        nq = dec * jnp.sum(q_t * n_old, axis=0, keepdims=True) + sc
        hval = num / jnp.maximum(jnp.abs(nq), floor)
        n_o_ref[...] = dec * n_old + wig * k_t
        mu = jnp.mean(hval, axis=0, keepdims=True)
        dlt = hval - mu
        var = jnp.mean(dlt * dlt, axis=0, keepdims=True)
        w_norm = mnorm_ref[pl.ds(pl.multiple_of(h * FH, SUBLANES), MLSTM_DH), :]
        mo = head_rows(MO_OFF, FH, MLSTM_DH)
        mixt_scr[pl.ds(pl.multiple_of(YM_OFF + h * FH, SUBLANES), MLSTM_DH), :] = (
            jax.nn.sigmoid(mo) * dlt * lax.rsqrt(var + LN_EPS) * w_norm)

    s0 = kh * SK
    ea_sl = ea_scr[pl.ds(pl.multiple_of(h * KP + s0, SUBLANES), SK), :]
    gq_sl = head_rows(GQ_OFF, KP, SK, s0) * (GLA_DK ** -0.5)
    gk_sl = head_rows(GK_OFF, KP, SK, s0)
    gv_t = head_rows(GV_OFF, HP, GLA_DV)
    gqe_sl = gq_sl * ea_sl
    oacc = jnp.zeros((GLA_DV, B), F32)
    for k in range(SK):
        s_old = s_ref[k]
        oacc = oacc + gqe_sl[k:k + 1, :] * s_old
        s_o_ref[k] = ea_sl[k:k + 1, :] * s_old + gk_sl[k:k + 1, :] * gv_t

    @pl.when(kh == 0)
    def _():
        oin_scr[...] = oacc

    @pl.when(kh > 0)
    def _():
        oin_scr[...] += oacc

    @pl.when(kh == K_SPLIT - 1)
    def _():
        gq_t = head_rows(GQ_OFF, KP, GLA_DK) * (GLA_DK ** -0.5)
        gk_t = head_rows(GK_OFF, KP, GLA_DK)
        qk = jnp.sum(gq_t * gk_t, axis=0, keepdims=True)
        o = oin_scr[...] + qk * gv_t
        ms = jnp.mean(o * o, axis=0, keepdims=True)
        w_norm = gnorm_ref[pl.ds(pl.multiple_of(h * HP, SUBLANES), GLA_DV), :]
        gg = head_rows(GG_OFF, HP, GLA_DV)
        mixt_scr[pl.ds(pl.multiple_of(YG_OFF + h * HP, SUBLANES), GLA_DV), :] = (
            gg * jax.nn.sigmoid(gg) * o * lax.rsqrt(ms + LN_EPS) * w_norm)

    @pl.when(last)
    def _():
        u = jnp.concatenate([jnp.transpose(zt_scr[c0:c0 + LANES, :]) for c0 in range(0, D_POOL, LANES)], axis=1)
        run = u
        sums = {}
        for j in range(1, max(POOL_WINDOWS)):
            run = run + pool_ref[POOL_BUF - j]
            if j + 1 in POOL_WINDOWS:
                sums[j + 1] = run
        grp = lax.broadcasted_iota(jnp.int32, (B, D_POOL), 1) // POOL_GROUP
        wmean = jnp.zeros((B, D_POOL), F32)
        for gidx, w in enumerate(POOL_WINDOWS):
            wmean = jnp.where(grp == gidx, sums[w] / float(min(w, PAST_LEN + 1)), wmean)
        y_pool = _dot((wmean - u).astype(BF16), pmix_ref[...]) * pscale_ref[...]
        for j in range(POOL_BUF - 1):
            pool_o_ref[j] = pool_ref[j + 1]
        pool_o_ref[POOL_BUF - 1] = u
        mix = jnp.concatenate(
            [y_pool] + [jnp.transpose(mixt_scr[c0:c0 + LANES, :]) for c0 in range(YM_OFF, N_MIX, LANES)], axis=1)
        y = ALPHA * x_ref[...] + _dot(mix.astype(BF16), w_out_ref[...])
        xo_ref[...] = _layer_norm_rows(y, lng_ref[...], lnb_ref[...])


def _sample_step(x, states, consts, carried, layer):
    B = x.shape[0]
    out_shape = (jax.ShapeDtypeStruct((B, D_MODEL), F32),) + tuple(
        jax.ShapeDtypeStruct(s.shape, F32) for s in states)
    state_specs = [
        pl.BlockSpec((None, POOL_BUF, B, D_POOL), lambda h, k: (layer, 0, 0, 0)),
        pl.BlockSpec((None, None, CK, MLSTM_DH, B), lambda h, k: (layer, h, k, 0, 0)),
        pl.BlockSpec((None, None, MLSTM_DH, B), lambda h, k: (layer, h, 0, 0)),
        pl.BlockSpec((None, HEADS, B), lambda h, k: (layer, 0, 0)),
        pl.BlockSpec((None, None, SK, GLA_DV, B), lambda h, k: (layer, h, k, 0, 0)),
    ]
    x_spec = pl.BlockSpec((B, D_MODEL), lambda h, k: (0, 0))
    n_in = 1 + len(states) + len(consts)
    return pl.pallas_call(
        functools.partial(_sample_step_kernel, len(carried)),
        grid=(HEADS, K_SPLIT),
        in_specs=[x_spec] + state_specs + [_layer_spec(cst.shape, layer) for cst in consts]
        + [pl.BlockSpec(memory_space=pl.ANY)] * len(carried),
        out_specs=[x_spec] + state_specs,
        out_shape=out_shape,
        input_output_aliases={n_in + i: 1 + i for i in range(len(carried))},
        scratch_shapes=[
            pltpu.VMEM((N_Z, B), F32),
            pltpu.VMEM((N_MIX, B), F32),
            pltpu.VMEM((2 * SUBLANES, B), F32),
            pltpu.VMEM((HEADS * KP, B), F32),
            pltpu.VMEM((MLSTM_DH, B), F32),
            pltpu.VMEM((GLA_DV, B), F32),
        ],
        compiler_params=pltpu.CompilerParams(
            dimension_semantics=("arbitrary", "arbitrary"), vmem_limit_bytes=VMEM_LIMIT),
        name="sample_step",
    )(x, *states, *consts, *carried)


def _ffn_kernel(x_ref, p_ref, wg_ref, wu_ref, wd_ref, wp_ref, wpg_ref, lng_ref, lnb_ref, o_ref):
    x = x_ref[...]
    xb = x.astype(BF16)
    acc = ALPHA * x + _dot(p_ref[...].astype(BF16), wp_ref[...]) * jax.nn.sigmoid(_dot(xb, wpg_ref[...]))
    for c0 in range(0, D_FF, FF_CHUNK):
        c1 = min(c0 + FF_CHUNK, D_FF)
        gate = _dot(xb, wg_ref[:, c0:c1])
        up = _dot(xb, wu_ref[:, c0:c1])
        hid = (gate * jax.nn.sigmoid(gate) * up).astype(BF16)
        acc = acc + _dot(hid, wd_ref[c0:c1, :])
    o_ref[...] = _layer_norm_rows(acc, lng_ref[...], lnb_ref[...])


def _ffn(x, p, consts, layer, tm):
    M = x.shape[0]
    return pl.pallas_call(
        _ffn_kernel,
        grid=(M // tm,),
        in_specs=[pl.BlockSpec((tm, D_MODEL), lambda i: (i, 0)),
                  pl.BlockSpec((None, tm, D_PLE), lambda i: (layer, i, 0))]
        + [_layer_spec(c.shape, layer) for c in consts],
        out_specs=pl.BlockSpec((tm, D_MODEL), lambda i: (i, 0)),
        out_shape=jax.ShapeDtypeStruct((M, D_MODEL), F32),
        compiler_params=pltpu.CompilerParams(
            dimension_semantics=("arbitrary",), vmem_limit_bytes=VMEM_LIMIT),
        name="ffn",
    )(x, p, *consts)


def kernel(x_prompt, x_sample, p_prompt, p_sample, state_pool, state_mlstm_C, state_mlstm_n, state_mlstm_m, state_gla_S, w_in, mlstm_i_bias, mlstm_f_bias, mlstm_norm_w, gla_a2, gla_a_bias, gla_norm_w, pool_mix, pool_scale, w_out, ln1_g, ln1_b, w_gate, w_up, w_down, w_ple, w_ple_gate, ln2_g, ln2_b):
    Bp, Tp, _ = x_prompt.shape
    Bs = x_sample.shape[0]

    w_in_p = _pad_in_proj_t(w_in).astype(BF16)
    w_out_p = _pad_out_proj(w_out).astype(BF16)
    gbias = jnp.zeros((DEPTH, 1, LANES), F32)
    gbias = gbias.at[:, 0, GATE_I:GATE_I + HEADS].set(mlstm_i_bias.astype(F32))
    gbias = gbias.at[:, 0, GATE_F:GATE_F + HEADS].set(mlstm_f_bias.astype(F32))
    a2_p = jnp.zeros((DEPTH, LANES, HEADS * KP), F32)
    a2_p = a2_p.at[:, GATE_A:GATE_A + GLA_RANK, :].set(_pad_heads(gla_a2.astype(F32), GLA_DK, KP)).astype(BF16)
    abias_p = _pad_heads(gla_a_bias.astype(F32), GLA_DK, KP)[:, None, :]
    mnorm_r = mlstm_norm_w.astype(F32)[:, None, :]
    gnorm_p = _pad_heads(gla_norm_w.astype(F32), GLA_DV, HP)[:, None, :]
    n_grp = len(POOL_WINDOWS)
    eye = jnp.eye(n_grp, dtype=F32)
    pmix_bd = (pool_mix.astype(F32)[:, :, :, None, :] * eye[None, :, None, :, None]).reshape(
        DEPTH, D_POOL, D_POOL).astype(BF16)
    pscale = pool_scale.astype(F32)[:, None, :]
    row = lambda a: a.astype(F32)[:, None, :]
    ln1_g_r, ln1_b_r, ln2_g_r, ln2_b_r = row(ln1_g), row(ln1_b), row(ln2_g), row(ln2_b)
    wg_b, wu_b, wd_b = w_gate.astype(BF16), w_up.astype(BF16), w_down.astype(BF16)
    wp_b, wpg_b = w_ple.astype(BF16), w_ple_gate.astype(BF16)

    mnorm_t = jnp.broadcast_to(jnp.transpose(mnorm_r, (0, 2, 1)), (DEPTH, HEADS * FH, LANES))
    mixer_consts = (w_in_p, w_out_p, gbias, a2_p, abias_p, mnorm_t, gnorm_p, pmix_bd, pscale, ln1_g_r, ln1_b_r)
    ffn_consts = (wg_b, wu_b, wd_b, wp_b, wpg_b, ln2_g_r, ln2_b_r)
    xp = x_prompt
    xs = x_sample.reshape(Bs, D_MODEL)
    pp = p_prompt.reshape(DEPTH, Bp * Tp, D_PLE)
    ps = p_sample.reshape(DEPTH, Bs, D_PLE)
    sample_states = (jnp.transpose(state_pool, (0, 2, 1, 3)), jnp.transpose(state_mlstm_C, (0, 2, 3, 4, 1)),
                     jnp.transpose(state_mlstm_n, (0, 2, 3, 1)), jnp.transpose(state_mlstm_m, (0, 2, 1)),
                     jnp.transpose(state_gla_S, (0, 2, 3, 4, 1)))
    col = lambda a: jnp.transpose(a, (0, 2, 1))
    sample_consts = (w_in_p, w_out_p, col(gbias), col(a2_p), col(abias_p), col(mnorm_r), col(gnorm_p),
                     pmix_bd, pscale, ln1_g_r, ln1_b_r)
    outs_p = []
    carried = ()
    for l in range(DEPTH):
        xp, pool_p, c_p, n_p, m_p, s_p = _prompt_mixer(xp, mixer_consts, l)
        xp = _ffn(xp.reshape(Bp * Tp, D_MODEL), pp, ffn_consts, l, tm=512).reshape(Bp, Tp, D_MODEL)
        outs_p.append((pool_p, c_p, n_p, m_p[:, 0, GATE_F:GATE_F + HEADS], s_p))
        xs, *carried = _sample_step(xs, sample_states, sample_consts, tuple(carried), l)
        xs = _ffn(xs, ps, ffn_consts, l, tm=Bs)

    pool_s, c_s, n_s, m_s, s_s = carried
    stack = lambda j: jnp.stack([o[j] for o in outs_p], axis=0)
    return ((xp, xs.reshape(Bs, 1, D_MODEL)) + tuple(stack(j) for j in range(5))
            + (jnp.transpose(pool_s, (0, 2, 1, 3)), jnp.transpose(c_s, (0, 4, 1, 2, 3)),
               jnp.transpose(n_s, (0, 3, 1, 2)), jnp.transpose(m_s, (0, 2, 1)),
               jnp.transpose(s_s, (0, 4, 1, 2, 3))))
```

```python
import functools

import numpy as np
import jax
import jax.numpy as jnp
from jax import lax
from jax.experimental import pallas as pl
from jax.experimental.pallas import tpu as pltpu

F32 = jnp.float32
BF16 = jnp.bfloat16

D_MODEL = 1024
DEPTH = 4
PAST_LEN = 16384
D_POOL = 256
POOL_WINDOWS = (2, 4, 8, 16)
POOL_GROUP = 64
POOL_BUF = 15
D_MLSTM = 384
HEADS = 4
MLSTM_DH = 96
D_GLA = 384
GLA_DV = 96
GLA_DK = 48
GLA_RANK = 16
GLA_TAU = 16.0
D_MIX = 1024
D_FF = 2816
D_PLE = 256
CHUNK = 64
ALPHA = (2 * DEPTH) ** 0.25
LN_EPS = 1e-5

LANES = 128
SUBLANES = 8
VMEM_LIMIT = 56 * 1024 * 1024

HP = LANES
FH = MLSTM_DH
KP = 64
U_OFF = 0
MQ_OFF = U_OFF + D_POOL
MK_OFF = MQ_OFF + HEADS * FH
MV_OFF = MK_OFF + HEADS * HP
MO_OFF = MV_OFF + HEADS * FH
GQ_OFF = MO_OFF + HEADS * FH
GK_OFF = GQ_OFF + HEADS * KP
GV_OFF = GK_OFF + HEADS * KP
GG_OFF = GV_OFF + HEADS * HP
GATE_OFF = GG_OFF + HEADS * HP
N_Z = GATE_OFF + LANES
GATE_I = 0
GATE_F = 4
GATE_A = 8
YM_OFF = D_POOL
YG_OFF = YM_OFF + HEADS * FH
N_MIX = YG_OFF + HEADS * HP

TT = 512
MCHUNK = 4 * CHUNK
ZT_Q = 0
ZT_V = ZT_Q + HEADS * FH
ZT_O = ZT_V + HEADS * FH
ZT_ROWS = ZT_O + HEADS * FH
FF_CHUNK = 512


def _pad_heads(a, width, padded):
    lead = a.shape[:-1]
    a = a.reshape(lead + (HEADS, width))
    a = jnp.pad(a, [(0, 0)] * len(lead) + [(0, 0), (0, padded - width)])
    return a.reshape(lead + (HEADS * padded,))


def _pad_head_rows(a, width, padded):
    L, _, K = a.shape
    a = jnp.pad(a.reshape(L, HEADS, width, K), ((0, 0), (0, 0), (0, padded - width), (0, 0)))
    return a.reshape(L, HEADS * padded, K)


def _pad_in_proj_t(w_in):
    sizes = (D_POOL, D_MLSTM, D_MLSTM, D_MLSTM, HEADS, HEADS, D_MLSTM,
             HEADS * GLA_DK, HEADS * GLA_DK, D_GLA, GLA_RANK, D_GLA)
    w_t = jnp.transpose(w_in, (0, 2, 1))
    u, mq, mk, mv, mi, mf, mo, gq, gk, gv, ga, gg = jnp.split(w_t, np.cumsum(sizes)[:-1].tolist(), axis=1)
    gate_pad = jnp.zeros((w_t.shape[0], LANES - GATE_A - GLA_RANK, w_t.shape[2]), w_t.dtype)
    wide = lambda a: _pad_head_rows(a, MLSTM_DH, HP)
    narrow = lambda a: _pad_head_rows(a, GLA_DK, KP)
    return jnp.concatenate([u, mq, wide(mk), mv, mo, narrow(gq), narrow(gk), wide(gv),
                            wide(gg), mi, mf, ga, gate_pad], axis=1)


def _pad_out_proj(w_out):
    dense = w_out[:, :D_POOL + D_MLSTM]
    heads = w_out[:, D_POOL + D_MLSTM:].reshape(w_out.shape[0], HEADS, GLA_DV, D_MODEL)
    heads = jnp.pad(heads, ((0, 0), (0, 0), (0, HP - GLA_DV), (0, 0)))
    return jnp.concatenate([dense, heads.reshape(w_out.shape[0], HEADS * HP, D_MODEL)], axis=1)


def _dot(a, b):
    return jnp.dot(a, b, preferred_element_type=F32)


def _dot_nt(a, b):
    return lax.dot_general(a, b, (((1,), (1,)), ((), ())), preferred_element_type=F32)


def _dot_tn(a, b):
    return lax.dot_general(a, b, (((0,), (0,)), ((), ())), preferred_element_type=F32)


def _log_sigmoid(x):
    return jnp.minimum(x, 0.0) - jnp.log1p(jnp.exp(-jnp.abs(x)))


def _cummax_rows(x):
    n = x.shape[0]
    rid = lax.broadcasted_iota(jnp.int32, x.shape, 0)
    sh = 1
    while sh < n:
        x = jnp.maximum(x, jnp.where(rid >= sh, pltpu.roll(x, sh, axis=0), -jnp.inf))
        sh *= 2
    return x


def _cumsum_rows(x):
    n = x.shape[0]
    rid = lax.broadcasted_iota(jnp.int32, x.shape, 0)
    sh = 1
    while sh < n:
        x = x + jnp.where(rid >= sh, pltpu.roll(x, sh, axis=0), 0.0)
        sh *= 2
    return x


def _layer_norm_rows(y, g, b):
    mu = jnp.mean(y, axis=-1, keepdims=True)
    d = y - mu
    var = jnp.mean(d * d, axis=-1, keepdims=True)
    return d * lax.rsqrt(var + LN_EPS) * g + b


def _pool_select(lane, a, b):
    return jnp.where((lane & (LANES - 1)) < POOL_GROUP, a, b)


def _prompt_mixer_kernel(x_ref, w_in_ref, w_out_ref, gbias_ref, a2_ref, abias_ref, mnorm_t_ref,
                         gnorm_ref, pmix_ref, pscale_ref, lng_ref, lnb_ref,
                         xo_ref, pool_o_ref, c_o_ref, n_o_ref, m_o_ref, s_o_ref,
                         z_scr, zt_scr, mix_scr, mixt_scr, ext_scr, c_scr, m_scr, s_scr, lvl_scr, tr_scr):
    ti = pl.program_id(1)
    n_t = pl.num_programs(1)

    @pl.when(ti == 0)
    def _():
        ext_scr[0:2 * SUBLANES, :] = jnp.zeros((2 * SUBLANES, D_POOL), F32)
        c_scr[...] = jnp.zeros(c_scr.shape, F32)
        m_scr[...] = jnp.zeros(m_scr.shape, F32)
        s_scr[...] = jnp.zeros(s_scr.shape, F32)

    xb = x_ref[0].astype(BF16)
    for lo, hi in ((U_OFF, MQ_OFF), (MK_OFF, MV_OFF), (GQ_OFF, N_Z)):
        for c0 in range(lo, hi, 512):
            c1 = min(c0 + 512, hi)
            z_scr[:, c0:c1] = _dot_nt(xb, w_in_ref[c0:c1, :])
    for src, dst in ((MQ_OFF, ZT_Q), (MV_OFF, ZT_V), (MO_OFF, ZT_O)):
        zt_scr[dst:dst + HEADS * FH, :] = _dot_nt(w_in_ref[src:src + HEADS * FH, :], xb)

    hist = 2 * SUBLANES
    u = z_scr[:, U_OFF:U_OFF + D_POOL]
    ext_scr[hist:hist + TT, :] = u
    ext = ext_scr[...]
    e0 = ext[:, 0:LANES]
    e1 = ext[:, LANES:2 * LANES]
    a2s = e0 + pltpu.roll(e0, 1, axis=0)
    a4s = a2s + pltpu.roll(a2s, 2, axis=0)
    b2s = e1 + pltpu.roll(e1, 1, axis=0)
    b4s = b2s + pltpu.roll(b2s, 2, axis=0)
    b8s = b4s + pltpu.roll(b4s, 4, axis=0)
    b16s = b8s + pltpu.roll(b8s, 8, axis=0)
    lane = lax.broadcasted_iota(jnp.int32, (TT, LANES), 1)
    pos1 = (lax.broadcasted_iota(jnp.int32, (TT, LANES), 0) + ti * TT + 1).astype(F32)
    ws0 = _pool_select(lane, a2s[hist:], a4s[hist:])
    ws1 = _pool_select(lane, b8s[hist:], b16s[hist:])
    cnt0 = jnp.minimum(_pool_select(lane, 2.0, 4.0), pos1)
    cnt1 = jnp.minimum(_pool_select(lane, 8.0, 16.0), pos1)
    d0 = ws0 / cnt0 - u[:, 0:LANES]
    d1 = ws1 / cnt1 - u[:, LANES:]
    dpool = jnp.concatenate([d0, d1], axis=1).astype(BF16)
    y_pool = _dot(dpool, pmix_ref[...]) * pscale_ref[...]
    mix_scr[:, 0:D_POOL] = y_pool.astype(BF16)
    tail = ext_scr[TT:TT + hist, :]
    ext_scr[0:hist, :] = tail

    @pl.when(ti == n_t - 1)
    def _():
        pool_o_ref[0] = ext_scr[1:hist, :]

    gbias = gbias_ref[...]
    abias = abias_ref[...]
    trow = lax.broadcasted_iota(jnp.int32, (HEADS * CHUNK, CHUNK), 0) & (CHUNK - 1)
    scol = lax.broadcasted_iota(jnp.int32, (HEADS * CHUNK, CHUNK), 1)
    differ = trow ^ scol
    level = jnp.where(scol > trow, -2, -1)
    for bit in range(CHUNK.bit_length() - 1):
        level = level + jnp.where((scol < trow) & (differ >= (1 << bit)), 1, 0)
    lvl_scr[...] = level

    def chunk_body(c):
        rows = pl.ds(c * CHUNK, CHUNK)
        graw = z_scr[rows, GATE_OFF:GATE_OFF + LANES]

        ya = _dot(graw.astype(BF16), a2_ref[...]) + abias
        la = _log_sigmoid(ya) * (1.0 / GLA_TAU)
        bc = _cumsum_rows(la)
        gq = z_scr[rows, GQ_OFF:GQ_OFF + HEADS * KP] * (GLA_DK ** -0.5)
        gk = z_scr[rows, GK_OFF:GK_OFF + HEADS * KP]
        width = HEADS * KP
        t_id = lax.broadcasted_iota(jnp.int32, (CHUNK, width), 0)
        lane_head = lax.broadcasted_iota(jnp.int32, (CHUNK, width), 1) // KP

        def stack_heads(a):
            return jnp.concatenate([jnp.where(lane_head == h, a, jnp.zeros_like(a)) for h in range(HEADS)], axis=0)

        amat = jnp.where(lvl_scr[...] == -1, _dot_nt(stack_heads(gq.astype(BF16)), gk.astype(BF16)), 0.0)
        half = CHUNK // 2
        while half >= 1:
            blk = 2 * half
            upper = (t_id & half) != 0
            if blk >= SUBLANES:
                ref_rows = jnp.concatenate(
                    [jnp.broadcast_to(bc[b0 + half - 1:b0 + half, :], (blk, width))
                     for b0 in range(0, CHUNK, blk)], axis=0)
            else:
                off = t_id & (blk - 1)
                ref_rows = bc
                for o in range(blk):
                    sh = (o - (half - 1)) % CHUNK
                    if sh != 0:
                        ref_rows = jnp.where(off == o, pltpu.roll(bc, sh, axis=0), ref_rows)
            dlt = bc - ref_rows
            fac = (jnp.where(upper, gq, gk) * jnp.exp(jnp.where(upper, dlt, -dlt))).astype(BF16)
            amat = jnp.where(lvl_scr[...] == half.bit_length() - 1, _dot_nt(stack_heads(fac), fac), amat)
            half //= 2

        o_inter = _dot(stack_heads((gq * jnp.exp(bc)).astype(BF16)), s_scr[...].astype(BF16))
        bc_t = jnp.transpose(bc)
        gk_t = jnp.transpose(gk)
        last_t = bc_t[:, CHUNK - 1:CHUNK]
        ke_t = (gk_t * jnp.exp(last_t - bc_t)).astype(BF16)
        s_dec = jnp.exp(last_t)
        for h in range(HEADS):
            hr = slice(h * CHUNK, (h + 1) * CHUNK)
            gv = z_scr[rows, GV_OFF + h * HP:GV_OFF + (h + 1) * HP].astype(BF16)
            o = o_inter[hr] + _dot(amat[hr].astype(BF16), gv)
            ms = jnp.sum(o * o, axis=1, keepdims=True) * (1.0 / GLA_DV)
            on = o * lax.rsqrt(ms + LN_EPS) * gnorm_ref[:, h * HP:(h + 1) * HP]
            gg = z_scr[rows, GG_OFF + h * HP:GG_OFF + (h + 1) * HP]
            mix_scr[rows, YG_OFF + h * HP:YG_OFF + (h + 1) * HP] = (gg * jax.nn.sigmoid(gg) * on).astype(BF16)
            kr = slice(h * KP, (h + 1) * KP)
            s_scr[kr, :] = s_dec[kr] * s_scr[kr, :] + _dot(ke_t[kr], gv)

    causal = (lax.broadcasted_iota(jnp.int32, (MCHUNK, MCHUNK), 0)
              <= lax.broadcasted_iota(jnp.int32, (MCHUNK, MCHUNK), 1))
    ones_rows = jnp.where(lax.broadcasted_iota(jnp.int32, (SUBLANES, MCHUNK), 0) == 0, 1.0, 0.0)

    def mchunk_body(p):
        for cc in range(MCHUNK // CHUNK):
            chunk_body(p * (MCHUNK // CHUNK) + cc)
        lanes = pl.ds(p * MCHUNK, MCHUNK)
        rows = pl.ds(p * MCHUNK, MCHUNK)
        g_r = z_scr[rows, GATE_OFF:GATE_OFF + LANES] + gbias
        b = _cumsum_rows(_log_sigmoid(g_r))
        i_sh = pltpu.roll(g_r, GATE_F - GATE_I, axis=1)
        r_r = i_sh - b
        m_in = m_scr[0:1, :]
        inter = b + m_in
        m_t = jnp.maximum(inter, b + _cummax_rows(r_r))
        b_end = b[MCHUNK - 1:MCHUNK, :]
        m_end = m_t[MCHUNK - 1:MCHUNK, :]
        m_scr[0:1, :] = m_end
        a = jnp.transpose(b - m_t)
        w_inter = jnp.transpose(jnp.exp(inter - m_t))
        floor = jnp.transpose(jnp.exp(-m_t))
        w_end = jnp.transpose(jnp.exp(b_end - b + i_sh - m_end))
        decay = jnp.exp(b_end + m_in - m_end)
        for h in range(HEADS):
            gl = GATE_F + h
            fr = slice(h * FH, (h + 1) * FH)
            q_t = (zt_scr[ZT_Q + h * FH:ZT_Q + (h + 1) * FH, lanes] * (MLSTM_DH ** -0.5)).astype(BF16)
            v_aug = jnp.concatenate([zt_scr[ZT_V + h * FH:ZT_V + (h + 1) * FH, lanes], ones_rows], axis=0)
            k_r = z_scr[rows, MK_OFF + h * HP:MK_OFF + h * HP + FH].astype(BF16)
            e_t = jnp.exp(jnp.where(causal, a[gl:gl + 1, :] + r_r[:, gl:gl + 1], -jnp.inf))
            s_t = (_dot(k_r, q_t) * e_t).astype(BF16)
            ct = c_scr[h]
            c_scr[h] = decay[:, gl:gl + 1] * ct + _dot((v_aug * w_end[gl:gl + 1, :]).astype(BF16), k_r)
            num = (w_inter[gl:gl + 1, :] * _dot(ct.astype(BF16), q_t)
                   + _dot(v_aug.astype(BF16), s_t))
            hval = num[0:FH] / jnp.maximum(jnp.abs(num[FH:FH + 1, :]), floor[gl:gl + 1, :])
            mu = jnp.mean(hval, axis=0, keepdims=True)
            dlt = hval - mu
            var = jnp.mean(dlt * dlt, axis=0, keepdims=True)
            mo_t = zt_scr[ZT_O + h * FH:ZT_O + (h + 1) * FH, lanes]
            mixt_scr[fr, lanes] = (jax.nn.sigmoid(mo_t) * dlt * lax.rsqrt(var + LN_EPS)
                                   * mnorm_t_ref[fr, :]).astype(BF16)

    for p in range(TT // MCHUNK):
        mchunk_body(p)

    y = (ALPHA * x_ref[0] + _dot(mix_scr[:, 0:YM_OFF], w_out_ref[0:YM_OFF, :])
         + _dot_tn(mixt_scr[...], w_out_ref[YM_OFF:YG_OFF, :])
         + _dot(mix_scr[:, YG_OFF:N_MIX], w_out_ref[YG_OFF:N_MIX, :]))
    xo_ref[0] = _layer_norm_rows(y, lng_ref[...], lnb_ref[...])

    @pl.when(ti == n_t - 1)
    def _():
        for h in range(HEADS):
            ct = c_scr[h]
            tr_scr[...] = jnp.zeros(tr_scr.shape, F32)
            tr_scr[0:FH, 0:FH] = ct[0:FH, :]
            c_o_ref[0, h] = jnp.transpose(tr_scr[...])[0:FH, 0:FH]
            n_o_ref[0, h:h + 1, :] = ct[FH:FH + 1, :]
            s_o_ref[0, h] = s_scr[h * KP:h * KP + GLA_DK, 0:GLA_DV]
        m_o_ref[0] = m_scr[...]


def _layer_spec(shape, layer):
    nd = len(shape)
    return pl.BlockSpec((None,) + tuple(shape[1:]), lambda *_: (layer,) + (0,) * (nd - 1),
                        pipeline_mode=pl.Buffered(1))


def _prompt_mixer(x, consts, layer):
    B, T, _ = x.shape
    out_shape = (
        jax.ShapeDtypeStruct((B, T, D_MODEL), F32),
        jax.ShapeDtypeStruct((B, POOL_BUF, D_POOL), F32),
        jax.ShapeDtypeStruct((B, HEADS, MLSTM_DH, MLSTM_DH), F32),
        jax.ShapeDtypeStruct((B, HEADS, FH), F32),
        jax.ShapeDtypeStruct((B, SUBLANES, LANES), F32),
        jax.ShapeDtypeStruct((B, HEADS, GLA_DK, GLA_DV), F32),
    )
    out_specs = (
        pl.BlockSpec((1, TT, D_MODEL), lambda b, t: (b, t, 0)),
        pl.BlockSpec((1, POOL_BUF, D_POOL), lambda b, t: (b, 0, 0)),
        pl.BlockSpec((1, HEADS, MLSTM_DH, MLSTM_DH), lambda b, t: (b, 0, 0, 0)),
        pl.BlockSpec((1, HEADS, FH), lambda b, t: (b, 0, 0)),
        pl.BlockSpec((1, SUBLANES, LANES), lambda b, t: (b, 0, 0)),
        pl.BlockSpec((1, HEADS, GLA_DK, GLA_DV), lambda b, t: (b, 0, 0, 0)),
    )
    return pl.pallas_call(
        _prompt_mixer_kernel,
        grid=(B, T // TT),
        in_specs=[pl.BlockSpec((1, TT, D_MODEL), lambda b, t: (b, t, 0))]
        + [_layer_spec(c.shape, layer) for c in consts],
        out_specs=out_specs,
        out_shape=out_shape,
        scratch_shapes=[
            pltpu.VMEM((TT, N_Z), F32),
            pltpu.VMEM((ZT_ROWS, TT), F32),
            pltpu.VMEM((TT, N_MIX), BF16),
            pltpu.VMEM((HEADS * FH, TT), BF16),
            pltpu.VMEM((TT + 2 * SUBLANES, D_POOL), F32),
            pltpu.VMEM((HEADS, FH + SUBLANES, FH), F32),
            pltpu.VMEM((SUBLANES, LANES), F32),
            pltpu.VMEM((HEADS * KP, HP), F32),
            pltpu.VMEM((HEADS * CHUNK, CHUNK), jnp.int32),
            pltpu.VMEM((LANES, LANES), F32),
        ],
        compiler_params=pltpu.CompilerParams(
            dimension_semantics=("arbitrary", "arbitrary"), vmem_limit_bytes=VMEM_LIMIT),
        name="prompt_mixer",
    )(x, *consts)


K_SPLIT = 2
CK = MLSTM_DH // K_SPLIT
SK = GLA_DK // K_SPLIT


def _row_of(a, idx):
    sub = lax.broadcasted_iota(jnp.int32, a.shape, 0)
    return jnp.sum(jnp.where(sub == idx, a, 0.0), axis=0, keepdims=True)


def _sample_step_kernel(n_carried, *refs):
    (x_ref, pool_ref, c_ref, n_ref, m_ref, s_ref,
     w_in_ref, w_out_ref, gbias_ref, a2t_ref, abias_ref, mnorm_ref, gnorm_ref,
     pmix_ref, pscale_ref, lng_ref, lnb_ref) = refs[:17]
    (xo_ref, pool_o_ref, c_o_ref, n_o_ref, m_o_ref, s_o_ref,
     zt_scr, mixt_scr, gate_scr, ea_scr, num_scr, oin_scr) = refs[17 + n_carried:]
    h = pl.program_id(0)
    kh = pl.program_id(1)
    B = x_ref.shape[0]
    first = jnp.logical_and(h == 0, kh == 0)
    last = jnp.logical_and(h == HEADS - 1, kh == K_SPLIT - 1)

    @pl.when(first)
    def _():
        mixt_scr[...] = jnp.zeros(mixt_scr.shape, F32)
        xb = x_ref[...].astype(BF16)
        for c0 in range(0, N_Z, LANES):
            zt_scr[c0:c0 + LANES, :] = _dot_nt(w_in_ref[c0:c0 + LANES, :], xb)
        g = zt_scr[GATE_OFF:GATE_OFF + LANES, :]
        gb = g + gbias_ref[...]
        log_i = gb[GATE_I:GATE_I + HEADS]
        log_f = _log_sigmoid(gb[GATE_F:GATE_F + HEADS])
        inter = log_f + m_ref[...]
        m_t = jnp.maximum(inter, log_i)
        m_o_ref[...] = m_t
        gate_scr[0:HEADS, :] = jnp.exp(inter - m_t)
        gate_scr[HEADS:2 * HEADS, :] = jnp.exp(log_i - m_t)
        gate_scr[2 * HEADS:3 * HEADS, :] = jnp.exp(-m_t)
        ya = _dot(a2t_ref[...], g.astype(BF16)) + abias_ref[...]
        ea_scr[...] = jnp.exp(_log_sigmoid(ya) * (1.0 / GLA_TAU))

    gates = gate_scr[...]
    dec = _row_of(gates, h)
    wig = _row_of(gates, h + HEADS)
    floor = _row_of(gates, h + 2 * HEADS)

    def head_rows(off, width, count, start=0):
        return zt_scr[pl.ds(pl.multiple_of(off + h * width + start, SUBLANES), count), :]

    k0 = kh * CK
    q_sl = head_rows(MQ_OFF, FH, CK, k0) * (MLSTM_DH ** -0.5)
    k_sl = head_rows(MK_OFF, HP, CK, k0) * wig
    v_t = head_rows(MV_OFF, FH, MLSTM_DH)
    acc = jnp.zeros((MLSTM_DH, B), F32)
    for k in range(CK):
        c_old = c_ref[k]
        acc = acc + q_sl[k:k + 1, :] * c_old
        c_o_ref[k] = dec * c_old + k_sl[k:k + 1, :] * v_t

    @pl.when(kh == 0)
    def _():
        num_scr[...] = acc

    @pl.when(kh > 0)
    def _():
        num_scr[...] += acc

    @pl.when(kh == K_SPLIT - 1)
    def _():
        q_t = head_rows(MQ_OFF, FH, MLSTM_DH) * (MLSTM_DH ** -0.5)
        k_t = head_rows(MK_OFF, HP, MLSTM_DH)
        n_old = n_ref[...]
        sc = jnp.sum(q_t * k_t, axis=0, keepdims=True) * wig
        num = dec * num_scr[...] + sc * v_t
        nq = dec * jnp.sum(q_t * n_old, axis=0, keepdims=True) + sc
        hval = num / jnp.maximum(jnp.abs(nq), floor)
        n_o_ref[...] = dec * n_old + wig * k_t
        mu = jnp.mean(hval, axis=0, keepdims=True)
        dlt = hval - mu
        var = jnp.mean(dlt * dlt, axis=0, keepdims=True)
        w_norm = mnorm_ref[pl.ds(pl.multiple_of(h * FH, SUBLANES), MLSTM_DH), :]
        mo = head_rows(MO_OFF, FH, MLSTM_DH)
        mixt_scr[pl.ds(pl.multiple_of(YM_OFF + h * FH, SUBLANES), MLSTM_DH), :] = (
            jax.nn.sigmoid(mo) * dlt * lax.rsqrt(var + LN_EPS) * w_norm)

    s0 = kh * SK
    ea_sl = ea_scr[pl.ds(pl.multiple_of(h * KP + s0, SUBLANES), SK), :]
    gq_sl = head_rows(GQ_OFF, KP, SK, s0) * (GLA_DK ** -0.5)
    gk_sl = head_rows(GK_OFF, KP, SK, s0)
    gv_t = head_rows(GV_OFF, HP, GLA_DV)
    gqe_sl = gq_sl * ea_sl
    oacc = jnp.zeros((GLA_DV, B), F32)
    for k in range(SK):
        s_old = s_ref[k]
        oacc = oacc + gqe_sl[k:k + 1, :] * s_old
        s_o_ref[k] = ea_sl[k:k + 1, :] * s_old + gk_sl[k:k + 1, :] * gv_t

    @pl.when(kh == 0)
    def _():
        oin_scr[...] = oacc

    @pl.when(kh > 0)
    def _():
        oin_scr[...] += oacc

    @pl.when(kh == K_SPLIT - 1)
    def _():
        gq_t = head_rows(GQ_OFF, KP, GLA_DK) * (GLA_DK ** -0.5)
        gk_t = head_rows(GK_OFF, KP, GLA_DK)
        qk = jnp.sum(gq_t * gk_t, axis=0, keepdims=True)
        o = oin_scr[...] + qk * gv_t
        ms = jnp.mean(o * o, axis=0, keepdims=True)
        w_norm = gnorm_ref[pl.ds(pl.multiple_of(h * HP, SUBLANES), GLA_DV), :]
        gg = head_rows(GG_OFF, HP, GLA_DV)
        mixt_scr[pl.ds(pl.multiple_of(YG_OFF + h * HP, SUBLANES), GLA_DV), :] = (
            gg * jax.nn.sigmoid(gg) * o * lax.rsqrt(ms + LN_EPS) * w_norm)

    @pl.when(last)
    def _():
        u = jnp.concatenate([jnp.transpose(zt_scr[c0:c0 + LANES, :]) for c0 in range(0, D_POOL, LANES)], axis=1)
        run = u
        sums = {}
        for j in range(1, max(POOL_WINDOWS)):
            run = run + pool_ref[POOL_BUF - j]
            if j + 1 in POOL_WINDOWS:
                sums[j + 1] = run
        grp = lax.broadcasted_iota(jnp.int32, (B, D_POOL), 1) // POOL_GROUP
        wmean = jnp.zeros((B, D_POOL), F32)
        for gidx, w in enumerate(POOL_WINDOWS):
            wmean = jnp.where(grp == gidx, sums[w] / float(min(w, PAST_LEN + 1)), wmean)
        y_pool = _dot((wmean - u).astype(BF16), pmix_ref[...]) * pscale_ref[...]
        for j in range(POOL_BUF - 1):
            pool_o_ref[j] = pool_ref[j + 1]
        pool_o_ref[POOL_BUF - 1] = u
        mix = jnp.concatenate(
            [y_pool] + [jnp.transpose(mixt_scr[c0:c0 + LANES, :]) for c0 in range(YM_OFF, N_MIX, LANES)], axis=1)
        y = ALPHA * x_ref[...] + _dot(mix.astype(BF16), w_out_ref[...])
        xo_ref[...] = _layer_norm_rows(y, lng_ref[...], lnb_ref[...])


def _sample_step(x, states, consts, carried, layer):
    B = x.shape[0]
    out_shape = (jax.ShapeDtypeStruct((B, D_MODEL), F32),) + tuple(
        jax.ShapeDtypeStruct(s.shape, F32) for s in states)
    state_specs = [
        pl.BlockSpec((None, POOL_BUF, B, D_POOL), lambda h, k: (layer, 0, 0, 0)),
        pl.BlockSpec((None, None, CK, MLSTM_DH, B), lambda h, k: (layer, h, k, 0, 0)),
        pl.BlockSpec((None, None, MLSTM_DH, B), lambda h, k: (layer, h, 0, 0)),
        pl.BlockSpec((None, HEADS, B), lambda h, k: (layer, 0, 0)),
        pl.BlockSpec((None, None, SK, GLA_DV, B), lambda h, k: (layer, h, k, 0, 0)),
    ]
    x_spec = pl.BlockSpec((B, D_MODEL), lambda h, k: (0, 0))
    n_in = 1 + len(states) + len(consts)
    return pl.pallas_call(
        functools.partial(_sample_step_kernel, len(carried)),
        grid=(HEADS, K_SPLIT),
        in_specs=[x_spec] + state_specs + [_layer_spec(cst.shape, layer) for cst in consts]
        + [pl.BlockSpec(memory_space=pl.ANY)] * len(carried),
        out_specs=[x_spec] + state_specs,
        out_shape=out_shape,
        input_output_aliases={n_in + i: 1 + i for i in range(len(carried))},
        scratch_shapes=[
            pltpu.VMEM((N_Z, B), F32),
            pltpu.VMEM((N_MIX, B), F32),
            pltpu.VMEM((2 * SUBLANES, B), F32),
            pltpu.VMEM((HEADS * KP, B), F32),
            pltpu.VMEM((MLSTM_DH, B), F32),
            pltpu.VMEM((GLA_DV, B), F32),
        ],
        compiler_params=pltpu.CompilerParams(
            dimension_semantics=("arbitrary", "arbitrary"), vmem_limit_bytes=VMEM_LIMIT),
        name="sample_step",
    )(x, *states, *consts, *carried)


def _ffn_kernel(x_ref, p_ref, wg_ref, wu_ref, wd_ref, wp_ref, wpg_ref, lng_ref, lnb_ref, o_ref):
    x = x_ref[...]
    xb = x.astype(BF16)
    acc = ALPHA * x + _dot(p_ref[...].astype(BF16), wp_ref[...]) * jax.nn.sigmoid(_dot(xb, wpg_ref[...]))
    for c0 in range(0, D_FF, FF_CHUNK):
        c1 = min(c0 + FF_CHUNK, D_FF)
        gate = _dot(xb, wg_ref[:, c0:c1])
        up = _dot(xb, wu_ref[:, c0:c1])
        hid = (gate * jax.nn.sigmoid(gate) * up).astype(BF16)
        acc = acc + _dot(hid, wd_ref[c0:c1, :])
    o_ref[...] = _layer_norm_rows(acc, lng_ref[...], lnb_ref[...])


def _ffn(x, p, consts, layer, tm):
    M = x.shape[0]
    return pl.pallas_call(
        _ffn_kernel,
        grid=(M // tm,),
        in_specs=[pl.BlockSpec((tm, D_MODEL), lambda i: (i, 0)),
                  pl.BlockSpec((None, tm, D_PLE), lambda i: (layer, i, 0))]
        + [_layer_spec(c.shape, layer) for c in consts],
        out_specs=pl.BlockSpec((tm, D_MODEL), lambda i: (i, 0)),
        out_shape=jax.ShapeDtypeStruct((M, D_MODEL), F32),
        compiler_params=pltpu.CompilerParams(
            dimension_semantics=("arbitrary",), vmem_limit_bytes=VMEM_LIMIT),
        name="ffn",
    )(x, p, *consts)


def kernel(x_prompt, x_sample, p_prompt, p_sample, state_pool, state_mlstm_C, state_mlstm_n, state_mlstm_m, state_gla_S, w_in, mlstm_i_bias, mlstm_f_bias, mlstm_norm_w, gla_a2, gla_a_bias, gla_norm_w, pool_mix, pool_scale, w_out, ln1_g, ln1_b, w_gate, w_up, w_down, w_ple, w_ple_gate, ln2_g, ln2_b):
    Bp, Tp, _ = x_prompt.shape
    Bs = x_sample.shape[0]

    w_in_p = _pad_in_proj_t(w_in).astype(BF16)
    w_out_p = _pad_out_proj(w_out).astype(BF16)
    gbias = jnp.zeros((DEPTH, 1, LANES), F32)
    gbias = gbias.at[:, 0, GATE_I:GATE_I + HEADS].set(mlstm_i_bias.astype(F32))
    gbias = gbias.at[:, 0, GATE_F:GATE_F + HEADS].set(mlstm_f_bias.astype(F32))
    a2_p = jnp.zeros((DEPTH, LANES, HEADS * KP), F32)
    a2_p = a2_p.at[:, GATE_A:GATE_A + GLA_RANK, :].set(_pad_heads(gla_a2.astype(F32), GLA_DK, KP)).astype(BF16)
    abias_p = _pad_heads(gla_a_bias.astype(F32), GLA_DK, KP)[:, None, :]
    mnorm_r = mlstm_norm_w.astype(F32)[:, None, :]
    gnorm_p = _pad_heads(gla_norm_w.astype(F32), GLA_DV, HP)[:, None, :]
    n_grp = len(POOL_WINDOWS)
    eye = jnp.eye(n_grp, dtype=F32)
    pmix_bd = (pool_mix.astype(F32)[:, :, :, None, :] * eye[None, :, None, :, None]).reshape(
        DEPTH, D_POOL, D_POOL).astype(BF16)
    pscale = pool_scale.astype(F32)[:, None, :]
    row = lambda a: a.astype(F32)[:, None, :]
    ln1_g_r, ln1_b_r, ln2_g_r, ln2_b_r = row(ln1_g), row(ln1_b), row(ln2_g), row(ln2_b)
    wg_b, wu_b, wd_b = w_gate.astype(BF16), w_up.astype(BF16), w_down.astype(BF16)
    wp_b, wpg_b = w_ple.astype(BF16), w_ple_gate.astype(BF16)

    mnorm_t = jnp.broadcast_to(jnp.transpose(mnorm_r, (0, 2, 1)), (DEPTH, HEADS * FH, MCHUNK))
    mixer_consts = (w_in_p, w_out_p, gbias, a2_p, abias_p, mnorm_t, gnorm_p, pmix_bd, pscale, ln1_g_r, ln1_b_r)
    ffn_consts = (wg_b, wu_b, wd_b, wp_b, wpg_b, ln2_g_r, ln2_b_r)
    xp = x_prompt
    xs = x_sample.reshape(Bs, D_MODEL)
    pp = p_prompt.reshape(DEPTH, Bp * Tp, D_PLE)
    ps = p_sample.reshape(DEPTH, Bs, D_PLE)
    sample_states = (jnp.transpose(state_pool, (0, 2, 1, 3)), jnp.transpose(state_mlstm_C, (0, 2, 3, 4, 1)),
                     jnp.transpose(state_mlstm_n, (0, 2, 3, 1)), jnp.transpose(state_mlstm_m, (0, 2, 1)),
                     jnp.transpose(state_gla_S, (0, 2, 3, 4, 1)))
    col = lambda a: jnp.transpose(a, (0, 2, 1))
    sample_consts = (w_in_p, w_out_p, col(gbias), col(a2_p), col(abias_p), col(mnorm_r), col(gnorm_p),
                     pmix_bd, pscale, ln1_g_r, ln1_b_r)
    outs_p = []
    carried = ()
    for l in range(DEPTH):
        xp, pool_p, c_p, n_p, m_p, s_p = _prompt_mixer(xp, mixer_consts, l)
        xp = _ffn(xp.reshape(Bp * Tp, D_MODEL), pp, ffn_consts, l, tm=512).reshape(Bp, Tp, D_MODEL)
        outs_p.append((pool_p, c_p, n_p, m_p[:, 0, GATE_F:GATE_F + HEADS], s_p))
        xs, *carried = _sample_step(xs, sample_states, sample_consts, tuple(carried), l)
        xs = _ffn(xs, ps, ffn_consts, l, tm=Bs)

    pool_s, c_s, n_s, m_s, s_s = carried
    stack = lambda j: jnp.stack([o[j] for o in outs_p], axis=0)
    return ((xp, xs.reshape(Bs, 1, D_MODEL)) + tuple(stack(j) for j in range(5))
            + (jnp.transpose(pool_s, (0, 2, 1, 3)), jnp.transpose(c_s, (0, 4, 1, 2, 3)),
               jnp.transpose(n_s, (0, 3, 1, 2)), jnp.transpose(m_s, (0, 2, 1)),
               jnp.transpose(s_s, (0, 4, 1, 2, 3))))
```

```python
import functools

import numpy as np
import jax
import jax.numpy as jnp
from jax import lax
from jax.experimental import pallas as pl
from jax.experimental.pallas import tpu as pltpu

F32 = jnp.float32
BF16 = jnp.bfloat16

D_MODEL = 1024
DEPTH = 4
PAST_LEN = 16384
D_POOL = 256
POOL_WINDOWS = (2, 4, 8, 16)
POOL_GROUP = 64
POOL_BUF = 15
D_MLSTM = 384
HEADS = 4
MLSTM_DH = 96
D_GLA = 384
GLA_DV = 96
GLA_DK = 48
GLA_RANK = 16
GLA_TAU = 16.0
D_MIX = 1024
D_FF = 2816
D_PLE = 256
ALPHA = (2 * DEPTH) ** 0.25
LN_EPS = 1e-5

LANES = 128
SUBLANES = 8
VMEM_LIMIT = 56 * 1024 * 1024

HP = LANES
FH = MLSTM_DH
KP = 64
U_OFF = 0
MQ_OFF = U_OFF + D_POOL
MK_OFF = MQ_OFF + HEADS * FH
MV_OFF = MK_OFF + HEADS * HP
MO_OFF = MV_OFF + HEADS * FH
GQ_OFF = MO_OFF + HEADS * FH
GK_OFF = GQ_OFF + HEADS * KP
GV_OFF = GK_OFF + HEADS * KP
GG_OFF = GV_OFF + HEADS * HP
GATE_OFF = GG_OFF + HEADS * HP
N_Z = GATE_OFF + LANES
GATE_I = 0
GATE_F = 4
GATE_A = 8
YM_OFF = D_POOL
YG_OFF = YM_OFF + HEADS * FH
N_MIX = YG_OFF + HEADS * HP

TT = 512
CHUNK = 128
MCHUNK = 256
ZT_Q = 0
ZT_V = ZT_Q + HEADS * FH
ZT_O = ZT_V + HEADS * FH
ZT_ROWS = ZT_O + HEADS * FH
FF_CHUNK = 512


def _pad_heads(a, width, padded):
    lead = a.shape[:-1]
    a = a.reshape(lead + (HEADS, width))
    a = jnp.pad(a, [(0, 0)] * len(lead) + [(0, 0), (0, padded - width)])
    return a.reshape(lead + (HEADS * padded,))


def _pad_head_rows(a, width, padded):
    L, _, K = a.shape
    a = jnp.pad(a.reshape(L, HEADS, width, K), ((0, 0), (0, 0), (0, padded - width), (0, 0)))
    return a.reshape(L, HEADS * padded, K)


def _pad_in_proj_t(w_in):
    sizes = (D_POOL, D_MLSTM, D_MLSTM, D_MLSTM, HEADS, HEADS, D_MLSTM,
             HEADS * GLA_DK, HEADS * GLA_DK, D_GLA, GLA_RANK, D_GLA)
    w_t = jnp.transpose(w_in, (0, 2, 1))
    u, mq, mk, mv, mi, mf, mo, gq, gk, gv, ga, gg = jnp.split(w_t, np.cumsum(sizes)[:-1].tolist(), axis=1)
    gate_pad = jnp.zeros((w_t.shape[0], LANES - GATE_A - GLA_RANK, w_t.shape[2]), w_t.dtype)
    wide = lambda a: _pad_head_rows(a, MLSTM_DH, HP)
    narrow = lambda a: _pad_head_rows(a, GLA_DK, KP)
    return jnp.concatenate([u, mq, wide(mk), mv, mo, narrow(gq), narrow(gk), wide(gv),
                            wide(gg), mi, mf, ga, gate_pad], axis=1)


def _pad_out_proj(w_out):
    dense = w_out[:, :D_POOL + D_MLSTM]
    heads = w_out[:, D_POOL + D_MLSTM:].reshape(w_out.shape[0], HEADS, GLA_DV, D_MODEL)
    heads = jnp.pad(heads, ((0, 0), (0, 0), (0, HP - GLA_DV), (0, 0)))
    return jnp.concatenate([dense, heads.reshape(w_out.shape[0], HEADS * HP, D_MODEL)], axis=1)


def _dot(a, b):
    return jnp.dot(a, b, preferred_element_type=F32)


def _dot_nt(a, b):
    return lax.dot_general(a, b, (((1,), (1,)), ((), ())), preferred_element_type=F32)


def _dot_tn(a, b):
    return lax.dot_general(a, b, (((0,), (0,)), ((), ())), preferred_element_type=F32)


def _log_sigmoid(x):
    return jnp.minimum(x, 0.0) - jnp.log1p(jnp.exp(-jnp.abs(x)))


def _cummax_rows(x):
    n = x.shape[0]
    rid = lax.broadcasted_iota(jnp.int32, x.shape, 0)
    sh = 1
    while sh < n:
        x = jnp.maximum(x, jnp.where(rid >= sh, pltpu.roll(x, sh, axis=0), -jnp.inf))
        sh *= 2
    return x


def _cumsum_rows(x):
    n = x.shape[0]
    rid = lax.broadcasted_iota(jnp.int32, x.shape, 0)
    sh = 1
    while sh < n:
        x = x + jnp.where(rid >= sh, pltpu.roll(x, sh, axis=0), 0.0)
        sh *= 2
    return x


def _layer_norm_rows(y, g, b):
    mu = jnp.mean(y, axis=-1, keepdims=True)
    d = y - mu
    var = jnp.mean(d * d, axis=-1, keepdims=True)
    return d * lax.rsqrt(var + LN_EPS) * g + b


def _pool_select(lane, a, b):
    return jnp.where((lane & (LANES - 1)) < POOL_GROUP, a, b)


def _prompt_mixer_kernel(x_ref, w_in_ref, w_out_ref, gbias_ref, a2_ref, abias_ref, mnorm_t_ref,
                         gnorm_ref, pmix_ref, pscale_ref, lng_ref, lnb_ref,
                         xo_ref, pool_o_ref, c_o_ref, n_o_ref, m_o_ref, s_o_ref,
                         z_scr, zt_scr, mix_scr, mixt_scr, ext_scr, c_scr, m_scr, s_scr, lvl_scr, tr_scr):
    ti = pl.program_id(1)
    n_t = pl.num_programs(1)

    @pl.when(ti == 0)
    def _():
        ext_scr[0:2 * SUBLANES, :] = jnp.zeros((2 * SUBLANES, D_POOL), F32)
        c_scr[...] = jnp.zeros(c_scr.shape, F32)
        m_scr[...] = jnp.zeros(m_scr.shape, F32)
        s_scr[...] = jnp.zeros(s_scr.shape, F32)

    xb = x_ref[0].astype(BF16)
    for lo, hi in ((U_OFF, MQ_OFF), (MK_OFF, MV_OFF), (GQ_OFF, N_Z)):
        for c0 in range(lo, hi, 512):
            c1 = min(c0 + 512, hi)
            z_scr[:, c0:c1] = _dot_nt(xb, w_in_ref[c0:c1, :])
    for src, dst in ((MQ_OFF, ZT_Q), (MV_OFF, ZT_V), (MO_OFF, ZT_O)):
        zt_scr[dst:dst + HEADS * FH, :] = _dot_nt(w_in_ref[src:src + HEADS * FH, :], xb)

    hist = 2 * SUBLANES
    u = z_scr[:, U_OFF:U_OFF + D_POOL]
    ext_scr[hist:hist + TT, :] = u
    ext = ext_scr[...]
    e0 = ext[:, 0:LANES]
    e1 = ext[:, LANES:2 * LANES]
    a2s = e0 + pltpu.roll(e0, 1, axis=0)
    a4s = a2s + pltpu.roll(a2s, 2, axis=0)
    b2s = e1 + pltpu.roll(e1, 1, axis=0)
    b4s = b2s + pltpu.roll(b2s, 2, axis=0)
    b8s = b4s + pltpu.roll(b4s, 4, axis=0)
    b16s = b8s + pltpu.roll(b8s, 8, axis=0)
    lane = lax.broadcasted_iota(jnp.int32, (TT, LANES), 1)
    pos1 = (lax.broadcasted_iota(jnp.int32, (TT, LANES), 0) + ti * TT + 1).astype(F32)
    ws0 = _pool_select(lane, a2s[hist:], a4s[hist:])
    ws1 = _pool_select(lane, b8s[hist:], b16s[hist:])
    cnt0 = jnp.minimum(_pool_select(lane, 2.0, 4.0), pos1)
    cnt1 = jnp.minimum(_pool_select(lane, 8.0, 16.0), pos1)
    d0 = ws0 / cnt0 - u[:, 0:LANES]
    d1 = ws1 / cnt1 - u[:, LANES:]
    dpool = jnp.concatenate([d0, d1], axis=1).astype(BF16)
    y_pool = _dot(dpool, pmix_ref[...]) * pscale_ref[...]
    mix_scr[:, 0:D_POOL] = y_pool.astype(BF16)
    tail = ext_scr[TT:TT + hist, :]
    ext_scr[0:hist, :] = tail

    @pl.when(ti == n_t - 1)
    def _():
        pool_o_ref[0] = ext_scr[1:hist, :]

    gbias = gbias_ref[...]
    abias = abias_ref[...]
    trow = lax.broadcasted_iota(jnp.int32, (HEADS * CHUNK, CHUNK), 0) & (CHUNK - 1)
    scol = lax.broadcasted_iota(jnp.int32, (HEADS * CHUNK, CHUNK), 1)
    differ = trow ^ scol
    level = jnp.where(scol > trow, -2, -1)
    for bit in range(CHUNK.bit_length() - 1):
        level = level + jnp.where((scol < trow) & (differ >= (1 << bit)), 1, 0)
    lvl_scr[...] = level

    def chunk_body(c):
        rows = pl.ds(c * CHUNK, CHUNK)
        graw = z_scr[rows, GATE_OFF:GATE_OFF + LANES]

        ya = _dot(graw.astype(BF16), a2_ref[...]) + abias
        la = _log_sigmoid(ya) * (1.0 / GLA_TAU)
        bc = _cumsum_rows(la)
        gq = z_scr[rows, GQ_OFF:GQ_OFF + HEADS * KP] * (GLA_DK ** -0.5)
        gk = z_scr[rows, GK_OFF:GK_OFF + HEADS * KP]
        width = HEADS * KP
        t_id = lax.broadcasted_iota(jnp.int32, (CHUNK, width), 0)
        lane_head = lax.broadcasted_iota(jnp.int32, (CHUNK, width), 1) // KP

        def stack_heads(a):
            return jnp.concatenate([jnp.where(lane_head == h, a, jnp.zeros_like(a)) for h in range(HEADS)], axis=0)

        amat = jnp.where(lvl_scr[...] == -1, _dot_nt(stack_heads(gq.astype(BF16)), gk.astype(BF16)), 0.0)
        half = CHUNK // 2
        while half >= 1:
            blk = 2 * half
            upper = (t_id & half) != 0
            if blk >= SUBLANES:
                ref_rows = jnp.concatenate(
                    [jnp.broadcast_to(bc[b0 + half - 1:b0 + half, :], (blk, width))
                     for b0 in range(0, CHUNK, blk)], axis=0)
            else:
                off = t_id & (blk - 1)
                ref_rows = bc
                for o in range(blk):
                    sh = (o - (half - 1)) % CHUNK
                    if sh != 0:
                        ref_rows = jnp.where(off == o, pltpu.roll(bc, sh, axis=0), ref_rows)
            dlt = bc - ref_rows
            fac = (jnp.where(upper, gq, gk) * jnp.exp(jnp.where(upper, dlt, -dlt))).astype(BF16)
            amat = jnp.where(lvl_scr[...] == half.bit_length() - 1, _dot_nt(stack_heads(fac), fac), amat)
            half //= 2

        o_inter = _dot(stack_heads((gq * jnp.exp(bc)).astype(BF16)), s_scr[...].astype(BF16))
        bc_t = jnp.transpose(bc)
        gk_t = jnp.transpose(gk)
        last_t = bc_t[:, CHUNK - 1:CHUNK]
        ke_t = (gk_t * jnp.exp(last_t - bc_t)).astype(BF16)
        s_dec = jnp.exp(last_t)
        for h in range(HEADS):
            hr = slice(h * CHUNK, (h + 1) * CHUNK)
            gv = z_scr[rows, GV_OFF + h * HP:GV_OFF + (h + 1) * HP].astype(BF16)
            o = o_inter[hr] + _dot(amat[hr].astype(BF16), gv)
            ms = jnp.sum(o * o, axis=1, keepdims=True) * (1.0 / GLA_DV)
            on = o * lax.rsqrt(ms + LN_EPS) * gnorm_ref[:, h * HP:(h + 1) * HP]
            gg = z_scr[rows, GG_OFF + h * HP:GG_OFF + (h + 1) * HP]
            mix_scr[rows, YG_OFF + h * HP:YG_OFF + (h + 1) * HP] = (gg * jax.nn.sigmoid(gg) * on).astype(BF16)
            kr = slice(h * KP, (h + 1) * KP)
            s_scr[kr, :] = s_dec[kr] * s_scr[kr, :] + _dot(ke_t[kr], gv)

    causal = (lax.broadcasted_iota(jnp.int32, (MCHUNK, MCHUNK), 0)
              <= lax.broadcasted_iota(jnp.int32, (MCHUNK, MCHUNK), 1))
    ones_rows = jnp.where(lax.broadcasted_iota(jnp.int32, (SUBLANES, MCHUNK), 0) == 0, 1.0, 0.0)

    def mchunk_body(p):
        for cc in range(MCHUNK // CHUNK):
            chunk_body(p * (MCHUNK // CHUNK) + cc)
        lanes = pl.ds(p * MCHUNK, MCHUNK)
        rows = pl.ds(p * MCHUNK, MCHUNK)
        g_r = z_scr[rows, GATE_OFF:GATE_OFF + LANES] + gbias
        b = _cumsum_rows(_log_sigmoid(g_r))
        i_sh = pltpu.roll(g_r, GATE_F - GATE_I, axis=1)
        r_r = i_sh - b
        m_in = m_scr[0:1, :]
        inter = b + m_in
        m_t = jnp.maximum(inter, b + _cummax_rows(r_r))
        b_end = b[MCHUNK - 1:MCHUNK, :]
        m_end = m_t[MCHUNK - 1:MCHUNK, :]
        m_scr[0:1, :] = m_end
        a = jnp.transpose(b - m_t)
        w_inter = jnp.transpose(jnp.exp(inter - m_t))
        floor = jnp.transpose(jnp.exp(-m_t))
        w_end = jnp.transpose(jnp.exp(b_end - b + i_sh - m_end))
        decay = jnp.exp(b_end + m_in - m_end)
        for h in range(HEADS):
            gl = GATE_F + h
            fr = slice(h * FH, (h + 1) * FH)
            q_t = (zt_scr[ZT_Q + h * FH:ZT_Q + (h + 1) * FH, lanes] * (MLSTM_DH ** -0.5)).astype(BF16)
            v_aug = jnp.concatenate([zt_scr[ZT_V + h * FH:ZT_V + (h + 1) * FH, lanes], ones_rows], axis=0)
            k_r = z_scr[rows, MK_OFF + h * HP:MK_OFF + h * HP + FH].astype(BF16)
            e_t = jnp.exp(jnp.where(causal, a[gl:gl + 1, :] + r_r[:, gl:gl + 1], -jnp.inf))
            s_t = (_dot(k_r, q_t) * e_t).astype(BF16)
            ct = c_scr[h]
            c_scr[h] = decay[:, gl:gl + 1] * ct + _dot((v_aug * w_end[gl:gl + 1, :]).astype(BF16), k_r)
            num = (w_inter[gl:gl + 1, :] * _dot(ct.astype(BF16), q_t)
                   + _dot(v_aug.astype(BF16), s_t))
            hval = num[0:FH] / jnp.maximum(jnp.abs(num[FH:FH + 1, :]), floor[gl:gl + 1, :])
            mu = jnp.mean(hval, axis=0, keepdims=True)
            dlt = hval - mu
            var = jnp.mean(dlt * dlt, axis=0, keepdims=True)
            mo_t = zt_scr[ZT_O + h * FH:ZT_O + (h + 1) * FH, lanes]
            mixt_scr[fr, lanes] = (jax.nn.sigmoid(mo_t) * dlt * lax.rsqrt(var + LN_EPS)
                                   * mnorm_t_ref[fr, :]).astype(BF16)

    for p in range(TT // MCHUNK):
        mchunk_body(p)

    y = (ALPHA * x_ref[0] + _dot(mix_scr[:, 0:YM_OFF], w_out_ref[0:YM_OFF, :])
         + _dot_tn(mixt_scr[...], w_out_ref[YM_OFF:YG_OFF, :])
         + _dot(mix_scr[:, YG_OFF:N_MIX], w_out_ref[YG_OFF:N_MIX, :]))
    xo_ref[0] = _layer_norm_rows(y, lng_ref[...], lnb_ref[...])

    @pl.when(ti == n_t - 1)
    def _():
        for h in range(HEADS):
            ct = c_scr[h]
            tr_scr[...] = jnp.zeros(tr_scr.shape, F32)
            tr_scr[0:FH, 0:FH] = ct[0:FH, :]
            c_o_ref[0, h] = jnp.transpose(tr_scr[...])[0:FH, 0:FH]
            n_o_ref[0, h:h + 1, :] = ct[FH:FH + 1, :]
            s_o_ref[0, h] = s_scr[h * KP:h * KP + GLA_DK, 0:GLA_DV]
        m_o_ref[0] = m_scr[...]


def _layer_spec(shape, layer):
    nd = len(shape)
    return pl.BlockSpec((None,) + tuple(shape[1:]), lambda *_: (layer,) + (0,) * (nd - 1),
                        pipeline_mode=pl.Buffered(1))


def _prompt_mixer(x, consts, layer):
    B, T, _ = x.shape
    out_shape = (
        jax.ShapeDtypeStruct((B, T, D_MODEL), F32),
        jax.ShapeDtypeStruct((B, POOL_BUF, D_POOL), F32),
        jax.ShapeDtypeStruct((B, HEADS, MLSTM_DH, MLSTM_DH), F32),
        jax.ShapeDtypeStruct((B, HEADS, FH), F32),
        jax.ShapeDtypeStruct((B, SUBLANES, LANES), F32),
        jax.ShapeDtypeStruct((B, HEADS, GLA_DK, GLA_DV), F32),
    )
    out_specs = (
        pl.BlockSpec((1, TT, D_MODEL), lambda b, t: (b, t, 0)),
        pl.BlockSpec((1, POOL_BUF, D_POOL), lambda b, t: (b, 0, 0)),
        pl.BlockSpec((1, HEADS, MLSTM_DH, MLSTM_DH), lambda b, t: (b, 0, 0, 0)),
        pl.BlockSpec((1, HEADS, FH), lambda b, t: (b, 0, 0)),
        pl.BlockSpec((1, SUBLANES, LANES), lambda b, t: (b, 0, 0)),
        pl.BlockSpec((1, HEADS, GLA_DK, GLA_DV), lambda b, t: (b, 0, 0, 0)),
    )
    return pl.pallas_call(
        _prompt_mixer_kernel,
        grid=(B, T // TT),
        in_specs=[pl.BlockSpec((1, TT, D_MODEL), lambda b, t: (b, t, 0))]
        + [_layer_spec(c.shape, layer) for c in consts],
        out_specs=out_specs,
        out_shape=out_shape,
        scratch_shapes=[
            pltpu.VMEM((TT, N_Z), F32),
            pltpu.VMEM((ZT_ROWS, TT), F32),
            pltpu.VMEM((TT, N_MIX), BF16),
            pltpu.VMEM((HEADS * FH, TT), BF16),
            pltpu.VMEM((TT + 2 * SUBLANES, D_POOL), F32),
            pltpu.VMEM((HEADS, FH + SUBLANES, FH), F32),
            pltpu.VMEM((SUBLANES, LANES), F32),
            pltpu.VMEM((HEADS * KP, HP), F32),
            pltpu.VMEM((HEADS * CHUNK, CHUNK), jnp.int32),
            pltpu.VMEM((LANES, LANES), F32),
        ],
        compiler_params=pltpu.CompilerParams(
            dimension_semantics=("arbitrary", "arbitrary"), vmem_limit_bytes=VMEM_LIMIT),
        name="prompt_mixer",
    )(x, *consts)


K_SPLIT = 2
CK = MLSTM_DH // K_SPLIT
SK = GLA_DK // K_SPLIT


def _row_of(a, idx):
    sub = lax.broadcasted_iota(jnp.int32, a.shape, 0)
    return jnp.sum(jnp.where(sub == idx, a, 0.0), axis=0, keepdims=True)


def _sample_step_kernel(n_carried, *refs):
    (x_ref, pool_ref, c_ref, n_ref, m_ref, s_ref,
     w_in_ref, w_out_ref, gbias_ref, a2t_ref, abias_ref, mnorm_ref, gnorm_ref,
     pmix_ref, pscale_ref, lng_ref, lnb_ref) = refs[:17]
    (xo_ref, pool_o_ref, c_o_ref, n_o_ref, m_o_ref, s_o_ref,
     zt_scr, mixt_scr, gate_scr, ea_scr, num_scr, oin_scr) = refs[17 + n_carried:]
    h = pl.program_id(0)
    kh = pl.program_id(1)
    B = x_ref.shape[0]
    first = jnp.logical_and(h == 0, kh == 0)
    last = jnp.logical_and(h == HEADS - 1, kh == K_SPLIT - 1)

    @pl.when(first)
    def _():
        mixt_scr[...] = jnp.zeros(mixt_scr.shape, F32)
        xb = x_ref[...].astype(BF16)
        for c0 in range(0, N_Z, LANES):
            zt_scr[c0:c0 + LANES, :] = _dot_nt(w_in_ref[c0:c0 + LANES, :], xb)
        g = zt_scr[GATE_OFF:GATE_OFF + LANES, :]
        gb = g + gbias_ref[...]
        log_i = gb[GATE_I:GATE_I + HEADS]
        log_f = _log_sigmoid(gb[GATE_F:GATE_F + HEADS])
        inter = log_f + m_ref[...]
        m_t = jnp.maximum(inter, log_i)
        m_o_ref[...] = m_t
        gate_scr[0:HEADS, :] = jnp.exp(inter - m_t)
        gate_scr[HEADS:2 * HEADS, :] = jnp.exp(log_i - m_t)
        gate_scr[2 * HEADS:3 * HEADS, :] = jnp.exp(-m_t)
        ya = _dot(a2t_ref[...], g.astype(BF16)) + abias_ref[...]
        ea_scr[...] = jnp.exp(_log_sigmoid(ya) * (1.0 / GLA_TAU))

    gates = gate_scr[...]
    dec = _row_of(gates, h)
    wig = _row_of(gates, h + HEADS)
    floor = _row_of(gates, h + 2 * HEADS)

    def head_rows(off, width, count, start=0):
        return zt_scr[pl.ds(pl.multiple_of(off + h * width + start, SUBLANES), count), :]

    k0 = kh * CK
    q_sl = head_rows(MQ_OFF, FH, CK, k0) * (MLSTM_DH ** -0.5)
    k_sl = head_rows(MK_OFF, HP, CK, k0) * wig
    v_t = head_rows(MV_OFF, FH, MLSTM_DH)
    acc = jnp.zeros((MLSTM_DH, B), F32)
    for k in range(CK):
        c_old = c_ref[k]
        acc = acc + q_sl[k:k + 1, :] * c_old
        c_o_ref[k] = dec * c_old + k_sl[k:k + 1, :] * v_t

    @pl.when(kh == 0)
    def _():
        num_scr[...] = acc

    @pl.when(kh > 0)
    def _():
        num_scr[...] += acc

    @pl.when(kh == K_SPLIT - 1)
    def _():
        q_t = head_rows(MQ_OFF, FH, MLSTM_DH) * (MLSTM_DH ** -0.5)
        k_t = head_rows(MK_OFF, HP, MLSTM_DH)
        n_old = n_ref[...]
        sc = jnp.sum(q_t * k_t, axis=0, keepdims=True) * wig
        num = dec * num_scr[...] + sc * v_t
        nq = dec * jnp.sum(q_t * n_old, axis=0, keepdims=True) + sc
        hval = num / jnp.maximum(jnp.abs(nq), floor)
        n_o_ref[...] = dec * n_old + wig * k_t
        mu = jnp.mean(hval, axis=0, keepdims=True)
        dlt = hval - mu
        var = jnp.mean(dlt * dlt, axis=0, keepdims=True)
        w_norm = mnorm_ref[pl.ds(pl.multiple_of(h * FH, SUBLANES), MLSTM_DH), :]
        mo = head_rows(MO_OFF, FH, MLSTM_DH)
        mixt_scr[pl.ds(pl.multiple_of(YM_OFF + h * FH, SUBLANES), MLSTM_DH), :] = (
            jax.nn.sigmoid(mo) * dlt * lax.rsqrt(var + LN_EPS) * w_norm)

    s0 = kh * SK
    ea_sl = ea_scr[pl.ds(pl.multiple_of(h * KP + s0, SUBLANES), SK), :]
    gq_sl = head_rows(GQ_OFF, KP, SK, s0) * (GLA_DK ** -0.5)
    gk_sl = head_rows(GK_OFF, KP, SK, s0)
    gv_t = head_rows(GV_OFF, HP, GLA_DV)
    gqe_sl = gq_sl * ea_sl
    oacc = jnp.zeros((GLA_DV, B), F32)
    for k in range(SK):
        s_old = s_ref[k]
        oacc = oacc + gqe_sl[k:k + 1, :] * s_old
        s_o_ref[k] = ea_sl[k:k + 1, :] * s_old + gk_sl[k:k + 1, :] * gv_t

    @pl.when(kh == 0)
    def _():
        oin_scr[...] = oacc

    @pl.when(kh > 0)
    def _():
        oin_scr[...] += oacc

    @pl.when(kh == K_SPLIT - 1)
    def _():
        gq_t = head_rows(GQ_OFF, KP, GLA_DK) * (GLA_DK ** -0.5)
        gk_t = head_rows(GK_OFF, KP, GLA_DK)
        qk = jnp.sum(gq_t * gk_t, axis=0, keepdims=True)
        o = oin_scr[...] + qk * gv_t
        ms = jnp.mean(o * o, axis=0, keepdims=True)
        w_norm = gnorm_ref[pl.ds(pl.multiple_of(h * HP, SUBLANES), GLA_DV), :]
        gg = head_rows(GG_OFF, HP, GLA_DV)
        mixt_scr[pl.ds(pl.multiple_of(YG_OFF + h * HP, SUBLANES), GLA_DV), :] = (
            gg * jax.nn.sigmoid(gg) * o * lax.rsqrt(ms + LN_EPS) * w_norm)

    @pl.when(last)
    def _():
        u = jnp.concatenate([jnp.transpose(zt_scr[c0:c0 + LANES, :]) for c0 in range(0, D_POOL, LANES)], axis=1)
        run = u
        sums = {}
        for j in range(1, max(POOL_WINDOWS)):
            run = run + pool_ref[POOL_BUF - j]
            if j + 1 in POOL_WINDOWS:
                sums[j + 1] = run
        grp = lax.broadcasted_iota(jnp.int32, (B, D_POOL), 1) // POOL_GROUP
        wmean = jnp.zeros((B, D_POOL), F32)
        for gidx, w in enumerate(POOL_WINDOWS):
            wmean = jnp.where(grp == gidx, sums[w] / float(min(w, PAST_LEN + 1)), wmean)
        y_pool = _dot((wmean - u).astype(BF16), pmix_ref[...]) * pscale_ref[...]
        for j in range(POOL_BUF - 1):
            pool_o_ref[j] = pool_ref[j + 1]
        pool_o_ref[POOL_BUF - 1] = u
        mix = jnp.concatenate(
            [y_pool] + [jnp.transpose(mixt_scr[c0:c0 + LANES, :]) for c0 in range(YM_OFF, N_MIX, LANES)], axis=1)
        y = ALPHA * x_ref[...] + _dot(mix.astype(BF16), w_out_ref[...])
        xo_ref[...] = _layer_norm_rows(y, lng_ref[...], lnb_ref[...])


def _sample_step(x, states, consts, carried, layer):
    B = x.shape[0]
    out_shape = (jax.ShapeDtypeStruct((B, D_MODEL), F32),) + tuple(
        jax.ShapeDtypeStruct(s.shape, F32) for s in states)
    state_specs = [
        pl.BlockSpec((None, POOL_BUF, B, D_POOL), lambda h, k: (layer, 0, 0, 0)),
        pl.BlockSpec((None, None, CK, MLSTM_DH, B), lambda h, k: (layer, h, k, 0, 0)),
        pl.BlockSpec((None, None, MLSTM_DH, B), lambda h, k: (layer, h, 0, 0)),
        pl.BlockSpec((None, HEADS, B), lambda h, k: (layer, 0, 0)),
        pl.BlockSpec((None, None, SK, GLA_DV, B), lambda h, k: (layer, h, k, 0, 0)),
    ]
    x_spec = pl.BlockSpec((B, D_MODEL), lambda h, k: (0, 0))
    n_in = 1 + len(states) + len(consts)
    return pl.pallas_call(
        functools.partial(_sample_step_kernel, len(carried)),
        grid=(HEADS, K_SPLIT),
        in_specs=[x_spec] + state_specs + [_layer_spec(cst.shape, layer) for cst in consts]
        + [pl.BlockSpec(memory_space=pl.ANY)] * len(carried),
        out_specs=[x_spec] + state_specs,
        out_shape=out_shape,
        input_output_aliases={n_in + i: 1 + i for i in range(len(carried))},
        scratch_shapes=[
            pltpu.VMEM((N_Z, B), F32),
            pltpu.VMEM((N_MIX, B), F32),
            pltpu.VMEM((2 * SUBLANES, B), F32),
            pltpu.VMEM((HEADS * KP, B), F32),
            pltpu.VMEM((MLSTM_DH, B), F32),
            pltpu.VMEM((GLA_DV, B), F32),
        ],
        compiler_params=pltpu.CompilerParams(
            dimension_semantics=("arbitrary", "arbitrary"), vmem_limit_bytes=VMEM_LIMIT),
        name="sample_step",
    )(x, *states, *consts, *carried)


def _ffn_kernel(x_ref, p_ref, wg_ref, wu_ref, wd_ref, wp_ref, wpg_ref, lng_ref, lnb_ref, o_ref):
    x = x_ref[...]
    xb = x.astype(BF16)
    acc = ALPHA * x + _dot(p_ref[...].astype(BF16), wp_ref[...]) * jax.nn.sigmoid(_dot(xb, wpg_ref[...]))
    for c0 in range(0, D_FF, FF_CHUNK):
        c1 = min(c0 + FF_CHUNK, D_FF)
        gate = _dot(xb, wg_ref[:, c0:c1])
        up = _dot(xb, wu_ref[:, c0:c1])
        hid = (gate * jax.nn.sigmoid(gate) * up).astype(BF16)
        acc = acc + _dot(hid, wd_ref[c0:c1, :])
    o_ref[...] = _layer_norm_rows(acc, lng_ref[...], lnb_ref[...])


def _ffn(x, p, consts, layer, tm):
    M = x.shape[0]
    return pl.pallas_call(
        _ffn_kernel,
        grid=(M // tm,),
        in_specs=[pl.BlockSpec((tm, D_MODEL), lambda i: (i, 0)),
                  pl.BlockSpec((None, tm, D_PLE), lambda i: (layer, i, 0))]
        + [_layer_spec(c.shape, layer) for c in consts],
        out_specs=pl.BlockSpec((tm, D_MODEL), lambda i: (i, 0)),
        out_shape=jax.ShapeDtypeStruct((M, D_MODEL), F32),
        compiler_params=pltpu.CompilerParams(
            dimension_semantics=("arbitrary",), vmem_limit_bytes=VMEM_LIMIT),
        name="ffn",
    )(x, p, *consts)


def kernel(x_prompt, x_sample, p_prompt, p_sample, state_pool, state_mlstm_C, state_mlstm_n, state_mlstm_m, state_gla_S, w_in, mlstm_i_bias, mlstm_f_bias, mlstm_norm_w, gla_a2, gla_a_bias, gla_norm_w, pool_mix, pool_scale, w_out, ln1_g, ln1_b, w_gate, w_up, w_down, w_ple, w_ple_gate, ln2_g, ln2_b):
    Bp, Tp, _ = x_prompt.shape
    Bs = x_sample.shape[0]

    w_in_p = _pad_in_proj_t(w_in).astype(BF16)
    w_out_p = _pad_out_proj(w_out).astype(BF16)
    gbias = jnp.zeros((DEPTH, 1, LANES), F32)
    gbias = gbias.at[:, 0, GATE_I:GATE_I + HEADS].set(mlstm_i_bias.astype(F32))
    gbias = gbias.at[:, 0, GATE_F:GATE_F + HEADS].set(mlstm_f_bias.astype(F32))
    a2_p = jnp.zeros((DEPTH, LANES, HEADS * KP), F32)
    a2_p = a2_p.at[:, GATE_A:GATE_A + GLA_RANK, :].set(_pad_heads(gla_a2.astype(F32), GLA_DK, KP)).astype(BF16)
    abias_p = _pad_heads(gla_a_bias.astype(F32), GLA_DK, KP)[:, None, :]
    mnorm_r = mlstm_norm_w.astype(F32)[:, None, :]
    gnorm_p = _pad_heads(gla_norm_w.astype(F32), GLA_DV, HP)[:, None, :]
    n_grp = len(POOL_WINDOWS)
    eye = jnp.eye(n_grp, dtype=F32)
    pmix_bd = (pool_mix.astype(F32)[:, :, :, None, :] * eye[None, :, None, :, None]).reshape(
        DEPTH, D_POOL, D_POOL).astype(BF16)
    pscale = pool_scale.astype(F32)[:, None, :]
    row = lambda a: a.astype(F32)[:, None, :]
    ln1_g_r, ln1_b_r, ln2_g_r, ln2_b_r = row(ln1_g), row(ln1_b), row(ln2_g), row(ln2_b)
    wg_b, wu_b, wd_b = w_gate.astype(BF16), w_up.astype(BF16), w_down.astype(BF16)
    wp_b, wpg_b = w_ple.astype(BF16), w_ple_gate.astype(BF16)

    mnorm_t = jnp.broadcast_to(jnp.transpose(mnorm_r, (0, 2, 1)), (DEPTH, HEADS * FH, MCHUNK))
    mixer_consts = (w_in_p, w_out_p, gbias, a2_p, abias_p, mnorm_t, gnorm_p, pmix_bd, pscale, ln1_g_r, ln1_b_r)
    ffn_consts = (wg_b, wu_b, wd_b, wp_b, wpg_b, ln2_g_r, ln2_b_r)
    xp = x_prompt
    xs = x_sample.reshape(Bs, D_MODEL)
    pp = p_prompt.reshape(DEPTH, Bp * Tp, D_PLE)
    ps = p_sample.reshape(DEPTH, Bs, D_PLE)
    sample_states = (jnp.transpose(state_pool, (0, 2, 1, 3)), jnp.transpose(state_mlstm_C, (0, 2, 3, 4, 1)),
                     jnp.transpose(state_mlstm_n, (0, 2, 3, 1)), jnp.transpose(state_mlstm_m, (0, 2, 1)),
                     jnp.transpose(state_gla_S, (0, 2, 3, 4, 1)))
    col = lambda a: jnp.transpose(a, (0, 2, 1))
    sample_consts = (w_in_p, w_out_p, col(gbias), col(a2_p), col(abias_p), col(mnorm_r), col(gnorm_p),
                     pmix_bd, pscale, ln1_g_r, ln1_b_r)
    outs_p = []
    carried = ()
    for l in range(DEPTH):
        xp, pool_p, c_p, n_p, m_p, s_p = _prompt_mixer(xp, mixer_consts, l)
        xp = _ffn(xp.reshape(Bp * Tp, D_MODEL), pp, ffn_consts, l, tm=512).reshape(Bp, Tp, D_MODEL)
        outs_p.append((pool_p, c_p, n_p, m_p[:, 0, GATE_F:GATE_F + HEADS], s_p))
        xs, *carried = _sample_step(xs, sample_states, sample_consts, tuple(carried), l)
        xs = _ffn(xs, ps, ffn_consts, l, tm=Bs)

    pool_s, c_s, n_s, m_s, s_s = carried
    stack = lambda j: jnp.stack([o[j] for o in outs_p], axis=0)
    return ((xp, xs.reshape(Bs, 1, D_MODEL)) + tuple(stack(j) for j in range(5))
            + (jnp.transpose(pool_s, (0, 2, 1, 3)), jnp.transpose(c_s, (0, 4, 1, 2, 3)),
               jnp.transpose(n_s, (0, 3, 1, 2)), jnp.transpose(m_s, (0, 2, 1)),
               jnp.transpose(s_s, (0, 4, 1, 2, 3))))
```

```python
import functools

import numpy as np
import jax
import jax.numpy as jnp
from jax import lax
from jax.experimental import pallas as pl
from jax.experimental.pallas import tpu as pltpu

F32 = jnp.float32
BF16 = jnp.bfloat16

D_MODEL = 1024
DEPTH = 4
PAST_LEN = 16384
D_POOL = 256
POOL_WINDOWS = (2, 4, 8, 16)
POOL_GROUP = 64
POOL_BUF = 15
D_MLSTM = 384
HEADS = 4
MLSTM_DH = 96
D_GLA = 384
GLA_DV = 96
GLA_DK = 48
GLA_RANK = 16
GLA_TAU = 16.0
D_MIX = 1024
D_FF = 2816
D_PLE = 256
ALPHA = (2 * DEPTH) ** 0.25
LN_EPS = 1e-5

LANES = 128
SUBLANES = 8
VMEM_LIMIT = 56 * 1024 * 1024

HP = LANES
FH = MLSTM_DH
KP = 64
U_OFF = 0
MQ_OFF = U_OFF + D_POOL
MK_OFF = MQ_OFF + HEADS * FH
MV_OFF = MK_OFF + HEADS * HP
MO_OFF = MV_OFF + HEADS * FH
GQ_OFF = MO_OFF + HEADS * FH
GK_OFF = GQ_OFF + HEADS * KP
GV_OFF = GK_OFF + HEADS * KP
GG_OFF = GV_OFF + HEADS * HP
GATE_OFF = GG_OFF + HEADS * HP
N_Z = GATE_OFF + LANES
GATE_I = 0
GATE_F = 4
GATE_A = 8
YM_OFF = D_POOL
YG_OFF = YM_OFF + HEADS * FH
N_MIX = YG_OFF + HEADS * HP

TT = 512
CHUNK = 128
MCHUNK = 256
ZT_Q = 0
ZT_V = ZT_Q + HEADS * FH
ZT_O = ZT_V + HEADS * FH
ZT_ROWS = ZT_O + HEADS * FH
FF_CHUNK = 256


def _pad_heads(a, width, padded):
    lead = a.shape[:-1]
    a = a.reshape(lead + (HEADS, width))
    a = jnp.pad(a, [(0, 0)] * len(lead) + [(0, 0), (0, padded - width)])
    return a.reshape(lead + (HEADS * padded,))


def _pad_head_rows(a, width, padded):
    L, _, K = a.shape
    a = jnp.pad(a.reshape(L, HEADS, width, K), ((0, 0), (0, 0), (0, padded - width), (0, 0)))
    return a.reshape(L, HEADS * padded, K)


def _pad_in_proj_t(w_in):
    sizes = (D_POOL, D_MLSTM, D_MLSTM, D_MLSTM, HEADS, HEADS, D_MLSTM,
             HEADS * GLA_DK, HEADS * GLA_DK, D_GLA, GLA_RANK, D_GLA)
    w_t = jnp.transpose(w_in, (0, 2, 1))
    u, mq, mk, mv, mi, mf, mo, gq, gk, gv, ga, gg = jnp.split(w_t, np.cumsum(sizes)[:-1].tolist(), axis=1)
    gate_pad = jnp.zeros((w_t.shape[0], LANES - GATE_A - GLA_RANK, w_t.shape[2]), w_t.dtype)
    wide = lambda a: _pad_head_rows(a, MLSTM_DH, HP)
    narrow = lambda a: _pad_head_rows(a, GLA_DK, KP)
    return jnp.concatenate([u, mq, wide(mk), mv, mo, narrow(gq), narrow(gk), wide(gv),
                            wide(gg), mi, mf, ga, gate_pad], axis=1)


def _pad_out_proj(w_out):
    dense = w_out[:, :D_POOL + D_MLSTM]
    heads = w_out[:, D_POOL + D_MLSTM:].reshape(w_out.shape[0], HEADS, GLA_DV, D_MODEL)
    heads = jnp.pad(heads, ((0, 0), (0, 0), (0, HP - GLA_DV), (0, 0)))
    return jnp.concatenate([dense, heads.reshape(w_out.shape[0], HEADS * HP, D_MODEL)], axis=1)


def _dot(a, b):
    return jnp.dot(a, b, preferred_element_type=F32)


def _dot_nt(a, b):
    return lax.dot_general(a, b, (((1,), (1,)), ((), ())), preferred_element_type=F32)


def _dot_tn(a, b):
    return lax.dot_general(a, b, (((0,), (0,)), ((), ())), preferred_element_type=F32)


def _log_sigmoid(x):
    return jnp.minimum(x, 0.0) - jnp.log1p(jnp.exp(-jnp.abs(x)))


def _cummax_rows(x):
    n = x.shape[0]
    rid = lax.broadcasted_iota(jnp.int32, x.shape, 0)
    sh = 1
    while sh < n:
        x = jnp.maximum(x, jnp.where(rid >= sh, pltpu.roll(x, sh, axis=0), -jnp.inf))
        sh *= 2
    return x


def _cumsum_rows(x):
    n = x.shape[0]
    rid = lax.broadcasted_iota(jnp.int32, x.shape, 0)
    sh = 1
    while sh < n:
        x = x + jnp.where(rid >= sh, pltpu.roll(x, sh, axis=0), 0.0)
        sh *= 2
    return x


def _layer_norm_rows(y, g, b):
    mu = jnp.mean(y, axis=-1, keepdims=True)
    d = y - mu
    var = jnp.mean(d * d, axis=-1, keepdims=True)
    return d * lax.rsqrt(var + LN_EPS) * g + b


def _pool_select(lane, a, b):
    return jnp.where((lane & (LANES - 1)) < POOL_GROUP, a, b)


def _prompt_mixer_kernel(x_ref, w_in_ref, w_out_ref, gbias_ref, a2_ref, abias_ref, mnorm_t_ref,
                         gnorm_ref, pmix_ref, pscale_ref, lng_ref, lnb_ref,
                         xo_ref, pool_o_ref, c_o_ref, n_o_ref, m_o_ref, s_o_ref,
                         z_scr, zt_scr, mix_scr, mixt_scr, ext_scr, c_scr, m_scr, s_scr, lvl_scr, tr_scr):
    ti = pl.program_id(1)
    n_t = pl.num_programs(1)

    @pl.when(ti == 0)
    def _():
        ext_scr[0:2 * SUBLANES, :] = jnp.zeros((2 * SUBLANES, D_POOL), F32)
        c_scr[...] = jnp.zeros(c_scr.shape, F32)
        m_scr[...] = jnp.zeros(m_scr.shape, F32)
        s_scr[...] = jnp.zeros(s_scr.shape, F32)

    xb = x_ref[0].astype(BF16)
    for lo, hi in ((U_OFF, MQ_OFF), (MK_OFF, MV_OFF), (GQ_OFF, N_Z)):
        for c0 in range(lo, hi, 512):
            c1 = min(c0 + 512, hi)
            z_scr[:, c0:c1] = _dot_nt(xb, w_in_ref[c0:c1, :])
    for src, dst in ((MQ_OFF, ZT_Q), (MV_OFF, ZT_V), (MO_OFF, ZT_O)):
        zt_scr[dst:dst + HEADS * FH, :] = _dot_nt(w_in_ref[src:src + HEADS * FH, :], xb)

    hist = 2 * SUBLANES
    u = z_scr[:, U_OFF:U_OFF + D_POOL]
    ext_scr[hist:hist + TT, :] = u
    ext = ext_scr[...]
    e0 = ext[:, 0:LANES]
    e1 = ext[:, LANES:2 * LANES]
    a2s = e0 + pltpu.roll(e0, 1, axis=0)
    a4s = a2s + pltpu.roll(a2s, 2, axis=0)
    b2s = e1 + pltpu.roll(e1, 1, axis=0)
    b4s = b2s + pltpu.roll(b2s, 2, axis=0)
    b8s = b4s + pltpu.roll(b4s, 4, axis=0)
    b16s = b8s + pltpu.roll(b8s, 8, axis=0)
    lane = lax.broadcasted_iota(jnp.int32, (TT, LANES), 1)
    pos1 = (lax.broadcasted_iota(jnp.int32, (TT, LANES), 0) + ti * TT + 1).astype(F32)
    ws0 = _pool_select(lane, a2s[hist:], a4s[hist:])
    ws1 = _pool_select(lane, b8s[hist:], b16s[hist:])
    cnt0 = jnp.minimum(_pool_select(lane, 2.0, 4.0), pos1)
    cnt1 = jnp.minimum(_pool_select(lane, 8.0, 16.0), pos1)
    d0 = ws0 / cnt0 - u[:, 0:LANES]
    d1 = ws1 / cnt1 - u[:, LANES:]
    dpool = jnp.concatenate([d0, d1], axis=1).astype(BF16)
    y_pool = _dot(dpool, pmix_ref[...]) * pscale_ref[...]
    mix_scr[:, 0:D_POOL] = y_pool.astype(BF16)
    tail = ext_scr[TT:TT + hist, :]
    ext_scr[0:hist, :] = tail

    @pl.when(ti == n_t - 1)
    def _():
        pool_o_ref[0] = ext_scr[1:hist, :]

    gbias = gbias_ref[...]
    abias = abias_ref[...]
    trow = lax.broadcasted_iota(jnp.int32, (HEADS * CHUNK, CHUNK), 0) & (CHUNK - 1)
    scol = lax.broadcasted_iota(jnp.int32, (HEADS * CHUNK, CHUNK), 1)
    differ = trow ^ scol
    level = jnp.where(scol > trow, -2, -1)
    for bit in range(CHUNK.bit_length() - 1):
        level = level + jnp.where((scol < trow) & (differ >= (1 << bit)), 1, 0)
    lvl_scr[...] = level

    def chunk_body(c):
        rows = pl.ds(c * CHUNK, CHUNK)
        graw = z_scr[rows, GATE_OFF:GATE_OFF + LANES]

        ya = _dot(graw.astype(BF16), a2_ref[...]) + abias
        la = _log_sigmoid(ya) * (1.0 / GLA_TAU)
        bc = _cumsum_rows(la)
        gq = z_scr[rows, GQ_OFF:GQ_OFF + HEADS * KP] * (GLA_DK ** -0.5)
        gk = z_scr[rows, GK_OFF:GK_OFF + HEADS * KP]
        width = HEADS * KP
        t_id = lax.broadcasted_iota(jnp.int32, (CHUNK, width), 0)
        lane_head = lax.broadcasted_iota(jnp.int32, (CHUNK, width), 1) // KP

        def stack_heads(a):
            return jnp.concatenate([jnp.where(lane_head == h, a, jnp.zeros_like(a)) for h in range(HEADS)], axis=0)

        amat = jnp.where(lvl_scr[...] == -1, _dot_nt(stack_heads(gq.astype(BF16)), gk.astype(BF16)), 0.0)
        half = CHUNK // 2
        while half >= 1:
            blk = 2 * half
            upper = (t_id & half) != 0
            if blk >= SUBLANES:
                ref_rows = jnp.concatenate(
                    [jnp.broadcast_to(bc[b0 + half - 1:b0 + half, :], (blk, width))
                     for b0 in range(0, CHUNK, blk)], axis=0)
            else:
                off = t_id & (blk - 1)
                ref_rows = bc
                for o in range(blk):
                    sh = (o - (half - 1)) % CHUNK
                    if sh != 0:
                        ref_rows = jnp.where(off == o, pltpu.roll(bc, sh, axis=0), ref_rows)
            dlt = bc - ref_rows
            fac = (jnp.where(upper, gq, gk) * jnp.exp(jnp.where(upper, dlt, -dlt))).astype(BF16)
            amat = jnp.where(lvl_scr[...] == half.bit_length() - 1, _dot_nt(stack_heads(fac), fac), amat)
            half //= 2

        o_inter = _dot(stack_heads((gq * jnp.exp(bc)).astype(BF16)), s_scr[...].astype(BF16))
        bc_t = jnp.transpose(bc)
        gk_t = jnp.transpose(gk)
        last_t = bc_t[:, CHUNK - 1:CHUNK]
        ke_t = (gk_t * jnp.exp(last_t - bc_t)).astype(BF16)
        s_dec = jnp.exp(last_t)
        for h in range(HEADS):
            hr = slice(h * CHUNK, (h + 1) * CHUNK)
            gv = z_scr[rows, GV_OFF + h * HP:GV_OFF + (h + 1) * HP].astype(BF16)
            o = o_inter[hr] + _dot(amat[hr].astype(BF16), gv)
            ms = jnp.sum(o * o, axis=1, keepdims=True) * (1.0 / GLA_DV)
            on = o * lax.rsqrt(ms + LN_EPS) * gnorm_ref[:, h * HP:(h + 1) * HP]
            gg = z_scr[rows, GG_OFF + h * HP:GG_OFF + (h + 1) * HP]
            mix_scr[rows, YG_OFF + h * HP:YG_OFF + (h + 1) * HP] = (gg * jax.nn.sigmoid(gg) * on).astype(BF16)
            kr = slice(h * KP, (h + 1) * KP)
            s_scr[kr, :] = s_dec[kr] * s_scr[kr, :] + _dot(ke_t[kr], gv)

    causal = (lax.broadcasted_iota(jnp.int32, (MCHUNK, MCHUNK), 0)
              <= lax.broadcasted_iota(jnp.int32, (MCHUNK, MCHUNK), 1))
    ones_rows = jnp.where(lax.broadcasted_iota(jnp.int32, (SUBLANES, MCHUNK), 0) == 0, 1.0, 0.0)

    def mchunk_body(p):
        for cc in range(MCHUNK // CHUNK):
            chunk_body(p * (MCHUNK // CHUNK) + cc)
        lanes = pl.ds(p * MCHUNK, MCHUNK)
        rows = pl.ds(p * MCHUNK, MCHUNK)
        g_r = z_scr[rows, GATE_OFF:GATE_OFF + LANES] + gbias
        b = _cumsum_rows(_log_sigmoid(g_r))
        i_sh = pltpu.roll(g_r, GATE_F - GATE_I, axis=1)
        r_r = i_sh - b
        m_in = m_scr[0:1, :]
        inter = b + m_in
        m_t = jnp.maximum(inter, b + _cummax_rows(r_r))
        b_end = b[MCHUNK - 1:MCHUNK, :]
        m_end = m_t[MCHUNK - 1:MCHUNK, :]
        m_scr[0:1, :] = m_end
        a = jnp.transpose(b - m_t)
        w_inter = jnp.transpose(jnp.exp(inter - m_t))
        floor = jnp.transpose(jnp.exp(-m_t))
        w_end = jnp.transpose(jnp.exp(b_end - b + i_sh - m_end))
        decay = jnp.exp(b_end + m_in - m_end)
        for h in range(HEADS):
            gl = GATE_F + h
            fr = slice(h * FH, (h + 1) * FH)
            q_t = (zt_scr[ZT_Q + h * FH:ZT_Q + (h + 1) * FH, lanes] * (MLSTM_DH ** -0.5)).astype(BF16)
            v_aug = jnp.concatenate([zt_scr[ZT_V + h * FH:ZT_V + (h + 1) * FH, lanes], ones_rows], axis=0)
            k_r = z_scr[rows, MK_OFF + h * HP:MK_OFF + h * HP + FH].astype(BF16)
            e_t = jnp.exp(jnp.where(causal, a[gl:gl + 1, :] + r_r[:, gl:gl + 1], -jnp.inf))
            s_t = (_dot(k_r, q_t) * e_t).astype(BF16)
            ct = c_scr[h]
            c_scr[h] = decay[:, gl:gl + 1] * ct + _dot((v_aug * w_end[gl:gl + 1, :]).astype(BF16), k_r)
            num = (w_inter[gl:gl + 1, :] * _dot(ct.astype(BF16), q_t)
                   + _dot(v_aug.astype(BF16), s_t))
            hval = num[0:FH] / jnp.maximum(jnp.abs(num[FH:FH + 1, :]), floor[gl:gl + 1, :])
            mu = jnp.mean(hval, axis=0, keepdims=True)
            dlt = hval - mu
            var = jnp.mean(dlt * dlt, axis=0, keepdims=True)
            mo_t = zt_scr[ZT_O + h * FH:ZT_O + (h + 1) * FH, lanes]
            mixt_scr[fr, lanes] = (jax.nn.sigmoid(mo_t) * dlt * lax.rsqrt(var + LN_EPS)
                                   * mnorm_t_ref[fr, :]).astype(BF16)

    for p in range(TT // MCHUNK):
        mchunk_body(p)

    y = (ALPHA * x_ref[0] + _dot(mix_scr[:, 0:YM_OFF], w_out_ref[0:YM_OFF, :])
         + _dot_tn(mixt_scr[...], w_out_ref[YM_OFF:YG_OFF, :])
         + _dot(mix_scr[:, YG_OFF:N_MIX], w_out_ref[YG_OFF:N_MIX, :]))
    xo_ref[0] = _layer_norm_rows(y, lng_ref[...], lnb_ref[...])

    @pl.when(ti == n_t - 1)
    def _():
        for h in range(HEADS):
            ct = c_scr[h]
            tr_scr[...] = jnp.zeros(tr_scr.shape, F32)
            tr_scr[0:FH, 0:FH] = ct[0:FH, :]
            c_o_ref[0, h] = jnp.transpose(tr_scr[...])[0:FH, 0:FH]
            n_o_ref[0, h:h + 1, :] = ct[FH:FH + 1, :]
            s_o_ref[0, h] = s_scr[h * KP:h * KP + GLA_DK, 0:GLA_DV]
        m_o_ref[0] = m_scr[...]


def _layer_spec(shape, layer):
    nd = len(shape)
    return pl.BlockSpec((None,) + tuple(shape[1:]), lambda *_: (layer,) + (0,) * (nd - 1),
                        pipeline_mode=pl.Buffered(1))


def _prompt_mixer(x, consts, layer):
    B, T, _ = x.shape
    out_shape = (
        jax.ShapeDtypeStruct((B, T, D_MODEL), F32),
        jax.ShapeDtypeStruct((B, POOL_BUF, D_POOL), F32),
        jax.ShapeDtypeStruct((B, HEADS, MLSTM_DH, MLSTM_DH), F32),
        jax.ShapeDtypeStruct((B, HEADS, FH), F32),
        jax.ShapeDtypeStruct((B, SUBLANES, LANES), F32),
        jax.ShapeDtypeStruct((B, HEADS, GLA_DK, GLA_DV), F32),
    )
    out_specs = (
        pl.BlockSpec((1, TT, D_MODEL), lambda b, t: (b, t, 0)),
        pl.BlockSpec((1, POOL_BUF, D_POOL), lambda b, t: (b, 0, 0)),
        pl.BlockSpec((1, HEADS, MLSTM_DH, MLSTM_DH), lambda b, t: (b, 0, 0, 0)),
        pl.BlockSpec((1, HEADS, FH), lambda b, t: (b, 0, 0)),
        pl.BlockSpec((1, SUBLANES, LANES), lambda b, t: (b, 0, 0)),
        pl.BlockSpec((1, HEADS, GLA_DK, GLA_DV), lambda b, t: (b, 0, 0, 0)),
    )
    return pl.pallas_call(
        _prompt_mixer_kernel,
        grid=(B, T // TT),
        in_specs=[pl.BlockSpec((1, TT, D_MODEL), lambda b, t: (b, t, 0))]
        + [_layer_spec(c.shape, layer) for c in consts],
        out_specs=out_specs,
        out_shape=out_shape,
        scratch_shapes=[
            pltpu.VMEM((TT, N_Z), F32),
            pltpu.VMEM((ZT_ROWS, TT), F32),
            pltpu.VMEM((TT, N_MIX), BF16),
            pltpu.VMEM((HEADS * FH, TT), BF16),
            pltpu.VMEM((TT + 2 * SUBLANES, D_POOL), F32),
            pltpu.VMEM((HEADS, FH + SUBLANES, FH), F32),
            pltpu.VMEM((SUBLANES, LANES), F32),
            pltpu.VMEM((HEADS * KP, HP), F32),
            pltpu.VMEM((HEADS * CHUNK, CHUNK), jnp.int32),
            pltpu.VMEM((LANES, LANES), F32),
        ],
        compiler_params=pltpu.CompilerParams(
            dimension_semantics=("arbitrary", "arbitrary"), vmem_limit_bytes=VMEM_LIMIT),
        name="prompt_mixer",
    )(x, *consts)


K_SPLIT = 2
CK = MLSTM_DH // K_SPLIT
SK = GLA_DK // K_SPLIT


def _row_of(a, idx):
    sub = lax.broadcasted_iota(jnp.int32, a.shape, 0)
    return jnp.sum(jnp.where(sub == idx, a, 0.0), axis=0, keepdims=True)


def _sample_step_kernel(n_carried, *refs):
    (x_ref, pool_ref, c_ref, n_ref, m_ref, s_ref,
     w_in_ref, w_out_ref, gbias_ref, a2t_ref, abias_ref, mnorm_ref, gnorm_ref,
     pmix_ref, pscale_ref, lng_ref, lnb_ref) = refs[:17]
    (xo_ref, pool_o_ref, c_o_ref, n_o_ref, m_o_ref, s_o_ref,
     zt_scr, mixt_scr, gate_scr, ea_scr, num_scr, oin_scr) = refs[17 + n_carried:]
    h = pl.program_id(0)
    kh = pl.program_id(1)
    B = x_ref.shape[0]
    first = jnp.logical_and(h == 0, kh == 0)
    last = jnp.logical_and(h == HEADS - 1, kh == K_SPLIT - 1)

    @pl.when(first)
    def _():
        mixt_scr[...] = jnp.zeros(mixt_scr.shape, F32)
        xb = x_ref[...].astype(BF16)
        for c0 in range(0, N_Z, LANES):
            zt_scr[c0:c0 + LANES, :] = _dot_nt(w_in_ref[c0:c0 + LANES, :], xb)
        g = zt_scr[GATE_OFF:GATE_OFF + LANES, :]
        gb = g + gbias_ref[...]
        log_i = gb[GATE_I:GATE_I + HEADS]
        log_f = _log_sigmoid(gb[GATE_F:GATE_F + HEADS])
        inter = log_f + m_ref[...]
        m_t = jnp.maximum(inter, log_i)
        m_o_ref[...] = m_t
        gate_scr[0:HEADS, :] = jnp.exp(inter - m_t)
        gate_scr[HEADS:2 * HEADS, :] = jnp.exp(log_i - m_t)
        gate_scr[2 * HEADS:3 * HEADS, :] = jnp.exp(-m_t)
        ya = _dot(a2t_ref[...], g.astype(BF16)) + abias_ref[...]
        ea_scr[...] = jnp.exp(_log_sigmoid(ya) * (1.0 / GLA_TAU))

    gates = gate_scr[...]
    dec = _row_of(gates, h)
    wig = _row_of(gates, h + HEADS)
    floor = _row_of(gates, h + 2 * HEADS)

    def head_rows(off, width, count, start=0):
        return zt_scr[pl.ds(pl.multiple_of(off + h * width + start, SUBLANES), count), :]

    k0 = kh * CK
    q_sl = head_rows(MQ_OFF, FH, CK, k0) * (MLSTM_DH ** -0.5)
    k_sl = head_rows(MK_OFF, HP, CK, k0) * wig
    v_t = head_rows(MV_OFF, FH, MLSTM_DH)
    acc = jnp.zeros((MLSTM_DH, B), F32)
    for k in range(CK):
        c_old = c_ref[k]
        acc = acc + q_sl[k:k + 1, :] * c_old
        c_o_ref[k] = dec * c_old + k_sl[k:k + 1, :] * v_t

    @pl.when(kh == 0)
    def _():
        num_scr[...] = acc

    @pl.when(kh > 0)
    def _():
        num_scr[...] += acc

    @pl.when(kh == K_SPLIT - 1)
    def _():
        q_t = head_rows(MQ_OFF, FH, MLSTM_DH) * (MLSTM_DH ** -0.5)
        k_t = head_rows(MK_OFF, HP, MLSTM_DH)
        n_old = n_ref[...]
        sc = jnp.sum(q_t * k_t, axis=0, keepdims=True) * wig
        num = dec * num_scr[...] + sc * v_t
        nq = dec * jnp.sum(q_t * n_old, axis=0, keepdims=True) + sc
        hval = num / jnp.maximum(jnp.abs(nq), floor)
        n_o_ref[...] = dec * n_old + wig * k_t
        mu = jnp.mean(hval, axis=0, keepdims=True)
        dlt = hval - mu
        var = jnp.mean(dlt * dlt, axis=0, keepdims=True)
        w_norm = mnorm_ref[pl.ds(pl.multiple_of(h * FH, SUBLANES), MLSTM_DH), :]
        mo = head_rows(MO_OFF, FH, MLSTM_DH)
        mixt_scr[pl.ds(pl.multiple_of(YM_OFF + h * FH, SUBLANES), MLSTM_DH), :] = (
            jax.nn.sigmoid(mo) * dlt * lax.rsqrt(var + LN_EPS) * w_norm)

    s0 = kh * SK
    ea_sl = ea_scr[pl.ds(pl.multiple_of(h * KP + s0, SUBLANES), SK), :]
    gq_sl = head_rows(GQ_OFF, KP, SK, s0) * (GLA_DK ** -0.5)
    gk_sl = head_rows(GK_OFF, KP, SK, s0)
    gv_t = head_rows(GV_OFF, HP, GLA_DV)
    gqe_sl = gq_sl * ea_sl
    oacc = jnp.zeros((GLA_DV, B), F32)
    for k in range(SK):
        s_old = s_ref[k]
        oacc = oacc + gqe_sl[k:k + 1, :] * s_old
        s_o_ref[k] = ea_sl[k:k + 1, :] * s_old + gk_sl[k:k + 1, :] * gv_t

    @pl.when(kh == 0)
    def _():
        oin_scr[...] = oacc

    @pl.when(kh > 0)
    def _():
        oin_scr[...] += oacc

    @pl.when(kh == K_SPLIT - 1)
    def _():
        gq_t = head_rows(GQ_OFF, KP, GLA_DK) * (GLA_DK ** -0.5)
        gk_t = head_rows(GK_OFF, KP, GLA_DK)
        qk = jnp.sum(gq_t * gk_t, axis=0, keepdims=True)
        o = oin_scr[...] + qk * gv_t
        ms = jnp.mean(o * o, axis=0, keepdims=True)
        w_norm = gnorm_ref[pl.ds(pl.multiple_of(h * HP, SUBLANES), GLA_DV), :]
        gg = head_rows(GG_OFF, HP, GLA_DV)
        mixt_scr[pl.ds(pl.multiple_of(YG_OFF + h * HP, SUBLANES), GLA_DV), :] = (
            gg * jax.nn.sigmoid(gg) * o * lax.rsqrt(ms + LN_EPS) * w_norm)

    @pl.when(last)
    def _():
        u = jnp.concatenate([jnp.transpose(zt_scr[c0:c0 + LANES, :]) for c0 in range(0, D_POOL, LANES)], axis=1)
        run = u
        sums = {}
        for j in range(1, max(POOL_WINDOWS)):
            run = run + pool_ref[POOL_BUF - j]
            if j + 1 in POOL_WINDOWS:
                sums[j + 1] = run
        grp = lax.broadcasted_iota(jnp.int32, (B, D_POOL), 1) // POOL_GROUP
        wmean = jnp.zeros((B, D_POOL), F32)
        for gidx, w in enumerate(POOL_WINDOWS):
            wmean = jnp.where(grp == gidx, sums[w] / float(min(w, PAST_LEN + 1)), wmean)
        y_pool = _dot((wmean - u).astype(BF16), pmix_ref[...]) * pscale_ref[...]
        for j in range(POOL_BUF - 1):
            pool_o_ref[j] = pool_ref[j + 1]
        pool_o_ref[POOL_BUF - 1] = u
        mix = jnp.concatenate(
            [y_pool] + [jnp.transpose(mixt_scr[c0:c0 + LANES, :]) for c0 in range(YM_OFF, N_MIX, LANES)], axis=1)
        y = ALPHA * x_ref[...] + _dot(mix.astype(BF16), w_out_ref[...])
        xo_ref[...] = _layer_norm_rows(y, lng_ref[...], lnb_ref[...])


def _sample_step(x, states, consts, carried, layer):
    B = x.shape[0]
    out_shape = (jax.ShapeDtypeStruct((B, D_MODEL), F32),) + tuple(
        jax.ShapeDtypeStruct(s.shape, F32) for s in states)
    state_specs = [
        pl.BlockSpec((None, POOL_BUF, B, D_POOL), lambda h, k: (layer, 0, 0, 0)),
        pl.BlockSpec((None, None, CK, MLSTM_DH, B), lambda h, k: (layer, h, k, 0, 0)),
        pl.BlockSpec((None, None, MLSTM_DH, B), lambda h, k: (layer, h, 0, 0)),
        pl.BlockSpec((None, HEADS, B), lambda h, k: (layer, 0, 0)),
        pl.BlockSpec((None, None, SK, GLA_DV, B), lambda h, k: (layer, h, k, 0, 0)),
    ]
    x_spec = pl.BlockSpec((B, D_MODEL), lambda h, k: (0, 0))
    n_in = 1 + len(states) + len(consts)
    return pl.pallas_call(
        functools.partial(_sample_step_kernel, len(carried)),
        grid=(HEADS, K_SPLIT),
        in_specs=[x_spec] + state_specs + [_layer_spec(cst.shape, layer) for cst in consts]
        + [pl.BlockSpec(memory_space=pl.ANY)] * len(carried),
        out_specs=[x_spec] + state_specs,
        out_shape=out_shape,
        input_output_aliases={n_in + i: 1 + i for i in range(len(carried))},
        scratch_shapes=[
            pltpu.VMEM((N_Z, B), F32),
            pltpu.VMEM((N_MIX, B), F32),
            pltpu.VMEM((2 * SUBLANES, B), F32),
            pltpu.VMEM((HEADS * KP, B), F32),
            pltpu.VMEM((MLSTM_DH, B), F32),
            pltpu.VMEM((GLA_DV, B), F32),
        ],
        compiler_params=pltpu.CompilerParams(
            dimension_semantics=("arbitrary", "arbitrary"), vmem_limit_bytes=VMEM_LIMIT),
        name="sample_step",
    )(x, *states, *consts, *carried)


def _ffn_kernel(x_ref, p_ref, wg_ref, wu_ref, wd_ref, wp_ref, wpg_ref, lng_ref, lnb_ref, o_ref):
    x = x_ref[...]
    xb = x.astype(BF16)
    acc = ALPHA * x + _dot(p_ref[...].astype(BF16), wp_ref[...]) * jax.nn.sigmoid(_dot(xb, wpg_ref[...]))
    for c0 in range(0, D_FF, FF_CHUNK):
        c1 = min(c0 + FF_CHUNK, D_FF)
        gate = _dot(xb, wg_ref[:, c0:c1])
        up = _dot(xb, wu_ref[:, c0:c1])
        hid = (gate * jax.nn.sigmoid(gate) * up).astype(BF16)
        acc = acc + _dot(hid, wd_ref[c0:c1, :])
    o_ref[...] = _layer_norm_rows(acc, lng_ref[...], lnb_ref[...])


def _ffn(x, p, consts, layer, tm):
    M = x.shape[0]
    return pl.pallas_call(
        _ffn_kernel,
        grid=(M // tm,),
        in_specs=[pl.BlockSpec((tm, D_MODEL), lambda i: (i, 0)),
                  pl.BlockSpec((None, tm, D_PLE), lambda i: (layer, i, 0))]
        + [_layer_spec(c.shape, layer) for c in consts],
        out_specs=pl.BlockSpec((tm, D_MODEL), lambda i: (i, 0)),
        out_shape=jax.ShapeDtypeStruct((M, D_MODEL), F32),
        compiler_params=pltpu.CompilerParams(
            dimension_semantics=("arbitrary",), vmem_limit_bytes=VMEM_LIMIT),
        name="ffn",
    )(x, p, *consts)


def kernel(x_prompt, x_sample, p_prompt, p_sample, state_pool, state_mlstm_C, state_mlstm_n, state_mlstm_m, state_gla_S, w_in, mlstm_i_bias, mlstm_f_bias, mlstm_norm_w, gla_a2, gla_a_bias, gla_norm_w, pool_mix, pool_scale, w_out, ln1_g, ln1_b, w_gate, w_up, w_down, w_ple, w_ple_gate, ln2_g, ln2_b):
    Bp, Tp, _ = x_prompt.shape
    Bs = x_sample.shape[0]

    w_in_p = _pad_in_proj_t(w_in).astype(BF16)
    w_out_p = _pad_out_proj(w_out).astype(BF16)
    gbias = jnp.zeros((DEPTH, 1, LANES), F32)
    gbias = gbias.at[:, 0, GATE_I:GATE_I + HEADS].set(mlstm_i_bias.astype(F32))
    gbias = gbias.at[:, 0, GATE_F:GATE_F + HEADS].set(mlstm_f_bias.astype(F32))
    a2_p = jnp.zeros((DEPTH, LANES, HEADS * KP), F32)
    a2_p = a2_p.at[:, GATE_A:GATE_A + GLA_RANK, :].set(_pad_heads(gla_a2.astype(F32), GLA_DK, KP)).astype(BF16)
    abias_p = _pad_heads(gla_a_bias.astype(F32), GLA_DK, KP)[:, None, :]
    mnorm_r = mlstm_norm_w.astype(F32)[:, None, :]
    gnorm_p = _pad_heads(gla_norm_w.astype(F32), GLA_DV, HP)[:, None, :]
    n_grp = len(POOL_WINDOWS)
    eye = jnp.eye(n_grp, dtype=F32)
    pmix_bd = (pool_mix.astype(F32)[:, :, :, None, :] * eye[None, :, None, :, None]).reshape(
        DEPTH, D_POOL, D_POOL).astype(BF16)
    pscale = pool_scale.astype(F32)[:, None, :]
    row = lambda a: a.astype(F32)[:, None, :]
    ln1_g_r, ln1_b_r, ln2_g_r, ln2_b_r = row(ln1_g), row(ln1_b), row(ln2_g), row(ln2_b)
    wg_b, wu_b, wd_b = w_gate.astype(BF16), w_up.astype(BF16), w_down.astype(BF16)
    wp_b, wpg_b = w_ple.astype(BF16), w_ple_gate.astype(BF16)

    mnorm_t = jnp.broadcast_to(jnp.transpose(mnorm_r, (0, 2, 1)), (DEPTH, HEADS * FH, MCHUNK))
    mixer_consts = (w_in_p, w_out_p, gbias, a2_p, abias_p, mnorm_t, gnorm_p, pmix_bd, pscale, ln1_g_r, ln1_b_r)
    ffn_consts = (wg_b, wu_b, wd_b, wp_b, wpg_b, ln2_g_r, ln2_b_r)
    xp = x_prompt
    xs = x_sample.reshape(Bs, D_MODEL)
    pp = p_prompt.reshape(DEPTH, Bp * Tp, D_PLE)
    ps = p_sample.reshape(DEPTH, Bs, D_PLE)
    sample_states = (jnp.transpose(state_pool, (0, 2, 1, 3)), jnp.transpose(state_mlstm_C, (0, 2, 3, 4, 1)),
                     jnp.transpose(state_mlstm_n, (0, 2, 3, 1)), jnp.transpose(state_mlstm_m, (0, 2, 1)),
                     jnp.transpose(state_gla_S, (0, 2, 3, 4, 1)))
    col = lambda a: jnp.transpose(a, (0, 2, 1))
    sample_consts = (w_in_p, w_out_p, col(gbias), col(a2_p), col(abias_p), col(mnorm_r), col(gnorm_p),
                     pmix_bd, pscale, ln1_g_r, ln1_b_r)
    outs_p = []
    carried = ()
    for l in range(DEPTH):
        xp, pool_p, c_p, n_p, m_p, s_p = _prompt_mixer(xp, mixer_consts, l)
        xp = _ffn(xp.reshape(Bp * Tp, D_MODEL), pp, ffn_consts, l, tm=512).reshape(Bp, Tp, D_MODEL)
        outs_p.append((pool_p, c_p, n_p, m_p[:, 0, GATE_F:GATE_F + HEADS], s_p))
        xs, *carried = _sample_step(xs, sample_states, sample_consts, tuple(carried), l)
        xs = _ffn(xs, ps, ffn_consts, l, tm=Bs)

    pool_s, c_s, n_s, m_s, s_s = carried
    stack = lambda j: jnp.stack([o[j] for o in outs_p], axis=0)
    return ((xp, xs.reshape(Bs, 1, D_MODEL)) + tuple(stack(j) for j in range(5))
            + (jnp.transpose(pool_s, (0, 2, 1, 3)), jnp.transpose(c_s, (0, 4, 1, 2, 3)),
               jnp.transpose(n_s, (0, 3, 1, 2)), jnp.transpose(m_s, (0, 2, 1)),
               jnp.transpose(s_s, (0, 4, 1, 2, 3))))
```

```python
import functools

import numpy as np
import jax
import jax.numpy as jnp
from jax import lax
from jax.experimental import pallas as pl
from jax.experimental.pallas import tpu as pltpu

F32 = jnp.float32
BF16 = jnp.bfloat16

D_MODEL = 1024
DEPTH = 4
PAST_LEN = 16384
D_POOL = 256
POOL_WINDOWS = (2, 4, 8, 16)
POOL_GROUP = 64
POOL_BUF = 15
D_MLSTM = 384
HEADS = 4
MLSTM_DH = 96
D_GLA = 384
GLA_DV = 96
GLA_DK = 48
GLA_RANK = 16
GLA_TAU = 16.0
D_FF = 2816
D_PLE = 256
ALPHA = (2 * DEPTH) ** 0.25
LN_EPS = 1e-5

LANES = 128
SUBLANES = 8
VMEM_LIMIT = 56 * 1024 * 1024

HP = LANES
FH = MLSTM_DH
KP = 64
U_OFF = 0
MQ_OFF = U_OFF + D_POOL
MK_OFF = MQ_OFF + HEADS * FH
MV_OFF = MK_OFF + HEADS * HP
MO_OFF = MV_OFF + HEADS * FH
GQ_OFF = MO_OFF + HEADS * FH
GK_OFF = GQ_OFF + HEADS * KP
GV_OFF = GK_OFF + HEADS * KP
GG_OFF = GV_OFF + HEADS * HP
GATE_OFF = GG_OFF + HEADS * HP
N_Z = GATE_OFF + LANES
GATE_I = 0
GATE_F = 4
GATE_A = 8
YM_OFF = D_POOL
YG_OFF = YM_OFF + HEADS * FH
N_MIX = YG_OFF + HEADS * HP

TT = 512
CHUNK = 128
MCHUNK = 256
ZT_Q = 0
ZT_V = ZT_Q + HEADS * FH
ZT_O = ZT_V + HEADS * FH
ZT_ROWS = ZT_O + HEADS * FH
PROJ_COLS = 512
FF_CHUNK = 256


def _pad_heads(a, width, padded):
    lead = a.shape[:-1]
    a = a.reshape(lead + (HEADS, width))
    a = jnp.pad(a, [(0, 0)] * len(lead) + [(0, 0), (0, padded - width)])
    return a.reshape(lead + (HEADS * padded,))


def _pad_head_rows(a, width, padded):
    L, _, K = a.shape
    a = jnp.pad(a.reshape(L, HEADS, width, K), ((0, 0), (0, 0), (0, padded - width), (0, 0)))
    return a.reshape(L, HEADS * padded, K)


def _pad_in_proj_t(w_in):
    sizes = (D_POOL, D_MLSTM, D_MLSTM, D_MLSTM, HEADS, HEADS, D_MLSTM,
             HEADS * GLA_DK, HEADS * GLA_DK, D_GLA, GLA_RANK, D_GLA)
    w_t = jnp.transpose(w_in, (0, 2, 1))
    u, mq, mk, mv, mi, mf, mo, gq, gk, gv, ga, gg = jnp.split(w_t, np.cumsum(sizes)[:-1].tolist(), axis=1)
    gate_pad = jnp.zeros((w_t.shape[0], LANES - GATE_A - GLA_RANK, w_t.shape[2]), w_t.dtype)
    wide = lambda a: _pad_head_rows(a, MLSTM_DH, HP)
    narrow = lambda a: _pad_head_rows(a, GLA_DK, KP)
    return jnp.concatenate([u, mq, wide(mk), mv, mo, narrow(gq), narrow(gk), wide(gv),
                            wide(gg), mi, mf, ga, gate_pad], axis=1)


def _pad_out_proj(w_out):
    dense = w_out[:, :D_POOL + D_MLSTM]
    heads = w_out[:, D_POOL + D_MLSTM:].reshape(w_out.shape[0], HEADS, GLA_DV, D_MODEL)
    heads = jnp.pad(heads, ((0, 0), (0, 0), (0, HP - GLA_DV), (0, 0)))
    return jnp.concatenate([dense, heads.reshape(w_out.shape[0], HEADS * HP, D_MODEL)], axis=1)


def _dot(a, b):
    return jnp.dot(a, b, preferred_element_type=F32)


def _dot_nt(a, b):
    return lax.dot_general(a, b, (((1,), (1,)), ((), ())), preferred_element_type=F32)


def _dot_tn(a, b):
    return lax.dot_general(a, b, (((0,), (0,)), ((), ())), preferred_element_type=F32)


def _log_sigmoid(x):
    return jnp.minimum(x, 0.0) - jnp.log1p(jnp.exp(-jnp.abs(x)))


def _cummax_rows(x):
    n = x.shape[0]
    rid = lax.broadcasted_iota(jnp.int32, x.shape, 0)
    sh = 1
    while sh < n:
        x = jnp.maximum(x, jnp.where(rid >= sh, pltpu.roll(x, sh, axis=0), -jnp.inf))
        sh *= 2
    return x


def _cumsum_rows(x):
    n = x.shape[0]
    rid = lax.broadcasted_iota(jnp.int32, x.shape, 0)
    sh = 1
    while sh < n:
        x = x + jnp.where(rid >= sh, pltpu.roll(x, sh, axis=0), 0.0)
        sh *= 2
    return x


def _layer_norm_rows(y, g, b):
    mu = jnp.mean(y, axis=-1, keepdims=True)
    d = y - mu
    var = jnp.mean(d * d, axis=-1, keepdims=True)
    return d * lax.rsqrt(var + LN_EPS) * g + b


def _pool_select(lane, a, b):
    return jnp.where((lane & (LANES - 1)) < POOL_GROUP, a, b)


def _prompt_mixer_kernel(x_ref, w_in_ref, w_out_ref, gbias_ref, a2_ref, abias_ref, mnorm_t_ref,
                         gnorm_ref, pmix_ref, pscale_ref, lng_ref, lnb_ref,
                         xo_ref, pool_o_ref, c_o_ref, n_o_ref, m_o_ref, s_o_ref,
                         z_scr, zt_scr, mix_scr, mixt_scr, ext_scr, c_scr, m_scr, s_scr, lvl_scr, tr_scr):
    ti = pl.program_id(1)
    n_t = pl.num_programs(1)

    @pl.when(ti == 0)
    def _():
        ext_scr[0:2 * SUBLANES, :] = jnp.zeros((2 * SUBLANES, D_POOL), F32)
        c_scr[...] = jnp.zeros(c_scr.shape, F32)
        m_scr[...] = jnp.zeros(m_scr.shape, F32)
        s_scr[...] = jnp.zeros(s_scr.shape, F32)

    xb = x_ref[0].astype(BF16)
    for lo, hi in ((U_OFF, MQ_OFF), (MK_OFF, MV_OFF), (GQ_OFF, N_Z)):
        for c0 in range(lo, hi, PROJ_COLS):
            c1 = min(c0 + PROJ_COLS, hi)
            z_scr[:, c0:c1] = _dot_nt(xb, w_in_ref[c0:c1, :])
    for src, dst in ((MQ_OFF, ZT_Q), (MV_OFF, ZT_V), (MO_OFF, ZT_O)):
        zt_scr[dst:dst + HEADS * FH, :] = _dot_nt(w_in_ref[src:src + HEADS * FH, :], xb)

    hist = 2 * SUBLANES
    u = z_scr[:, U_OFF:U_OFF + D_POOL]
    ext_scr[hist:hist + TT, :] = u
    ext = ext_scr[...]
    e0 = ext[:, 0:LANES]
    e1 = ext[:, LANES:2 * LANES]
    a2s = e0 + pltpu.roll(e0, 1, axis=0)
    a4s = a2s + pltpu.roll(a2s, 2, axis=0)
    b2s = e1 + pltpu.roll(e1, 1, axis=0)
    b4s = b2s + pltpu.roll(b2s, 2, axis=0)
    b8s = b4s + pltpu.roll(b4s, 4, axis=0)
    b16s = b8s + pltpu.roll(b8s, 8, axis=0)
    lane = lax.broadcasted_iota(jnp.int32, (TT, LANES), 1)
    pos1 = (lax.broadcasted_iota(jnp.int32, (TT, LANES), 0) + ti * TT + 1).astype(F32)
    ws0 = _pool_select(lane, a2s[hist:], a4s[hist:])
    ws1 = _pool_select(lane, b8s[hist:], b16s[hist:])
    cnt0 = jnp.minimum(_pool_select(lane, 2.0, 4.0), pos1)
    cnt1 = jnp.minimum(_pool_select(lane, 8.0, 16.0), pos1)
    d0 = ws0 / cnt0 - u[:, 0:LANES]
    d1 = ws1 / cnt1 - u[:, LANES:]
    dpool = jnp.concatenate([d0, d1], axis=1).astype(BF16)
    y_pool = _dot(dpool, pmix_ref[...]) * pscale_ref[...]
    mix_scr[:, 0:D_POOL] = y_pool.astype(BF16)
    tail = ext_scr[TT:TT + hist, :]
    ext_scr[0:hist, :] = tail

    @pl.when(ti == n_t - 1)
    def _():
        pool_o_ref[0] = ext_scr[1:hist, :]

    gbias = gbias_ref[...]
    abias = abias_ref[...]
    trow = lax.broadcasted_iota(jnp.int32, (HEADS * CHUNK, CHUNK), 0) & (CHUNK - 1)
    scol = lax.broadcasted_iota(jnp.int32, (HEADS * CHUNK, CHUNK), 1)
    differ = trow ^ scol
    level = jnp.where(scol > trow, -2, -1)
    for bit in range(CHUNK.bit_length() - 1):
        level = level + jnp.where((scol < trow) & (differ >= (1 << bit)), 1, 0)
    lvl_scr[...] = level

    def chunk_body(c):
        rows = pl.ds(c * CHUNK, CHUNK)
        graw = z_scr[rows, GATE_OFF:GATE_OFF + LANES]

        ya = _dot(graw.astype(BF16), a2_ref[...]) + abias
        la = _log_sigmoid(ya) * (1.0 / GLA_TAU)
        bc = _cumsum_rows(la)
        gq = z_scr[rows, GQ_OFF:GQ_OFF + HEADS * KP] * (GLA_DK ** -0.5)
        gk = z_scr[rows, GK_OFF:GK_OFF + HEADS * KP]
        width = HEADS * KP
        t_id = lax.broadcasted_iota(jnp.int32, (CHUNK, width), 0)
        lane_head = lax.broadcasted_iota(jnp.int32, (CHUNK, width), 1) // KP

        def stack_heads(a):
            return jnp.concatenate([jnp.where(lane_head == h, a, jnp.zeros_like(a)) for h in range(HEADS)], axis=0)

        amat = jnp.where(lvl_scr[...] == -1, _dot_nt(stack_heads(gq.astype(BF16)), gk.astype(BF16)), 0.0)
        half = CHUNK // 2
        while half >= 1:
            blk = 2 * half
            upper = (t_id & half) != 0
            if blk >= SUBLANES:
                ref_rows = jnp.concatenate(
                    [jnp.broadcast_to(bc[b0 + half - 1:b0 + half, :], (blk, width))
                     for b0 in range(0, CHUNK, blk)], axis=0)
            else:
                off = t_id & (blk - 1)
                ref_rows = bc
                for o in range(blk):
                    sh = (o - (half - 1)) % CHUNK
                    if sh != 0:
                        ref_rows = jnp.where(off == o, pltpu.roll(bc, sh, axis=0), ref_rows)
            dlt = bc - ref_rows
            fac = (jnp.where(upper, gq, gk) * jnp.exp(jnp.where(upper, dlt, -dlt))).astype(BF16)
            amat = jnp.where(lvl_scr[...] == half.bit_length() - 1, _dot_nt(stack_heads(fac), fac), amat)
            half //= 2

        o_inter = _dot(stack_heads((gq * jnp.exp(bc)).astype(BF16)), s_scr[...].astype(BF16))
        bc_t = jnp.transpose(bc)
        gk_t = jnp.transpose(gk)
        last_t = bc_t[:, CHUNK - 1:CHUNK]
        ke_t = (gk_t * jnp.exp(last_t - bc_t)).astype(BF16)
        s_dec = jnp.exp(last_t)
        for h in range(HEADS):
            hr = slice(h * CHUNK, (h + 1) * CHUNK)
            gv = z_scr[rows, GV_OFF + h * HP:GV_OFF + (h + 1) * HP].astype(BF16)
            o = o_inter[hr] + _dot(amat[hr].astype(BF16), gv)
            ms = jnp.sum(o * o, axis=1, keepdims=True) * (1.0 / GLA_DV)
            on = o * lax.rsqrt(ms + LN_EPS) * gnorm_ref[:, h * HP:(h + 1) * HP]
            gg = z_scr[rows, GG_OFF + h * HP:GG_OFF + (h + 1) * HP]
            mix_scr[rows, YG_OFF + h * HP:YG_OFF + (h + 1) * HP] = (gg * jax.nn.sigmoid(gg) * on).astype(BF16)
            kr = slice(h * KP, (h + 1) * KP)
            s_scr[kr, :] = s_dec[kr] * s_scr[kr, :] + _dot(ke_t[kr], gv)

    causal = (lax.broadcasted_iota(jnp.int32, (MCHUNK, MCHUNK), 0)
              <= lax.broadcasted_iota(jnp.int32, (MCHUNK, MCHUNK), 1))
    ones_rows = jnp.where(lax.broadcasted_iota(jnp.int32, (SUBLANES, MCHUNK), 0) == 0, 1.0, 0.0)

    def mchunk_body(p):
        for cc in range(MCHUNK // CHUNK):
            chunk_body(p * (MCHUNK // CHUNK) + cc)
        lanes = pl.ds(p * MCHUNK, MCHUNK)
        rows = pl.ds(p * MCHUNK, MCHUNK)
        g_r = z_scr[rows, GATE_OFF:GATE_OFF + LANES] + gbias
        b = _cumsum_rows(_log_sigmoid(g_r))
        i_sh = pltpu.roll(g_r, GATE_F - GATE_I, axis=1)
        r_r = i_sh - b
        m_in = m_scr[0:1, :]
        inter = b + m_in
        m_t = jnp.maximum(inter, b + _cummax_rows(r_r))
        b_end = b[MCHUNK - 1:MCHUNK, :]
        m_end = m_t[MCHUNK - 1:MCHUNK, :]
        m_scr[0:1, :] = m_end
        a = jnp.transpose(b - m_t)
        w_inter = jnp.transpose(jnp.exp(inter - m_t))
        floor = jnp.transpose(jnp.exp(-m_t))
        w_end = jnp.transpose(jnp.exp(b_end - b + i_sh - m_end))
        decay = jnp.exp(b_end + m_in - m_end)
        for h in range(HEADS):
            gl = GATE_F + h
            fr = slice(h * FH, (h + 1) * FH)
            q_t = (zt_scr[ZT_Q + h * FH:ZT_Q + (h + 1) * FH, lanes] * (MLSTM_DH ** -0.5)).astype(BF16)
            v_aug = jnp.concatenate([zt_scr[ZT_V + h * FH:ZT_V + (h + 1) * FH, lanes], ones_rows], axis=0)
            k_r = z_scr[rows, MK_OFF + h * HP:MK_OFF + h * HP + FH].astype(BF16)
            e_t = jnp.exp(jnp.where(causal, a[gl:gl + 1, :] + r_r[:, gl:gl + 1], -jnp.inf))
            s_t = (_dot(k_r, q_t) * e_t).astype(BF16)
            ct = c_scr[h]
            c_scr[h] = decay[:, gl:gl + 1] * ct + _dot((v_aug * w_end[gl:gl + 1, :]).astype(BF16), k_r)
            num = (w_inter[gl:gl + 1, :] * _dot(ct.astype(BF16), q_t)
                   + _dot(v_aug.astype(BF16), s_t))
            hval = num[0:FH] / jnp.maximum(jnp.abs(num[FH:FH + 1, :]), floor[gl:gl + 1, :])
            mu = jnp.mean(hval, axis=0, keepdims=True)
            dlt = hval - mu
            var = jnp.mean(dlt * dlt, axis=0, keepdims=True)
            mo_t = zt_scr[ZT_O + h * FH:ZT_O + (h + 1) * FH, lanes]
            mixt_scr[fr, lanes] = (jax.nn.sigmoid(mo_t) * dlt * lax.rsqrt(var + LN_EPS)
                                   * mnorm_t_ref[fr, :]).astype(BF16)

    for p in range(TT // MCHUNK):
        mchunk_body(p)

    y = (ALPHA * x_ref[0] + _dot(mix_scr[:, 0:YM_OFF], w_out_ref[0:YM_OFF, :])
         + _dot_tn(mixt_scr[...], w_out_ref[YM_OFF:YG_OFF, :])
         + _dot(mix_scr[:, YG_OFF:N_MIX], w_out_ref[YG_OFF:N_MIX, :]))
    xo_ref[0] = _layer_norm_rows(y, lng_ref[...], lnb_ref[...])

    @pl.when(ti == n_t - 1)
    def _():
        for h in range(HEADS):
            ct = c_scr[h]
            tr_scr[...] = jnp.zeros(tr_scr.shape, F32)
            tr_scr[0:FH, 0:FH] = ct[0:FH, :]
            c_o_ref[0, h] = jnp.transpose(tr_scr[...])[0:FH, 0:FH]
            n_o_ref[0, h:h + 1, :] = ct[FH:FH + 1, :]
            s_o_ref[0, h] = s_scr[h * KP:h * KP + GLA_DK, 0:GLA_DV]
        m_o_ref[0] = m_scr[...]


def _layer_spec(shape, layer):
    nd = len(shape)
    return pl.BlockSpec((None,) + tuple(shape[1:]), lambda *_: (layer,) + (0,) * (nd - 1),
                        pipeline_mode=pl.Buffered(1))


def _prompt_mixer(x, consts, layer):
    B, T, _ = x.shape
    out_shape = (
        jax.ShapeDtypeStruct((B, T, D_MODEL), F32),
        jax.ShapeDtypeStruct((B, POOL_BUF, D_POOL), F32),
        jax.ShapeDtypeStruct((B, HEADS, MLSTM_DH, MLSTM_DH), F32),
        jax.ShapeDtypeStruct((B, HEADS, FH), F32),
        jax.ShapeDtypeStruct((B, SUBLANES, LANES), F32),
        jax.ShapeDtypeStruct((B, HEADS, GLA_DK, GLA_DV), F32),
    )
    out_specs = (
        pl.BlockSpec((1, TT, D_MODEL), lambda b, t: (b, t, 0)),
        pl.BlockSpec((1, POOL_BUF, D_POOL), lambda b, t: (b, 0, 0)),
        pl.BlockSpec((1, HEADS, MLSTM_DH, MLSTM_DH), lambda b, t: (b, 0, 0, 0)),
        pl.BlockSpec((1, HEADS, FH), lambda b, t: (b, 0, 0)),
        pl.BlockSpec((1, SUBLANES, LANES), lambda b, t: (b, 0, 0)),
        pl.BlockSpec((1, HEADS, GLA_DK, GLA_DV), lambda b, t: (b, 0, 0, 0)),
    )
    return pl.pallas_call(
        _prompt_mixer_kernel,
        grid=(B, T // TT),
        in_specs=[pl.BlockSpec((1, TT, D_MODEL), lambda b, t: (b, t, 0))]
        + [_layer_spec(c.shape, layer) for c in consts],
        out_specs=out_specs,
        out_shape=out_shape,
        scratch_shapes=[
            pltpu.VMEM((TT, N_Z), F32),
            pltpu.VMEM((ZT_ROWS, TT), F32),
            pltpu.VMEM((TT, N_MIX), BF16),
            pltpu.VMEM((HEADS * FH, TT), BF16),
            pltpu.VMEM((TT + 2 * SUBLANES, D_POOL), F32),
            pltpu.VMEM((HEADS, FH + SUBLANES, FH), F32),
            pltpu.VMEM((SUBLANES, LANES), F32),
            pltpu.VMEM((HEADS * KP, HP), F32),
            pltpu.VMEM((HEADS * CHUNK, CHUNK), jnp.int32),
            pltpu.VMEM((LANES, LANES), F32),
        ],
        compiler_params=pltpu.CompilerParams(
            dimension_semantics=("arbitrary", "arbitrary"), vmem_limit_bytes=VMEM_LIMIT),
        name="prompt_mixer",
    )(x, *consts)


K_SPLIT = 2
CK = MLSTM_DH // K_SPLIT
SK = GLA_DK // K_SPLIT


def _row_of(a, idx):
    sub = lax.broadcasted_iota(jnp.int32, a.shape, 0)
    return jnp.sum(jnp.where(sub == idx, a, 0.0), axis=0, keepdims=True)


def _sample_step_kernel(n_carried, *refs):
    (x_ref, pool_ref, c_ref, n_ref, m_ref, s_ref,
     w_in_ref, w_out_ref, gbias_ref, a2t_ref, abias_ref, mnorm_ref, gnorm_ref,
     pmix_ref, pscale_ref, lng_ref, lnb_ref) = refs[:17]
    (xo_ref, pool_o_ref, c_o_ref, n_o_ref, m_o_ref, s_o_ref,
     zt_scr, mixt_scr, gate_scr, ea_scr, num_scr, oin_scr) = refs[17 + n_carried:]
    h = pl.program_id(0)
    kh = pl.program_id(1)
    B = x_ref.shape[0]
    first = jnp.logical_and(h == 0, kh == 0)
    last = jnp.logical_and(h == HEADS - 1, kh == K_SPLIT - 1)

    @pl.when(first)
    def _():
        mixt_scr[...] = jnp.zeros(mixt_scr.shape, F32)
        xb = x_ref[...].astype(BF16)
        for c0 in range(0, N_Z, LANES):
            zt_scr[c0:c0 + LANES, :] = _dot_nt(w_in_ref[c0:c0 + LANES, :], xb)
        g = zt_scr[GATE_OFF:GATE_OFF + LANES, :]
        gb = g + gbias_ref[...]
        log_i = gb[GATE_I:GATE_I + HEADS]
        log_f = _log_sigmoid(gb[GATE_F:GATE_F + HEADS])
        inter = log_f + m_ref[...]
        m_t = jnp.maximum(inter, log_i)
        m_o_ref[...] = m_t
        gate_scr[0:HEADS, :] = jnp.exp(inter - m_t)
        gate_scr[HEADS:2 * HEADS, :] = jnp.exp(log_i - m_t)
        gate_scr[2 * HEADS:3 * HEADS, :] = jnp.exp(-m_t)
        ya = _dot(a2t_ref[...], g.astype(BF16)) + abias_ref[...]
        ea_scr[...] = jnp.exp(_log_sigmoid(ya) * (1.0 / GLA_TAU))

    gates = gate_scr[...]
    dec = _row_of(gates, h)
    wig = _row_of(gates, h + HEADS)
    floor = _row_of(gates, h + 2 * HEADS)

    def head_rows(off, width, count, start=0):
        return zt_scr[pl.ds(pl.multiple_of(off + h * width + start, SUBLANES), count), :]

    k0 = kh * CK
    q_sl = head_rows(MQ_OFF, FH, CK, k0) * (MLSTM_DH ** -0.5)
    k_sl = head_rows(MK_OFF, HP, CK, k0) * wig
    v_t = head_rows(MV_OFF, FH, MLSTM_DH)
    acc = jnp.zeros((MLSTM_DH, B), F32)
    for k in range(CK):
        c_old = c_ref[k]
        acc = acc + q_sl[k:k + 1, :] * c_old
        c_o_ref[k] = dec * c_old + k_sl[k:k + 1, :] * v_t

    @pl.when(kh == 0)
    def _():
        num_scr[...] = acc

    @pl.when(kh > 0)
    def _():
        num_scr[...] += acc

    @pl.when(kh == K_SPLIT - 1)
    def _():
        q_t = head_rows(MQ_OFF, FH, MLSTM_DH) * (MLSTM_DH ** -0.5)
        k_t = head_rows(MK_OFF, HP, MLSTM_DH)
        n_old = n_ref[...]
        sc = jnp.sum(q_t * k_t, axis=0, keepdims=True) * wig
        num = dec * num_scr[...] + sc * v_t
        nq = dec * jnp.sum(q_t * n_old, axis=0, keepdims=True) + sc
        hval = num / jnp.maximum(jnp.abs(nq), floor)
        n_o_ref[...] = dec * n_old + wig * k_t
        mu = jnp.mean(hval, axis=0, keepdims=True)
        dlt = hval - mu
        var = jnp.mean(dlt * dlt, axis=0, keepdims=True)
        w_norm = mnorm_ref[pl.ds(pl.multiple_of(h * FH, SUBLANES), MLSTM_DH), :]
        mo = head_rows(MO_OFF, FH, MLSTM_DH)
        mixt_scr[pl.ds(pl.multiple_of(YM_OFF + h * FH, SUBLANES), MLSTM_DH), :] = (
            jax.nn.sigmoid(mo) * dlt * lax.rsqrt(var + LN_EPS) * w_norm)

    s0 = kh * SK
    ea_sl = ea_scr[pl.ds(pl.multiple_of(h * KP + s0, SUBLANES), SK), :]
    gq_sl = head_rows(GQ_OFF, KP, SK, s0) * (GLA_DK ** -0.5)
    gk_sl = head_rows(GK_OFF, KP, SK, s0)
    gv_t = head_rows(GV_OFF, HP, GLA_DV)
    gqe_sl = gq_sl * ea_sl
    oacc = jnp.zeros((GLA_DV, B), F32)
    for k in range(SK):
        s_old = s_ref[k]
        oacc = oacc + gqe_sl[k:k + 1, :] * s_old
        s_o_ref[k] = ea_sl[k:k + 1, :] * s_old + gk_sl[k:k + 1, :] * gv_t

    @pl.when(kh == 0)
    def _():
        oin_scr[...] = oacc

    @pl.when(kh > 0)
    def _():
        oin_scr[...] += oacc

    @pl.when(kh == K_SPLIT - 1)
    def _():
        gq_t = head_rows(GQ_OFF, KP, GLA_DK) * (GLA_DK ** -0.5)
        gk_t = head_rows(GK_OFF, KP, GLA_DK)
        qk = jnp.sum(gq_t * gk_t, axis=0, keepdims=True)
        o = oin_scr[...] + qk * gv_t
        ms = jnp.mean(o * o, axis=0, keepdims=True)
        w_norm = gnorm_ref[pl.ds(pl.multiple_of(h * HP, SUBLANES), GLA_DV), :]
        gg = head_rows(GG_OFF, HP, GLA_DV)
        mixt_scr[pl.ds(pl.multiple_of(YG_OFF + h * HP, SUBLANES), GLA_DV), :] = (
            gg * jax.nn.sigmoid(gg) * o * lax.rsqrt(ms + LN_EPS) * w_norm)

    @pl.when(last)
    def _():
        u = jnp.concatenate([jnp.transpose(zt_scr[c0:c0 + LANES, :]) for c0 in range(0, D_POOL, LANES)], axis=1)
        run = u
        sums = {}
        for j in range(1, max(POOL_WINDOWS)):
            run = run + pool_ref[POOL_BUF - j]
            if j + 1 in POOL_WINDOWS:
                sums[j + 1] = run
        grp = lax.broadcasted_iota(jnp.int32, (B, D_POOL), 1) // POOL_GROUP
        wmean = jnp.zeros((B, D_POOL), F32)
        for gidx, w in enumerate(POOL_WINDOWS):
            wmean = jnp.where(grp == gidx, sums[w] / float(min(w, PAST_LEN + 1)), wmean)
        y_pool = _dot((wmean - u).astype(BF16), pmix_ref[...]) * pscale_ref[...]
        for j in range(POOL_BUF - 1):
            pool_o_ref[j] = pool_ref[j + 1]
        pool_o_ref[POOL_BUF - 1] = u
        mix = jnp.concatenate(
            [y_pool] + [jnp.transpose(mixt_scr[c0:c0 + LANES, :]) for c0 in range(YM_OFF, N_MIX, LANES)], axis=1)
        y = ALPHA * x_ref[...] + _dot(mix.astype(BF16), w_out_ref[...])
        xo_ref[...] = _layer_norm_rows(y, lng_ref[...], lnb_ref[...])


def _sample_step(x, states, consts, carried, layer):
    B = x.shape[0]
    out_shape = (jax.ShapeDtypeStruct((B, D_MODEL), F32),) + tuple(
        jax.ShapeDtypeStruct(s.shape, F32) for s in states)
    state_specs = [
        pl.BlockSpec((None, POOL_BUF, B, D_POOL), lambda h, k: (layer, 0, 0, 0)),
        pl.BlockSpec((None, None, CK, MLSTM_DH, B), lambda h, k: (layer, h, k, 0, 0)),
        pl.BlockSpec((None, None, MLSTM_DH, B), lambda h, k: (layer, h, 0, 0)),
        pl.BlockSpec((None, HEADS, B), lambda h, k: (layer, 0, 0)),
        pl.BlockSpec((None, None, SK, GLA_DV, B), lambda h, k: (layer, h, k, 0, 0)),
    ]
    x_spec = pl.BlockSpec((B, D_MODEL), lambda h, k: (0, 0))
    n_in = 1 + len(states) + len(consts)
    return pl.pallas_call(
        functools.partial(_sample_step_kernel, len(carried)),
        grid=(HEADS, K_SPLIT),
        in_specs=[x_spec] + state_specs + [_layer_spec(cst.shape, layer) for cst in consts]
        + [pl.BlockSpec(memory_space=pl.ANY)] * len(carried),
        out_specs=[x_spec] + state_specs,
        out_shape=out_shape,
        input_output_aliases={n_in + i: 1 + i for i in range(len(carried))},
        scratch_shapes=[
            pltpu.VMEM((N_Z, B), F32),
            pltpu.VMEM((N_MIX, B), F32),
            pltpu.VMEM((2 * SUBLANES, B), F32),
            pltpu.VMEM((HEADS * KP, B), F32),
            pltpu.VMEM((MLSTM_DH, B), F32),
            pltpu.VMEM((GLA_DV, B), F32),
        ],
        compiler_params=pltpu.CompilerParams(
            dimension_semantics=("arbitrary", "arbitrary"), vmem_limit_bytes=VMEM_LIMIT),
        name="sample_step",
    )(x, *states, *consts, *carried)


def _ffn_kernel(x_ref, p_ref, wg_ref, wu_ref, wd_ref, wp_ref, wpg_ref, lng_ref, lnb_ref, o_ref):
    x = x_ref[...]
    xb = x.astype(BF16)
    acc = ALPHA * x + _dot(p_ref[...].astype(BF16), wp_ref[...]) * jax.nn.sigmoid(_dot(xb, wpg_ref[...]))
    for c0 in range(0, D_FF, FF_CHUNK):
        c1 = min(c0 + FF_CHUNK, D_FF)
        gate = _dot(xb, wg_ref[:, c0:c1])
        up = _dot(xb, wu_ref[:, c0:c1])
        hid = (gate * jax.nn.sigmoid(gate) * up).astype(BF16)
        acc = acc + _dot(hid, wd_ref[c0:c1, :])
    o_ref[...] = _layer_norm_rows(acc, lng_ref[...], lnb_ref[...])


def _ffn(x, p, consts, layer, tm):
    M = x.shape[0]
    return pl.pallas_call(
        _ffn_kernel,
        grid=(M // tm,),
        in_specs=[pl.BlockSpec((tm, D_MODEL), lambda i: (i, 0)),
                  pl.BlockSpec((None, tm, D_PLE), lambda i: (layer, i, 0))]
        + [_layer_spec(c.shape, layer) for c in consts],
        out_specs=pl.BlockSpec((tm, D_MODEL), lambda i: (i, 0)),
        out_shape=jax.ShapeDtypeStruct((M, D_MODEL), F32),
        compiler_params=pltpu.CompilerParams(
            dimension_semantics=("arbitrary",), vmem_limit_bytes=VMEM_LIMIT),
        name="ffn",
    )(x, p, *consts)


def kernel(x_prompt, x_sample, p_prompt, p_sample, state_pool, state_mlstm_C, state_mlstm_n, state_mlstm_m, state_gla_S, w_in, mlstm_i_bias, mlstm_f_bias, mlstm_norm_w, gla_a2, gla_a_bias, gla_norm_w, pool_mix, pool_scale, w_out, ln1_g, ln1_b, w_gate, w_up, w_down, w_ple, w_ple_gate, ln2_g, ln2_b):
    Bp, Tp, _ = x_prompt.shape
    Bs = x_sample.shape[0]

    w_in_p = _pad_in_proj_t(w_in).astype(BF16)
    w_out_p = _pad_out_proj(w_out).astype(BF16)
    gbias = jnp.zeros((DEPTH, 1, LANES), F32)
    gbias = gbias.at[:, 0, GATE_I:GATE_I + HEADS].set(mlstm_i_bias.astype(F32))
    gbias = gbias.at[:, 0, GATE_F:GATE_F + HEADS].set(mlstm_f_bias.astype(F32))
    a2_p = jnp.zeros((DEPTH, LANES, HEADS * KP), F32)
    a2_p = a2_p.at[:, GATE_A:GATE_A + GLA_RANK, :].set(_pad_heads(gla_a2.astype(F32), GLA_DK, KP)).astype(BF16)
    abias_p = _pad_heads(gla_a_bias.astype(F32), GLA_DK, KP)[:, None, :]
    mnorm_r = mlstm_norm_w.astype(F32)[:, None, :]
    gnorm_p = _pad_heads(gla_norm_w.astype(F32), GLA_DV, HP)[:, None, :]
    n_grp = len(POOL_WINDOWS)
    eye = jnp.eye(n_grp, dtype=F32)
    pmix_bd = (pool_mix.astype(F32)[:, :, :, None, :] * eye[None, :, None, :, None]).reshape(
        DEPTH, D_POOL, D_POOL).astype(BF16)
    pscale = pool_scale.astype(F32)[:, None, :]
    row = lambda a: a.astype(F32)[:, None, :]
    ln1_g_r, ln1_b_r, ln2_g_r, ln2_b_r = row(ln1_g), row(ln1_b), row(ln2_g), row(ln2_b)
    wg_b, wu_b, wd_b = w_gate.astype(BF16), w_up.astype(BF16), w_down.astype(BF16)
    wp_b, wpg_b = w_ple.astype(BF16), w_ple_gate.astype(BF16)

    mnorm_t = jnp.broadcast_to(jnp.transpose(mnorm_r, (0, 2, 1)), (DEPTH, HEADS * FH, MCHUNK))
    mixer_consts = (w_in_p, w_out_p, gbias, a2_p, abias_p, mnorm_t, gnorm_p, pmix_bd, pscale, ln1_g_r, ln1_b_r)
    ffn_consts = (wg_b, wu_b, wd_b, wp_b, wpg_b, ln2_g_r, ln2_b_r)
    xp = x_prompt
    xs = x_sample.reshape(Bs, D_MODEL)
    pp = p_prompt.reshape(DEPTH, Bp * Tp, D_PLE)
    ps = p_sample.reshape(DEPTH, Bs, D_PLE)
    sample_states = (jnp.transpose(state_pool, (0, 2, 1, 3)), jnp.transpose(state_mlstm_C, (0, 2, 3, 4, 1)),
                     jnp.transpose(state_mlstm_n, (0, 2, 3, 1)), jnp.transpose(state_mlstm_m, (0, 2, 1)),
                     jnp.transpose(state_gla_S, (0, 2, 3, 4, 1)))
    col = lambda a: jnp.transpose(a, (0, 2, 1))
    sample_consts = (w_in_p, w_out_p, col(gbias), col(a2_p), col(abias_p), col(mnorm_r), col(gnorm_p),
                     pmix_bd, pscale, ln1_g_r, ln1_b_r)
    outs_p = []
    carried = ()
    for l in range(DEPTH):
        xp, pool_p, c_p, n_p, m_p, s_p = _prompt_mixer(xp, mixer_consts, l)
        xp = _ffn(xp.reshape(Bp * Tp, D_MODEL), pp, ffn_consts, l, tm=512).reshape(Bp, Tp, D_MODEL)
        outs_p.append((pool_p, c_p, n_p, m_p[:, 0, GATE_F:GATE_F + HEADS], s_p))
        xs, *carried = _sample_step(xs, sample_states, sample_consts, tuple(carried), l)
        xs = _ffn(xs, ps, ffn_consts, l, tm=Bs)

    pool_s, c_s, n_s, m_s, s_s = carried
    stack = lambda j: jnp.stack([o[j] for o in outs_p], axis=0)
    return ((xp, xs.reshape(Bs, 1, D_MODEL)) + tuple(stack(j) for j in range(5))
            + (jnp.transpose(pool_s, (0, 2, 1, 3)), jnp.transpose(c_s, (0, 4, 1, 2, 3)),
               jnp.transpose(n_s, (0, 3, 1, 2)), jnp.transpose(m_s, (0, 2, 1)),
               jnp.transpose(s_s, (0, 4, 1, 2, 3))))
```

```python
import functools

import numpy as np
import jax
import jax.numpy as jnp
from jax import lax
from jax.experimental import pallas as pl
from jax.experimental.pallas import tpu as pltpu

F32 = jnp.float32
BF16 = jnp.bfloat16

D_MODEL = 1024
DEPTH = 4
PAST_LEN = 16384
D_POOL = 256
POOL_WINDOWS = (2, 4, 8, 16)
POOL_GROUP = 64
POOL_BUF = 15
D_MLSTM = 384
HEADS = 4
MLSTM_DH = 96
D_GLA = 384
GLA_DV = 96
GLA_DK = 48
GLA_RANK = 16
GLA_TAU = 16.0
D_MIX = 1024
D_FF = 2816
D_PLE = 256
ALPHA = (2 * DEPTH) ** 0.25
LN_EPS = 1e-5

LANES = 128
SUBLANES = 8
VMEM_LIMIT = 56 * 1024 * 1024

HP = LANES
FH = MLSTM_DH
KP = 64
U_OFF = 0
MQ_OFF = U_OFF + D_POOL
MK_OFF = MQ_OFF + HEADS * FH
MV_OFF = MK_OFF + HEADS * HP
MO_OFF = MV_OFF + HEADS * FH
GQ_OFF = MO_OFF + HEADS * FH
GK_OFF = GQ_OFF + HEADS * KP
GV_OFF = GK_OFF + HEADS * KP
GG_OFF = GV_OFF + HEADS * HP
GATE_OFF = GG_OFF + HEADS * HP
N_Z = GATE_OFF + LANES
GATE_I = 0
GATE_F = 4
GATE_A = 8
YM_OFF = D_POOL
YG_OFF = YM_OFF + HEADS * FH
N_MIX = YG_OFF + HEADS * HP

TT = 512
CHUNK = 128
MCHUNK = 256
ZT_Q = 0
ZT_V = ZT_Q + HEADS * FH
ZT_O = ZT_V + HEADS * FH
ZT_ROWS = ZT_O + HEADS * FH
FF_CHUNK = 256


def _pad_heads(a, width, padded):
    lead = a.shape[:-1]
    a = a.reshape(lead + (HEADS, width))
    a = jnp.pad(a, [(0, 0)] * len(lead) + [(0, 0), (0, padded - width)])
    return a.reshape(lead + (HEADS * padded,))


def _pad_head_rows(a, width, padded):
    L, _, K = a.shape
    a = jnp.pad(a.reshape(L, HEADS, width, K), ((0, 0), (0, 0), (0, padded - width), (0, 0)))
    return a.reshape(L, HEADS * padded, K)


def _pad_in_proj_t(w_in):
    sizes = (D_POOL, D_MLSTM, D_MLSTM, D_MLSTM, HEADS, HEADS, D_MLSTM,
             HEADS * GLA_DK, HEADS * GLA_DK, D_GLA, GLA_RANK, D_GLA)
    w_t = jnp.transpose(w_in, (0, 2, 1))
    u, mq, mk, mv, mi, mf, mo, gq, gk, gv, ga, gg = jnp.split(w_t, np.cumsum(sizes)[:-1].tolist(), axis=1)
    gate_pad = jnp.zeros((w_t.shape[0], LANES - GATE_A - GLA_RANK, w_t.shape[2]), w_t.dtype)
    wide = lambda a: _pad_head_rows(a, MLSTM_DH, HP)
    narrow = lambda a: _pad_head_rows(a, GLA_DK, KP)
    return jnp.concatenate([u, mq, wide(mk), mv, mo, narrow(gq), narrow(gk), wide(gv),
                            wide(gg), mi, mf, ga, gate_pad], axis=1)


def _pad_out_proj(w_out):
    dense = w_out[:, :D_POOL + D_MLSTM]
    heads = w_out[:, D_POOL + D_MLSTM:].reshape(w_out.shape[0], HEADS, GLA_DV, D_MODEL)
    heads = jnp.pad(heads, ((0, 0), (0, 0), (0, HP - GLA_DV), (0, 0)))
    return jnp.concatenate([dense, heads.reshape(w_out.shape[0], HEADS * HP, D_MODEL)], axis=1)


def _dot(a, b):
    return jnp.dot(a, b, preferred_element_type=F32)


def _dot_nt(a, b):
    return lax.dot_general(a, b, (((1,), (1,)), ((), ())), preferred_element_type=F32)


def _dot_tn(a, b):
    return lax.dot_general(a, b, (((0,), (0,)), ((), ())), preferred_element_type=F32)


def _log_sigmoid(x):
    return jnp.minimum(x, 0.0) - jnp.log1p(jnp.exp(-jnp.abs(x)))


def _cummax_rows(x):
    n = x.shape[0]
    rid = lax.broadcasted_iota(jnp.int32, x.shape, 0)
    sh = 1
    while sh < n:
        x = jnp.maximum(x, jnp.where(rid >= sh, pltpu.roll(x, sh, axis=0), -jnp.inf))
        sh *= 2
    return x


def _cumsum_rows(x):
    n = x.shape[0]
    rid = lax.broadcasted_iota(jnp.int32, x.shape, 0)
    sh = 1
    while sh < n:
        x = x + jnp.where(rid >= sh, pltpu.roll(x, sh, axis=0), 0.0)
        sh *= 2
    return x


def _layer_norm_rows(y, g, b):
    mu = jnp.mean(y, axis=-1, keepdims=True)
    d = y - mu
    var = jnp.mean(d * d, axis=-1, keepdims=True)
    return d * lax.rsqrt(var + LN_EPS) * g + b


def _pool_select(lane, a, b):
    return jnp.where((lane & (LANES - 1)) < POOL_GROUP, a, b)


def _prompt_mixer_kernel(x_ref, w_in_ref, w_out_ref, gbias_ref, a2_ref, abias_ref, mnorm_t_ref,
                         gnorm_ref, pmix_ref, pscale_ref, lng_ref, lnb_ref,
                         xo_ref, pool_o_ref, c_o_ref, n_o_ref, m_o_ref, s_o_ref,
                         z_scr, zt_scr, mix_scr, mixt_scr, ext_scr, c_scr, m_scr, s_scr, lvl_scr, tr_scr):
    ti = pl.program_id(1)
    n_t = pl.num_programs(1)

    @pl.when(ti == 0)
    def _():
        ext_scr[0:2 * SUBLANES, :] = jnp.zeros((2 * SUBLANES, D_POOL), F32)
        c_scr[...] = jnp.zeros(c_scr.shape, F32)
        m_scr[...] = jnp.zeros(m_scr.shape, F32)
        s_scr[...] = jnp.zeros(s_scr.shape, F32)

    xb = x_ref[0].astype(BF16)
    for lo, hi in ((U_OFF, MQ_OFF), (MK_OFF, MV_OFF), (GQ_OFF, N_Z)):
        for c0 in range(lo, hi, 512):
            c1 = min(c0 + 512, hi)
            z_scr[:, c0:c1] = _dot_nt(xb, w_in_ref[c0:c1, :])
    for src, dst in ((MQ_OFF, ZT_Q), (MV_OFF, ZT_V), (MO_OFF, ZT_O)):
        zt_scr[dst:dst + HEADS * FH, :] = _dot_nt(w_in_ref[src:src + HEADS * FH, :], xb)

    hist = 2 * SUBLANES
    u = z_scr[:, U_OFF:U_OFF + D_POOL]
    ext_scr[hist:hist + TT, :] = u
    ext = ext_scr[...]
    e0 = ext[:, 0:LANES]
    e1 = ext[:, LANES:2 * LANES]
    a2s = e0 + pltpu.roll(e0, 1, axis=0)
    a4s = a2s + pltpu.roll(a2s, 2, axis=0)
    b2s = e1 + pltpu.roll(e1, 1, axis=0)
    b4s = b2s + pltpu.roll(b2s, 2, axis=0)
    b8s = b4s + pltpu.roll(b4s, 4, axis=0)
    b16s = b8s + pltpu.roll(b8s, 8, axis=0)
    lane = lax.broadcasted_iota(jnp.int32, (TT, LANES), 1)
    pos1 = (lax.broadcasted_iota(jnp.int32, (TT, LANES), 0) + ti * TT + 1).astype(F32)
    ws0 = _pool_select(lane, a2s[hist:], a4s[hist:])
    ws1 = _pool_select(lane, b8s[hist:], b16s[hist:])
    cnt0 = jnp.minimum(_pool_select(lane, 2.0, 4.0), pos1)
    cnt1 = jnp.minimum(_pool_select(lane, 8.0, 16.0), pos1)
    d0 = ws0 / cnt0 - u[:, 0:LANES]
    d1 = ws1 / cnt1 - u[:, LANES:]
    dpool = jnp.concatenate([d0, d1], axis=1).astype(BF16)
    y_pool = _dot(dpool, pmix_ref[...]) * pscale_ref[...]
    mix_scr[:, 0:D_POOL] = y_pool.astype(BF16)
    tail = ext_scr[TT:TT + hist, :]
    ext_scr[0:hist, :] = tail

    @pl.when(ti == n_t - 1)
    def _():
        pool_o_ref[0] = ext_scr[1:hist, :]

    gbias = gbias_ref[...]
    abias = abias_ref[...]
    trow = lax.broadcasted_iota(jnp.int32, (2 * CHUNK, CHUNK), 0) & (CHUNK - 1)
    scol = lax.broadcasted_iota(jnp.int32, (2 * CHUNK, CHUNK), 1)
    differ = trow ^ scol
    level = jnp.where(scol > trow, -2, -1)
    for bit in range(CHUNK.bit_length() - 1):
        level = level + jnp.where((scol < trow) & (differ >= (1 << bit)), 1, 0)
    lvl_scr[...] = level

    def chunk_body(c):
        rows = pl.ds(c * CHUNK, CHUNK)
        graw = z_scr[rows, GATE_OFF:GATE_OFF + LANES]

        ya = _dot(graw.astype(BF16), a2_ref[...]) + abias
        la = _log_sigmoid(ya) * (1.0 / GLA_TAU)
        bc = _cumsum_rows(la)
        gq = z_scr[rows, GQ_OFF:GQ_OFF + HEADS * KP] * (GLA_DK ** -0.5)
        gk = z_scr[rows, GK_OFF:GK_OFF + HEADS * KP]
        width = HEADS * KP
        t_id = lax.broadcasted_iota(jnp.int32, (CHUNK, width), 0)

        lane_sub = lax.broadcasted_iota(jnp.int32, (CHUNK, LANES), 1) // KP
        lvl = lvl_scr[...]

        def stack_pair(a):
            return jnp.concatenate([jnp.where(lane_sub == j, a, jnp.zeros_like(a)) for j in range(2)], axis=0)

        facs = []
        half = CHUNK // 2
        while half >= 1:
            blk = 2 * half
            upper = (t_id & half) != 0
            if blk >= SUBLANES:
                ref_rows = jnp.concatenate(
                    [jnp.broadcast_to(bc[b0 + half - 1:b0 + half, :], (blk, width))
                     for b0 in range(0, CHUNK, blk)], axis=0)
            else:
                off = t_id & (blk - 1)
                ref_rows = bc
                for o in range(blk):
                    sh = (o - (half - 1)) % CHUNK
                    if sh != 0:
                        ref_rows = jnp.where(off == o, pltpu.roll(bc, sh, axis=0), ref_rows)
            dlt = bc - ref_rows
            facs.append((half.bit_length() - 1,
                         (jnp.where(upper, gq, gk) * jnp.exp(jnp.where(upper, dlt, -dlt))).astype(BF16)))
            half //= 2

        gqb = gq.astype(BF16)
        gkb = gk.astype(BF16)
        qi = (gq * jnp.exp(bc)).astype(BF16)
        bc_t = jnp.transpose(bc)
        gk_t = jnp.transpose(gk)
        last_t = bc_t[:, CHUNK - 1:CHUNK]
        ke_t = (gk_t * jnp.exp(last_t - bc_t)).astype(BF16)
        s_dec = jnp.exp(last_t)
        for g in range(HEADS // 2):
            tl = slice(g * LANES, (g + 1) * LANES)
            amat = jnp.where(lvl == -1, _dot_nt(stack_pair(gqb[:, tl]), gkb[:, tl]), 0.0)
            for k, fac in facs:
                amat = jnp.where(lvl == k, _dot_nt(stack_pair(fac[:, tl]), fac[:, tl]), amat)
            o_inter = _dot(stack_pair(qi[:, tl]), s_scr[tl, :].astype(BF16))
            for j in range(2):
                h = 2 * g + j
                hr = slice(j * CHUNK, (j + 1) * CHUNK)
                gv = z_scr[rows, GV_OFF + h * HP:GV_OFF + (h + 1) * HP].astype(BF16)
                o = o_inter[hr] + _dot(amat[hr].astype(BF16), gv)
                ms = jnp.sum(o * o, axis=1, keepdims=True) * (1.0 / GLA_DV)
                on = o * lax.rsqrt(ms + LN_EPS) * gnorm_ref[:, h * HP:(h + 1) * HP]
                gg = z_scr[rows, GG_OFF + h * HP:GG_OFF + (h + 1) * HP]
                mix_scr[rows, YG_OFF + h * HP:YG_OFF + (h + 1) * HP] = (gg * jax.nn.sigmoid(gg) * on).astype(BF16)
                kr = slice(h * KP, (h + 1) * KP)
                s_scr[kr, :] = s_dec[kr] * s_scr[kr, :] + _dot(ke_t[kr], gv)

    causal = (lax.broadcasted_iota(jnp.int32, (MCHUNK, MCHUNK), 0)
              <= lax.broadcasted_iota(jnp.int32, (MCHUNK, MCHUNK), 1))
    ones_rows = jnp.where(lax.broadcasted_iota(jnp.int32, (SUBLANES, MCHUNK), 0) == 0, 1.0, 0.0)

    def mchunk_body(p):
        for cc in range(MCHUNK // CHUNK):
            chunk_body(p * (MCHUNK // CHUNK) + cc)
        lanes = pl.ds(p * MCHUNK, MCHUNK)
        rows = pl.ds(p * MCHUNK, MCHUNK)
        g_r = z_scr[rows, GATE_OFF:GATE_OFF + LANES] + gbias
        b = _cumsum_rows(_log_sigmoid(g_r))
        i_sh = pltpu.roll(g_r, GATE_F - GATE_I, axis=1)
        r_r = i_sh - b
        m_in = m_scr[0:1, :]
        inter = b + m_in
        m_t = jnp.maximum(inter, b + _cummax_rows(r_r))
        b_end = b[MCHUNK - 1:MCHUNK, :]
        m_end = m_t[MCHUNK - 1:MCHUNK, :]
        m_scr[0:1, :] = m_end
        a = jnp.transpose(b - m_t)
        w_inter = jnp.transpose(jnp.exp(inter - m_t))
        floor = jnp.transpose(jnp.exp(-m_t))
        w_end = jnp.transpose(jnp.exp(b_end - b + i_sh - m_end))
        decay = jnp.exp(b_end + m_in - m_end)
        for h in range(HEADS):
            gl = GATE_F + h
            fr = slice(h * FH, (h + 1) * FH)
            q_t = (zt_scr[ZT_Q + h * FH:ZT_Q + (h + 1) * FH, lanes] * (MLSTM_DH ** -0.5)).astype(BF16)
            v_aug = jnp.concatenate([zt_scr[ZT_V + h * FH:ZT_V + (h + 1) * FH, lanes], ones_rows], axis=0)
            k_r = z_scr[rows, MK_OFF + h * HP:MK_OFF + h * HP + FH].astype(BF16)
            e_t = jnp.exp(jnp.where(causal, a[gl:gl + 1, :] + r_r[:, gl:gl + 1], -jnp.inf))
            s_t = (_dot(k_r, q_t) * e_t).astype(BF16)
            ct = c_scr[h]
            c_scr[h] = decay[:, gl:gl + 1] * ct + _dot((v_aug * w_end[gl:gl + 1, :]).astype(BF16), k_r)
            num = (w_inter[gl:gl + 1, :] * _dot(ct.astype(BF16), q_t)
                   + _dot(v_aug.astype(BF16), s_t))
            hval = num[0:FH] / jnp.maximum(jnp.abs(num[FH:FH + 1, :]), floor[gl:gl + 1, :])
            mu = jnp.mean(hval, axis=0, keepdims=True)
            dlt = hval - mu
            var = jnp.mean(dlt * dlt, axis=0, keepdims=True)
            mo_t = zt_scr[ZT_O + h * FH:ZT_O + (h + 1) * FH, lanes]
            mixt_scr[fr, lanes] = (jax.nn.sigmoid(mo_t) * dlt * lax.rsqrt(var + LN_EPS)
                                   * mnorm_t_ref[fr, :]).astype(BF16)

    for p in range(TT // MCHUNK):
        mchunk_body(p)

    y = (ALPHA * x_ref[0] + _dot(mix_scr[:, 0:YM_OFF], w_out_ref[0:YM_OFF, :])
         + _dot_tn(mixt_scr[...], w_out_ref[YM_OFF:YG_OFF, :])
         + _dot(mix_scr[:, YG_OFF:N_MIX], w_out_ref[YG_OFF:N_MIX, :]))
    xo_ref[0] = _layer_norm_rows(y, lng_ref[...], lnb_ref[...])

    @pl.when(ti == n_t - 1)
    def _():
        for h in range(HEADS):
            ct = c_scr[h]
            tr_scr[...] = jnp.zeros(tr_scr.shape, F32)
            tr_scr[0:FH, 0:FH] = ct[0:FH, :]
            c_o_ref[0, h] = jnp.transpose(tr_scr[...])[0:FH, 0:FH]
            n_o_ref[0, h:h + 1, :] = ct[FH:FH + 1, :]
            s_o_ref[0, h] = s_scr[h * KP:h * KP + GLA_DK, 0:GLA_DV]
        m_o_ref[0] = m_scr[...]


def _layer_spec(shape, layer):
    nd = len(shape)
    return pl.BlockSpec((None,) + tuple(shape[1:]), lambda *_: (layer,) + (0,) * (nd - 1),
                        pipeline_mode=pl.Buffered(1))


def _prompt_mixer(x, consts, layer):
    B, T, _ = x.shape
    out_shape = (
        jax.ShapeDtypeStruct((B, T, D_MODEL), F32),
        jax.ShapeDtypeStruct((B, POOL_BUF, D_POOL), F32),
        jax.ShapeDtypeStruct((B, HEADS, MLSTM_DH, MLSTM_DH), F32),
        jax.ShapeDtypeStruct((B, HEADS, FH), F32),
        jax.ShapeDtypeStruct((B, SUBLANES, LANES), F32),
        jax.ShapeDtypeStruct((B, HEADS, GLA_DK, GLA_DV), F32),
    )
    out_specs = (
        pl.BlockSpec((1, TT, D_MODEL), lambda b, t: (b, t, 0)),
        pl.BlockSpec((1, POOL_BUF, D_POOL), lambda b, t: (b, 0, 0)),
        pl.BlockSpec((1, HEADS, MLSTM_DH, MLSTM_DH), lambda b, t: (b, 0, 0, 0)),
        pl.BlockSpec((1, HEADS, FH), lambda b, t: (b, 0, 0)),
        pl.BlockSpec((1, SUBLANES, LANES), lambda b, t: (b, 0, 0)),
        pl.BlockSpec((1, HEADS, GLA_DK, GLA_DV), lambda b, t: (b, 0, 0, 0)),
    )
    return pl.pallas_call(
        _prompt_mixer_kernel,
        grid=(B, T // TT),
        in_specs=[pl.BlockSpec((1, TT, D_MODEL), lambda b, t: (b, t, 0))]
        + [_layer_spec(c.shape, layer) for c in consts],
        out_specs=out_specs,
        out_shape=out_shape,
        scratch_shapes=[
            pltpu.VMEM((TT, N_Z), F32),
            pltpu.VMEM((ZT_ROWS, TT), F32),
            pltpu.VMEM((TT, N_MIX), BF16),
            pltpu.VMEM((HEADS * FH, TT), BF16),
            pltpu.VMEM((TT + 2 * SUBLANES, D_POOL), F32),
            pltpu.VMEM((HEADS, FH + SUBLANES, FH), F32),
            pltpu.VMEM((SUBLANES, LANES), F32),
            pltpu.VMEM((HEADS * KP, HP), F32),
            pltpu.VMEM((2 * CHUNK, CHUNK), jnp.int32),
            pltpu.VMEM((LANES, LANES), F32),
        ],
        compiler_params=pltpu.CompilerParams(
            dimension_semantics=("arbitrary", "arbitrary"), vmem_limit_bytes=VMEM_LIMIT),
        name="prompt_mixer",
    )(x, *consts)


K_SPLIT = 2
CK = MLSTM_DH // K_SPLIT
SK = GLA_DK // K_SPLIT


def _row_of(a, idx):
    sub = lax.broadcasted_iota(jnp.int32, a.shape, 0)
    return jnp.sum(jnp.where(sub == idx, a, 0.0), axis=0, keepdims=True)


def _sample_step_kernel(n_carried, *refs):
    (x_ref, pool_ref, c_ref, n_ref, m_ref, s_ref,
     w_in_ref, w_out_ref, gbias_ref, a2t_ref, abias_ref, mnorm_ref, gnorm_ref,
     pmix_ref, pscale_ref, lng_ref, lnb_ref) = refs[:17]
    (xo_ref, pool_o_ref, c_o_ref, n_o_ref, m_o_ref, s_o_ref,
     zt_scr, mixt_scr, gate_scr, ea_scr, num_scr, oin_scr) = refs[17 + n_carried:]
    h = pl.program_id(0)
    kh = pl.program_id(1)
    B = x_ref.shape[0]
    first = jnp.logical_and(h == 0, kh == 0)
    last = jnp.logical_and(h == HEADS - 1, kh == K_SPLIT - 1)

    @pl.when(first)
    def _():
        mixt_scr[...] = jnp.zeros(mixt_scr.shape, F32)
        xb = x_ref[...].astype(BF16)
        for c0 in range(0, N_Z, LANES):
            zt_scr[c0:c0 + LANES, :] = _dot_nt(w_in_ref[c0:c0 + LANES, :], xb)
        g = zt_scr[GATE_OFF:GATE_OFF + LANES, :]
        gb = g + gbias_ref[...]
        log_i = gb[GATE_I:GATE_I + HEADS]
        log_f = _log_sigmoid(gb[GATE_F:GATE_F + HEADS])
        inter = log_f + m_ref[...]
        m_t = jnp.maximum(inter, log_i)
        m_o_ref[...] = m_t
        gate_scr[0:HEADS, :] = jnp.exp(inter - m_t)
        gate_scr[HEADS:2 * HEADS, :] = jnp.exp(log_i - m_t)
        gate_scr[2 * HEADS:3 * HEADS, :] = jnp.exp(-m_t)
        ya = _dot(a2t_ref[...], g.astype(BF16)) + abias_ref[...]
        ea_scr[...] = jnp.exp(_log_sigmoid(ya) * (1.0 / GLA_TAU))

    gates = gate_scr[...]
    dec = _row_of(gates, h)
    wig = _row_of(gates, h + HEADS)
    floor = _row_of(gates, h + 2 * HEADS)

    def head_rows(off, width, count, start=0):
        return zt_scr[pl.ds(pl.multiple_of(off + h * width + start, SUBLANES), count), :]

    k0 = kh * CK
    q_sl = head_rows(MQ_OFF, FH, CK, k0) * (MLSTM_DH ** -0.5)
    k_sl = head_rows(MK_OFF, HP, CK, k0) * wig
    v_t = head_rows(MV_OFF, FH, MLSTM_DH)
    acc = jnp.zeros((MLSTM_DH, B), F32)
    for k in range(CK):
        c_old = c_ref[k]
        acc = acc + q_sl[k:k + 1, :] * c_old
        c_o_ref[k] = dec * c_old + k_sl[k:k + 1, :] * v_t

    @pl.when(kh == 0)
    def _():
        num_scr[...] = acc

    @pl.when(kh > 0)
    def _():
        num_scr[...] += acc

    @pl.when(kh == K_SPLIT - 1)
    def _():
        q_t = head_rows(MQ_OFF, FH, MLSTM_DH) * (MLSTM_DH ** -0.5)
        k_t = head_rows(MK_OFF, HP, MLSTM_DH)
        n_old = n_ref[...]
        sc = jnp.sum(q_t * k_t, axis=0, keepdims=True) * wig
        num = dec * num_scr[...] + sc * v_t
        nq = dec * jnp.sum(q_t * n_old, axis=0, keepdims=True) + sc
        hval = num / jnp.maximum(jnp.abs(nq), floor)
        n_o_ref[...] = dec * n_old + wig * k_t
        mu = jnp.mean(hval, axis=0, keepdims=True)
        dlt = hval - mu
        var = jnp.mean(dlt * dlt, axis=0, keepdims=True)
        w_norm = mnorm_ref[pl.ds(pl.multiple_of(h * FH, SUBLANES), MLSTM_DH), :]
        mo = head_rows(MO_OFF, FH, MLSTM_DH)
        mixt_scr[pl.ds(pl.multiple_of(YM_OFF + h * FH, SUBLANES), MLSTM_DH), :] = (
            jax.nn.sigmoid(mo) * dlt * lax.rsqrt(var + LN_EPS) * w_norm)

    s0 = kh * SK
    ea_sl = ea_scr[pl.ds(pl.multiple_of(h * KP + s0, SUBLANES), SK), :]
    gq_sl = head_rows(GQ_OFF, KP, SK, s0) * (GLA_DK ** -0.5)
    gk_sl = head_rows(GK_OFF, KP, SK, s0)
    gv_t = head_rows(GV_OFF, HP, GLA_DV)
    gqe_sl = gq_sl * ea_sl
    oacc = jnp.zeros((GLA_DV, B), F32)
    for k in range(SK):
        s_old = s_ref[k]
        oacc = oacc + gqe_sl[k:k + 1, :] * s_old
        s_o_ref[k] = ea_sl[k:k + 1, :] * s_old + gk_sl[k:k + 1, :] * gv_t

    @pl.when(kh == 0)
    def _():
        oin_scr[...] = oacc

    @pl.when(kh > 0)
    def _():
        oin_scr[...] += oacc

    @pl.when(kh == K_SPLIT - 1)
    def _():
        gq_t = head_rows(GQ_OFF, KP, GLA_DK) * (GLA_DK ** -0.5)
        gk_t = head_rows(GK_OFF, KP, GLA_DK)
        qk = jnp.sum(gq_t * gk_t, axis=0, keepdims=True)
        o = oin_scr[...] + qk * gv_t
        ms = jnp.mean(o * o, axis=0, keepdims=True)
        w_norm = gnorm_ref[pl.ds(pl.multiple_of(h * HP, SUBLANES), GLA_DV), :]
        gg = head_rows(GG_OFF, HP, GLA_DV)
        mixt_scr[pl.ds(pl.multiple_of(YG_OFF + h * HP, SUBLANES), GLA_DV), :] = (
            gg * jax.nn.sigmoid(gg) * o * lax.rsqrt(ms + LN_EPS) * w_norm)

    @pl.when(last)
    def _():
        u = jnp.concatenate([jnp.transpose(zt_scr[c0:c0 + LANES, :]) for c0 in range(0, D_POOL, LANES)], axis=1)
        run = u
        sums = {}
        for j in range(1, max(POOL_WINDOWS)):
            run = run + pool_ref[POOL_BUF - j]
            if j + 1 in POOL_WINDOWS:
                sums[j + 1] = run
        grp = lax.broadcasted_iota(jnp.int32, (B, D_POOL), 1) // POOL_GROUP
        wmean = jnp.zeros((B, D_POOL), F32)
        for gidx, w in enumerate(POOL_WINDOWS):
            wmean = jnp.where(grp == gidx, sums[w] / float(min(w, PAST_LEN + 1)), wmean)
        y_pool = _dot((wmean - u).astype(BF16), pmix_ref[...]) * pscale_ref[...]
        for j in range(POOL_BUF - 1):
            pool_o_ref[j] = pool_ref[j + 1]
        pool_o_ref[POOL_BUF - 1] = u
        mix = jnp.concatenate(
            [y_pool] + [jnp.transpose(mixt_scr[c0:c0 + LANES, :]) for c0 in range(YM_OFF, N_MIX, LANES)], axis=1)
        y = ALPHA * x_ref[...] + _dot(mix.astype(BF16), w_out_ref[...])
        xo_ref[...] = _layer_norm_rows(y, lng_ref[...], lnb_ref[...])


def _sample_step(x, states, consts, carried, layer):
    B = x.shape[0]
    out_shape = (jax.ShapeDtypeStruct((B, D_MODEL), F32),) + tuple(
        jax.ShapeDtypeStruct(s.shape, F32) for s in states)
    state_specs = [
        pl.BlockSpec((None, POOL_BUF, B, D_POOL), lambda h, k: (layer, 0, 0, 0)),
        pl.BlockSpec((None, None, CK, MLSTM_DH, B), lambda h, k: (layer, h, k, 0, 0)),
        pl.BlockSpec((None, None, MLSTM_DH, B), lambda h, k: (layer, h, 0, 0)),
        pl.BlockSpec((None, HEADS, B), lambda h, k: (layer, 0, 0)),
        pl.BlockSpec((None, None, SK, GLA_DV, B), lambda h, k: (layer, h, k, 0, 0)),
    ]
    x_spec = pl.BlockSpec((B, D_MODEL), lambda h, k: (0, 0))
    n_in = 1 + len(states) + len(consts)
    return pl.pallas_call(
        functools.partial(_sample_step_kernel, len(carried)),
        grid=(HEADS, K_SPLIT),
        in_specs=[x_spec] + state_specs + [_layer_spec(cst.shape, layer) for cst in consts]
        + [pl.BlockSpec(memory_space=pl.ANY)] * len(carried),
        out_specs=[x_spec] + state_specs,
        out_shape=out_shape,
        input_output_aliases={n_in + i: 1 + i for i in range(len(carried))},
        scratch_shapes=[
            pltpu.VMEM((N_Z, B), F32),
            pltpu.VMEM((N_MIX, B), F32),
            pltpu.VMEM((2 * SUBLANES, B), F32),
            pltpu.VMEM((HEADS * KP, B), F32),
            pltpu.VMEM((MLSTM_DH, B), F32),
            pltpu.VMEM((GLA_DV, B), F32),
        ],
        compiler_params=pltpu.CompilerParams(
            dimension_semantics=("arbitrary", "arbitrary"), vmem_limit_bytes=VMEM_LIMIT),
        name="sample_step",
    )(x, *states, *consts, *carried)


def _ffn_kernel(x_ref, p_ref, wg_ref, wu_ref, wd_ref, wp_ref, wpg_ref, lng_ref, lnb_ref, o_ref):
    x = x_ref[...]
    xb = x.astype(BF16)
    acc = ALPHA * x + _dot(p_ref[...].astype(BF16), wp_ref[...]) * jax.nn.sigmoid(_dot(xb, wpg_ref[...]))
    for c0 in range(0, D_FF, FF_CHUNK):
        c1 = min(c0 + FF_CHUNK, D_FF)
        gate = _dot(xb, wg_ref[:, c0:c1])
        up = _dot(xb, wu_ref[:, c0:c1])
        hid = (gate * jax.nn.sigmoid(gate) * up).astype(BF16)
        acc = acc + _dot(hid, wd_ref[c0:c1, :])
    o_ref[...] = _layer_norm_rows(acc, lng_ref[...], lnb_ref[...])


def _ffn(x, p, consts, layer, tm):
    M = x.shape[0]
    return pl.pallas_call(
        _ffn_kernel,
        grid=(M // tm,),
        in_specs=[pl.BlockSpec((tm, D_MODEL), lambda i: (i, 0)),
                  pl.BlockSpec((None, tm, D_PLE), lambda i: (layer, i, 0))]
        + [_layer_spec(c.shape, layer) for c in consts],
        out_specs=pl.BlockSpec((tm, D_MODEL), lambda i: (i, 0)),
        out_shape=jax.ShapeDtypeStruct((M, D_MODEL), F32),
        compiler_params=pltpu.CompilerParams(
            dimension_semantics=("arbitrary",), vmem_limit_bytes=VMEM_LIMIT),
        name="ffn",
    )(x, p, *consts)


def kernel(x_prompt, x_sample, p_prompt, p_sample, state_pool, state_mlstm_C, state_mlstm_n, state_mlstm_m, state_gla_S, w_in, mlstm_i_bias, mlstm_f_bias, mlstm_norm_w, gla_a2, gla_a_bias, gla_norm_w, pool_mix, pool_scale, w_out, ln1_g, ln1_b, w_gate, w_up, w_down, w_ple, w_ple_gate, ln2_g, ln2_b):
    Bp, Tp, _ = x_prompt.shape
    Bs = x_sample.shape[0]

    w_in_p = _pad_in_proj_t(w_in).astype(BF16)
    w_out_p = _pad_out_proj(w_out).astype(BF16)
    gbias = jnp.zeros((DEPTH, 1, LANES), F32)
    gbias = gbias.at[:, 0, GATE_I:GATE_I + HEADS].set(mlstm_i_bias.astype(F32))
    gbias = gbias.at[:, 0, GATE_F:GATE_F + HEADS].set(mlstm_f_bias.astype(F32))
    a2_p = jnp.zeros((DEPTH, LANES, HEADS * KP), F32)
    a2_p = a2_p.at[:, GATE_A:GATE_A + GLA_RANK, :].set(_pad_heads(gla_a2.astype(F32), GLA_DK, KP)).astype(BF16)
    abias_p = _pad_heads(gla_a_bias.astype(F32), GLA_DK, KP)[:, None, :]
    mnorm_r = mlstm_norm_w.astype(F32)[:, None, :]
    gnorm_p = _pad_heads(gla_norm_w.astype(F32), GLA_DV, HP)[:, None, :]
    n_grp = len(POOL_WINDOWS)
    eye = jnp.eye(n_grp, dtype=F32)
    pmix_bd = (pool_mix.astype(F32)[:, :, :, None, :] * eye[None, :, None, :, None]).reshape(
        DEPTH, D_POOL, D_POOL).astype(BF16)
    pscale = pool_scale.astype(F32)[:, None, :]
    row = lambda a: a.astype(F32)[:, None, :]
    ln1_g_r, ln1_b_r, ln2_g_r, ln2_b_r = row(ln1_g), row(ln1_b), row(ln2_g), row(ln2_b)
    wg_b, wu_b, wd_b = w_gate.astype(BF16), w_up.astype(BF16), w_down.astype(BF16)
    wp_b, wpg_b = w_ple.astype(BF16), w_ple_gate.astype(BF16)

    mnorm_t = jnp.broadcast_to(jnp.transpose(mnorm_r, (0, 2, 1)), (DEPTH, HEADS * FH, MCHUNK))
    mixer_consts = (w_in_p, w_out_p, gbias, a2_p, abias_p, mnorm_t, gnorm_p, pmix_bd, pscale, ln1_g_r, ln1_b_r)
    ffn_consts = (wg_b, wu_b, wd_b, wp_b, wpg_b, ln2_g_r, ln2_b_r)
    xp = x_prompt
    xs = x_sample.reshape(Bs, D_MODEL)
    pp = p_prompt.reshape(DEPTH, Bp * Tp, D_PLE)
    ps = p_sample.reshape(DEPTH, Bs, D_PLE)
    sample_states = (jnp.transpose(state_pool, (0, 2, 1, 3)), jnp.transpose(state_mlstm_C, (0, 2, 3, 4, 1)),
                     jnp.transpose(state_mlstm_n, (0, 2, 3, 1)), jnp.transpose(state_mlstm_m, (0, 2, 1)),
                     jnp.transpose(state_gla_S, (0, 2, 3, 4, 1)))
    col = lambda a: jnp.transpose(a, (0, 2, 1))
    sample_consts = (w_in_p, w_out_p, col(gbias), col(a2_p), col(abias_p), col(mnorm_r), col(gnorm_p),
                     pmix_bd, pscale, ln1_g_r, ln1_b_r)
    outs_p = []
    carried = ()
    for l in range(DEPTH):
        xp, pool_p, c_p, n_p, m_p, s_p = _prompt_mixer(xp, mixer_consts, l)
        xp = _ffn(xp.reshape(Bp * Tp, D_MODEL), pp, ffn_consts, l, tm=512).reshape(Bp, Tp, D_MODEL)
        outs_p.append((pool_p, c_p, n_p, m_p[:, 0, GATE_F:GATE_F + HEADS], s_p))
        xs, *carried = _sample_step(xs, sample_states, sample_consts, tuple(carried), l)
        xs = _ffn(xs, ps, ffn_consts, l, tm=Bs)

    pool_s, c_s, n_s, m_s, s_s = carried
    stack = lambda j: jnp.stack([o[j] for o in outs_p], axis=0)
    return ((xp, xs.reshape(Bs, 1, D_MODEL)) + tuple(stack(j) for j in range(5))
            + (jnp.transpose(pool_s, (0, 2, 1, 3)), jnp.transpose(c_s, (0, 4, 1, 2, 3)),
               jnp.transpose(n_s, (0, 3, 1, 2)), jnp.transpose(m_s, (0, 2, 1)),
               jnp.transpose(s_s, (0, 4, 1, 2, 3))))
```
